```python
import math
import jax, jax.numpy as jnp
from jax import lax
import numpy as np

D_MODEL = 1024
BATCH = 32
SEQ = 256
DEPTH = 2
DEC_BATCH = 8
DEC_SEQ = 4096
PAST_LEN = 512

GRID_W = 64
HEAD_DIM = 64
MIX_WIDTH = D_MODEL
GROUP_WIDTH = MIX_WIDTH // 4
N_HEADS_A = GROUP_WIDTH // HEAD_DIM
KV_HEADS_A = 2
Q_PER_KV = N_HEADS_A // KV_HEADS_A
N_HEADS_C = GROUP_WIDTH // HEAD_DIM
N_FOURIER_GROUPS = GROUP_WIDTH // HEAD_DIM
FOURIER_GROUP_DIM = HEAD_DIM
SHORT_CONV = 3
DELTA_CHUNK = 64
Q_BLOCK = 128
ROPE_THETA = 10000.0
D_FF = 2816
N_MOD = 9
EPS = 1e-6
SPLIT_SIZES = (N_HEADS_A * HEAD_DIM, KV_HEADS_A * HEAD_DIM, KV_HEADS_A * HEAD_DIM,
               GROUP_WIDTH, GROUP_WIDTH, GROUP_WIDTH,
               GROUP_WIDTH, GROUP_WIDTH, GROUP_WIDTH, GROUP_WIDTH,
               N_HEADS_C, N_HEADS_C, N_HEADS_C, N_HEADS_C,
               GROUP_WIDTH)
PROJ_WIDTH = sum(SPLIT_SIZES)

kernel_name = 'hybrid_diffusion_parallel_groups_step'


def split_cols(t):
    offs = [int(o) for o in np.cumsum(SPLIT_SIZES)[:-1]]
    return jnp.split(t, offs, axis=-1)


def rms_norm(x, g):
    x32 = x.astype(jnp.float32)
    y = x32 * lax.rsqrt(jnp.mean(x32 * x32, axis=-1, keepdims=True) + EPS)
    return (y * g.astype(jnp.float32)).astype(x.dtype)


def l2_norm(x):
    return x * lax.rsqrt(jnp.sum(x * x, axis=-1, keepdims=True) + EPS)


def rope_2d(x):
    L = x.shape[1]
    rows = L // GRID_W
    r, cl = jnp.meshgrid(jnp.arange(rows), jnp.arange(GRID_W), indexing='ij')
    pos_r = r.reshape(-1).astype(jnp.float32)
    pos_c = cl.reshape(-1).astype(jnp.float32)
    quarter = HEAD_DIM // 4
    half = HEAD_DIM // 2
    inv_freq = ROPE_THETA ** (-jnp.arange(quarter, dtype=jnp.float32) / quarter)

    def rot(xh, pos):
        ang = pos[:, None] * inv_freq[None, :]
        cos = jnp.cos(ang)[None, :, None, :].astype(x.dtype)
        sin = jnp.sin(ang)[None, :, None, :].astype(x.dtype)
        x1, x2 = xh[..., :quarter], xh[..., quarter:]
        return jnp.concatenate([x1 * cos - x2 * sin, x2 * cos + x1 * sin], axis=-1)

    return jnp.concatenate([rot(x[..., :half], pos_r), rot(x[..., half:], pos_c)], axis=-1)


def block_attention(q, k, v):
    Bn, Lq = q.shape[0], q.shape[1]
    nb = Lq // Q_BLOCK
    qb = jnp.moveaxis(q.reshape(Bn, nb, Q_BLOCK, KV_HEADS_A, Q_PER_KV, HEAD_DIM), 1, 0)
    scale = HEAD_DIM ** -0.5

    def one_block(qi):
        s = jnp.einsum('bqhgd,bkhd->bhgqk', qi, k).astype(jnp.float32) * scale
        p = jax.nn.softmax(s, axis=-1).astype(v.dtype)
        return jnp.einsum('bhgqk,bkhd->bqhgd', p, v)

    o = lax.map(one_block, qb)
    return jnp.moveaxis(o, 0, 1).reshape(Bn, Lq, N_HEADS_A * HEAD_DIM)


def depthwise_conv(u, w):
    K = w.shape[0]
    pad = K // 2
    L = u.shape[1]
    up = jnp.pad(u, ((0, 0), (pad, pad), (0, 0)))
    y = up[:, 0:L] * w[0]
    for j in range(1, K):
        y = y + up[:, j:j + L] * w[j]
    return y


def chunk_gated_delta(q, k, v, g, beta, s0):
    Bn, H, L, dk = q.shape
    dv = v.shape[-1]
    C = DELTA_CHUNK
    n = L // C
    q = q.reshape(Bn, H, n, C, dk)
    k = k.reshape(Bn, H, n, C, dk)
    v = v.reshape(Bn, H, n, C, dv)
    g = g.reshape(Bn, H, n, C)
    beta = beta.reshape(Bn, H, n, C)
    gc = jnp.cumsum(g, axis=-1)
    idx = jnp.arange(C)
    incl = idx[:, None] >= idx[None, :]
    strict = idx[:, None] > idx[None, :]
    decay = jnp.exp(jnp.where(incl, gc[..., :, None] - gc[..., None, :], -jnp.inf))
    kb = k * beta[..., None]
    vb = v * beta[..., None]
    lmat = jnp.where(strict, jnp.einsum('bhncd,bhnjd->bhncj', kb, k) * decay, 0.0)
    eye = jnp.eye(C, dtype=jnp.float32)
    t_inv = lax.linalg.triangular_solve(eye + lmat, jnp.broadcast_to(eye, lmat.shape),
                                        left_side=True, lower=True, unit_diagonal=True)
    u = t_inv @ vb
    w = t_inv @ (kb * jnp.exp(gc)[..., None])
    aqk = jnp.einsum('bhncd,bhnjd->bhncj', q, k) * decay
    qg = q * jnp.exp(gc)[..., None]
    kdec = k * jnp.exp(gc[..., -1:] - gc)[..., None]
    glast = jnp.exp(gc[..., -1])

    def step(S, xs):
        u_c, w_c, qg_c, aqk_c, kdec_c, gl = xs
        v_new = u_c - w_c @ S
        o_c = qg_c @ S + aqk_c @ v_new
        S = S * gl[..., None, None] + jnp.swapaxes(kdec_c, -1, -2) @ v_new
        return S, o_c

    xs = tuple(jnp.moveaxis(a, 2, 0) for a in (u, w, qg, aqk, kdec, glast))
    S, o = lax.scan(step, s0, xs)
    o = jnp.moveaxis(o, 0, 2).reshape(Bn, H, L, dv)
    return o, S


def gated_deltanet(qc, kc, vc, zc, beta_f, beta_b, a_f, a_b, conv_w, a_log, dt_bias, norm_g, s0):
    f32 = jnp.float32
    Bn, L, _ = qc.shape
    qkv = jax.nn.silu(depthwise_conv(jnp.concatenate([qc, kc, vc], axis=-1), conv_w).astype(f32))
    q, k, v = jnp.split(qkv, 3, axis=-1)

    def heads(t):
        return jnp.swapaxes(t.reshape(Bn, L, N_HEADS_C, HEAD_DIM), 1, 2)

    q = l2_norm(heads(q)) * HEAD_DIM ** -0.5
    k = l2_norm(heads(k))
    v = heads(v)
    decay_rate = jnp.exp(a_log.astype(f32))
    dtb = dt_bias.astype(f32)

    def gates(b_raw, a_raw, d):
        beta = jax.nn.sigmoid(b_raw.astype(f32))
        g = -decay_rate[d] * jax.nn.softplus(a_raw.astype(f32) + dtb[d])
        return jnp.swapaxes(beta, 1, 2), jnp.swapaxes(g, 1, 2)

    bf, gf = gates(beta_f, a_f, 0)
    bb, gb = gates(beta_b, a_b, 1)
    s0 = s0.astype(f32)
    o_f, s_f = chunk_gated_delta(q, k, v, gf, bf, s0[:, 0])

    def rev(t):
        return jnp.flip(t, axis=2)

    o_b, s_b = chunk_gated_delta(rev(q), rev(k), rev(v), rev(gb), rev(bb), s0[:, 1])
    o = jnp.swapaxes(o_f + rev(o_b), 1, 2)
    z = zc.astype(f32).reshape(Bn, L, N_HEADS_C, HEAD_DIM)
    o = rms_norm(o, norm_g) * jax.nn.silu(z)
    return o.reshape(Bn, L, GROUP_WIDTH).astype(qc.dtype), jnp.stack([s_f, s_b], axis=1)


def fourier_mix(xd):
    Bn, L, _ = xd.shape
    t = xd.astype(jnp.float32).reshape(Bn, L, N_FOURIER_GROUPS, FOURIER_GROUP_DIM)
    y = jnp.fft.fft2(t, axes=(1, 3), norm='ortho').real
    return y.reshape(Bn, L, GROUP_WIDTH).astype(xd.dtype)


def swiglu(h, w1, w3, w2):
    return (jax.nn.silu(h @ w1) * (h @ w3)) @ w2


def token_mix(h, lp, ctx):
    Bn, L, _ = h.shape
    (qa, ka, va, gate_b, gate_c, xb, qc, kc, vc, zc,
     beta_f, beta_b, a_f, a_b, xd) = split_cols(h @ lp['w_in'])
    qa = rms_norm(qa.reshape(Bn, L, N_HEADS_A, HEAD_DIM), lp['q_norm'])
    ka = rms_norm(ka.reshape(Bn, L, KV_HEADS_A, HEAD_DIM), lp['k_norm'])
    va = va.reshape(Bn, L, KV_HEADS_A, HEAD_DIM)
    if ctx is None:
        k_all, v_all = ka, va
        s0 = jnp.zeros((Bn, 2, N_HEADS_C, HEAD_DIM, HEAD_DIM), jnp.float32)
    else:
        k_ctx, v_ctx, s0 = ctx
        qa = rope_2d(qa)
        k_all = jnp.concatenate([jnp.swapaxes(k_ctx, 1, 2).astype(ka.dtype), rope_2d(ka)], axis=1)
        v_all = jnp.concatenate([jnp.swapaxes(v_ctx, 1, 2).astype(va.dtype), va], axis=1)
    o_a = block_attention(qa.reshape(Bn, L, KV_HEADS_A, Q_PER_KV, HEAD_DIM), k_all, v_all)
    o_b = gate_b * depthwise_conv(gate_c * xb, lp['conv_b_w'])
    o_c, s_end = gated_deltanet(qc, kc, vc, zc, beta_f, beta_b, a_f, a_b, lp['conv_c_w'],
                                lp['delta_a_log'], lp['delta_dt_bias'], lp['delta_norm'], s0)
    o_d = fourier_mix(xd)
    out = jnp.concatenate([o_a, o_b, o_c, o_d], axis=-1) @ lp['w_out']
    return out, jnp.swapaxes(ka, 1, 2), jnp.swapaxes(va, 1, 2), s_end.astype(h.dtype)


def trunk_layer(x, cond, lp, ctx):
    mod = (jax.nn.silu(cond) @ lp['mod_w'] + lp['mod_b'])[:, None, :]
    sh1, sc1, g1, shm, scm, gm, sh2, sc2, g2 = jnp.split(mod, N_MOD, axis=-1)
    h = rms_norm(x, lp['norm_ffn1']) * (1.0 + sc1) + sh1
    x = x + 0.5 * g1 * swiglu(h, lp['ffn1_w1'], lp['ffn1_w3'], lp['ffn1_w2'])
    h = rms_norm(x, lp['norm_mix']) * (1.0 + scm) + shm
    out, k_new, v_new, s_new = token_mix(h, lp, ctx)
    x = x + gm * out
    h = rms_norm(x, lp['norm_ffn2']) * (1.0 + sc2) + sh2
    x = x + 0.5 * g2 * swiglu(h, lp['ffn2_w1'], lp['ffn2_w3'], lp['ffn2_w2'])
    return x, k_new, v_new, s_new


def setup_inputs(seed: int = 0) -> dict:
    key = jax.random.key(seed)
    ks = jax.random.split(key, 32)
    f32 = jnp.float32

    def nrm(k, shape, scale):
        return jax.random.normal(k, shape, f32) * scale

    def gain(k, shape):
        return 1.0 + 0.02 * jax.random.normal(k, shape, f32)

    dt = jnp.exp(jax.random.uniform(ks[25], (DEPTH, 2, N_HEADS_C), f32,
                                    math.log(1e-3), math.log(0.1)))
    return {
        'x_prompt': nrm(ks[0], (BATCH, SEQ, D_MODEL), 1.0),
        'x_sample': nrm(ks[1], (DEC_BATCH, DEC_SEQ, D_MODEL), 1.0),
        'c': nrm(ks[2], (DEC_BATCH, D_MODEL), 1.0),
        'cache_k': nrm(ks[3], (DEC_BATCH, DEPTH, KV_HEADS_A, PAST_LEN, HEAD_DIM), 1.0),
        'cache_v': nrm(ks[4], (DEC_BATCH, DEPTH, KV_HEADS_A, PAST_LEN, HEAD_DIM), 1.0),
        'state_delta': nrm(ks[5], (DEC_BATCH, DEPTH, 2, N_HEADS_C, HEAD_DIM, HEAD_DIM), 0.1),
        'c_ctx': nrm(ks[6], (D_MODEL,), 1.0),
        'mod_w': nrm(ks[7], (DEPTH, D_MODEL, N_MOD * D_MODEL), D_MODEL ** -0.5),
        'mod_b': nrm(ks[8], (DEPTH, N_MOD * D_MODEL), 0.02),
        'norm_ffn1': gain(ks[9], (DEPTH, D_MODEL)),
        'norm_mix': gain(ks[10], (DEPTH, D_MODEL)),
        'norm_ffn2': gain(ks[11], (DEPTH, D_MODEL)),
        'ffn1_w1': nrm(ks[12], (DEPTH, D_MODEL, D_FF), D_MODEL ** -0.5),
        'ffn1_w3': nrm(ks[13], (DEPTH, D_MODEL, D_FF), D_MODEL ** -0.5),
        'ffn1_w2': nrm(ks[14], (DEPTH, D_FF, D_MODEL), D_FF ** -0.5),
        'ffn2_w1': nrm(ks[15], (DEPTH, D_MODEL, D_FF), D_MODEL ** -0.5),
        'ffn2_w3': nrm(ks[16], (DEPTH, D_MODEL, D_FF), D_MODEL ** -0.5),
        'ffn2_w2': nrm(ks[17], (DEPTH, D_FF, D_MODEL), D_FF ** -0.5),
        'w_in': nrm(ks[18], (DEPTH, D_MODEL, PROJ_WIDTH), D_MODEL ** -0.5),
        'w_out': nrm(ks[19], (DEPTH, MIX_WIDTH, D_MODEL), MIX_WIDTH ** -0.5),
        'q_norm': gain(ks[20], (DEPTH, HEAD_DIM)),
        'k_norm': gain(ks[21], (DEPTH, HEAD_DIM)),
        'conv_b_w': nrm(ks[22], (DEPTH, SHORT_CONV, GROUP_WIDTH), SHORT_CONV ** -0.5),
        'conv_c_w': nrm(ks[23], (DEPTH, SHORT_CONV, 3 * GROUP_WIDTH), SHORT_CONV ** -0.5),
        'delta_a_log': jnp.log(jax.random.uniform(ks[24], (DEPTH, 2, N_HEADS_C), f32, 1.0, 16.0)),
        'delta_dt_bias': dt + jnp.log(-jnp.expm1(-dt)),
        'delta_norm': gain(ks[26], (DEPTH, HEAD_DIM)),
        'final_norm': gain(ks[27], (D_MODEL,)),
    }


def reference(x_prompt, x_sample, c, cache_k, cache_v, state_delta, c_ctx,
              mod_w, mod_b, norm_ffn1, norm_mix, norm_ffn2,
              ffn1_w1, ffn1_w3, ffn1_w2, ffn2_w1, ffn2_w3, ffn2_w2,
              w_in, w_out, q_norm, k_norm, conv_b_w, conv_c_w,
              delta_a_log, delta_dt_bias, delta_norm, final_norm):
    yp = x_prompt
    ys = x_sample
    cond_ctx = c_ctx[None, :]
    k_list, v_list, s_list = [], [], []
    for l in range(DEPTH):
        lp = {
            'mod_w': mod_w[l], 'mod_b': mod_b[l],
            'norm_ffn1': norm_ffn1[l], 'norm_mix': norm_mix[l], 'norm_ffn2': norm_ffn2[l],
            'ffn1_w1': ffn1_w1[l], 'ffn1_w3': ffn1_w3[l], 'ffn1_w2': ffn1_w2[l],
            'ffn2_w1': ffn2_w1[l], 'ffn2_w3': ffn2_w3[l], 'ffn2_w2': ffn2_w2[l],
            'w_in': w_in[l], 'w_out': w_out[l], 'q_norm': q_norm[l], 'k_norm': k_norm[l],
            'conv_b_w': conv_b_w[l], 'conv_c_w': conv_c_w[l],
            'delta_a_log': delta_a_log[l], 'delta_dt_bias': delta_dt_bias[l],
            'delta_norm': delta_norm[l],
        }
        yp, k_l, v_l, s_l = trunk_layer(yp, cond_ctx, lp, None)
        k_list.append(k_l)
        v_list.append(v_l)
        s_list.append(s_l)
        ys, _, _, _ = trunk_layer(ys, c, lp, (cache_k[:, l], cache_v[:, l], state_delta[:, l]))
    y_prompt = rms_norm(yp, final_norm)
    y_sample = rms_norm(ys, final_norm)
    new_cache_k = jnp.stack(k_list, axis=1)
    new_cache_v = jnp.stack(v_list, axis=1)
    new_state_delta = jnp.stack(s_list, axis=1)
    return (y_prompt, y_sample, new_cache_k, new_cache_v, new_state_delta)
```

```python
import functools

import jax
import jax.numpy as jnp
import numpy as np
from jax import lax
from jax.experimental import pallas as pl
from jax.experimental.pallas import tpu as pltpu

F32 = jnp.float32
BF16 = jnp.bfloat16

HEAD_DIM = 64
N_HEADS_A = 4
KV_HEADS_A = 2
N_HEADS_C = 4
GROUP_WIDTH = 256
GRID_W = 64
DELTA_CHUNK = 64
ROPE_THETA = 10000.0
N_MOD = 9
EPS = 1e-6
PROJ_MAIN = 2304
PROJ_SMALL = 16
LANES = 128
SUBLANES = 8
VMEM_LIMIT_BYTES = 56 * 1024 * 1024
DELTA_HEADS_PER_STEP = 2


def _cparams(sem, big=False):
    return pltpu.CompilerParams(
        dimension_semantics=sem,
        vmem_limit_bytes=VMEM_LIMIT_BYTES if big else None)


def _dot(a, b):
    return jnp.dot(a, b, preferred_element_type=F32)


def _dot_nt(a, b):
    return lax.dot_general(a, b, (((1,), (1,)), ((), ())), preferred_element_type=F32)


def _split2(x):
    hi = x.astype(BF16)
    lo = (x - hi.astype(F32)).astype(BF16)
    return hi, lo


def _split3(x):
    hi = x.astype(BF16)
    r = x - hi.astype(F32)
    mid = r.astype(BF16)
    lo = (r - mid.astype(F32)).astype(BF16)
    return hi, mid, lo


def _dot_exact_left(sel, x):
    hi, mid, lo = _split3(x)
    return _dot(sel, hi) + _dot(sel, mid) + _dot(sel, lo)


def _dot3(a, b):
    ah, al = _split2(a)
    bh, bl = _split2(b)
    return _dot(ah, bh) + (_dot(ah, bl) + _dot(al, bh))


def _rms(x, g):
    return x * lax.rsqrt(jnp.mean(x * x, axis=-1, keepdims=True) + EPS) * g


def _silu(x):
    return x * jax.nn.sigmoid(x)


def _group_sumsq(x, ones_bd):
    hi, lo = _split2(x * x)
    return _dot(hi, ones_bd) + _dot(lo, ones_bd)


def _mod_kernel(c_ref, w_ref, b_ref, o_ref):
    s = _silu(c_ref[...]).astype(BF16)
    o_ref[...] = _dot(s, w_ref[...].astype(BF16)) + b_ref[...]


def _modulation(cond, mod_w, mod_b):
    rows, d = cond.shape
    n = mod_w.shape[1]
    tn = d
    return pl.pallas_call(
        _mod_kernel,
        out_shape=jax.ShapeDtypeStruct((rows, n), F32),
        grid=(n // tn,),
        in_specs=[pl.BlockSpec((rows, d), lambda j: (0, 0)),
                  pl.BlockSpec((d, tn), lambda j: (0, j)),
                  pl.BlockSpec((1, tn), lambda j: (0, j))],
        out_specs=pl.BlockSpec((rows, tn), lambda j: (0, j)),
        compiler_params=_cparams(("arbitrary",)),
        name="modulation",
    )(cond, mod_w, mod_b.reshape(1, n))


def _ffn_chunks(f):
    step = 1024
    return [(s, min(s + step, f)) for s in range(0, f, step)]


def _ffn_kernel(*refs, which, n_mix, final):
    x_ref, mod_ref, g_ref, w1_ref, w3_ref, w2_ref = refs[:6]
    pos = 6
    mix_refs = refs[pos:pos + n_mix]
    pos += n_mix
    wout_ref = None
    if n_mix:
        wout_ref = refs[pos]
        pos += 1
    gf_ref = None
    if final:
        gf_ref = refs[pos]
        pos += 1
    o_ref = refs[pos]

    x = x_ref[0]
    mod = mod_ref[0]
    if n_mix:
        acc = None
        for i, m_ref in enumerate(mix_refs):
            w = m_ref.shape[-1]
            part = _dot(m_ref[0], wout_ref[i * w:(i + 1) * w, :])
            acc = part if acc is None else acc + part
        x = x + mod[5:6] * acc
    sh = mod[3 * which:3 * which + 1]
    sc = mod[3 * which + 1:3 * which + 2]
    gt = mod[3 * which + 2:3 * which + 3]
    h = (_rms(x, g_ref[...]) * (1.0 + sc) + sh).astype(BF16)
    out = None
    for s, e in _ffn_chunks(w1_ref.shape[1]):
        a = _dot(h, w1_ref[:, s:e])
        b = _dot(h, w3_ref[:, s:e])
        act = (_silu(a) * b).astype(BF16)
        part = _dot(act, w2_ref[s:e, :])
        out = part if out is None else out + part
    xn = x + 0.5 * gt * out
    if final:
        xn = _rms(xn, gf_ref[...])
    o_ref[0] = xn


def _ffn(x, mod, g, w1, w3, w2, *, which, mix=None, w_out=None, final_g=None):
    bsz, seq, d = x.shape
    f = w1.shape[1]
    tm = min(512, seq)
    per_batch_mod = mod.shape[0] > 1
    mod_map = (lambda b, i: (b, 0, 0)) if per_batch_mod else (lambda b, i: (0, 0, 0))
    const2 = lambda b, i: (0, 0)
    single = pl.Buffered(1)
    args = [x, mod, g.reshape(1, d), w1, w3, w2]
    specs = [pl.BlockSpec((1, tm, d), lambda b, i: (b, i, 0)),
             pl.BlockSpec((1, N_MOD, d), mod_map),
             pl.BlockSpec((1, d), const2),
             pl.BlockSpec((d, f), const2, pipeline_mode=single),
             pl.BlockSpec((d, f), const2, pipeline_mode=single),
             pl.BlockSpec((f, d), const2, pipeline_mode=single)]
    n_mix = 0
    if mix is not None:
        n_mix = len(mix)
        for m in mix:
            args.append(m)
            specs.append(pl.BlockSpec((1, tm, m.shape[-1]), lambda b, i: (b, i, 0)))
        args.append(w_out)
        specs.append(pl.BlockSpec(w_out.shape, const2, pipeline_mode=single))
    if final_g is not None:
        args.append(final_g.reshape(1, d))
        specs.append(pl.BlockSpec((1, d), const2))
    return pl.pallas_call(
        functools.partial(_ffn_kernel, which=which, n_mix=n_mix, final=final_g is not None),
        out_shape=jax.ShapeDtypeStruct((bsz, seq, d), F32),
        grid=(bsz, seq // tm),
        in_specs=specs,
        out_specs=pl.BlockSpec((1, tm, d), lambda b, i: (b, i, 0)),
        compiler_params=_cparams(("arbitrary", "arbitrary"), big=True),
        name="ffn",
    )(*args)


_PROJ_SPLITS = ((0, 512), (512, 1280), (1280, 2304), (2304, 2560), (2560, 2688))


def _inproj_kernel(x_ref, mod_ref, g_ref, w_ref, *o_refs):
    mod = mod_ref[0]
    h = (_rms(x_ref[0], g_ref[...]) * (1.0 + mod[4:5]) + mod[3:4]).astype(BF16)
    for (s, e), o_ref in zip(_PROJ_SPLITS, o_refs):
        o_ref[0] = _dot(h, w_ref[:, s:e])


def _inproj(x, mod, g, w_in_r):
    bsz, seq, d = x.shape
    tm = min(512, seq)
    per_batch_mod = mod.shape[0] > 1
    mod_map = (lambda b, i: (b, 0, 0)) if per_batch_mod else (lambda b, i: (0, 0, 0))
    const2 = lambda b, i: (0, 0)
    widths = [e - s for s, e in _PROJ_SPLITS]
    return pl.pallas_call(
        _inproj_kernel,
        out_shape=[jax.ShapeDtypeStruct((bsz, seq, w), F32) for w in widths],
        grid=(bsz, seq // tm),
        in_specs=[pl.BlockSpec((1, tm, d), lambda b, i: (b, i, 0)),
                  pl.BlockSpec((1, N_MOD, d), mod_map),
                  pl.BlockSpec((1, d), const2),
                  pl.BlockSpec(w_in_r.shape, const2, pipeline_mode=pl.Buffered(1))],
        out_specs=[pl.BlockSpec((1, tm, w), lambda b, i: (b, i, 0)) for w in widths],
        compiler_params=_cparams(("arbitrary", "arbitrary"), big=True),
        name="inproj",
    )(x, mod, g.reshape(1, d), w_in_r)


def _rope(x, cos, sin_signed):
    lane = lax.broadcasted_iota(jnp.int32, x.shape, 1)
    up = pltpu.roll(x, LANES - 16, axis=1)
    down = pltpu.roll(x, 16, axis=1)
    partner = jnp.where((lane % 32) < 16, up, down)
    return x * cos + partner * sin_signed


def _attn_prep_kernel(*refs, rope, want_cache):
    pa_ref, ones_ref, gq_ref, gk_ref = refs[:4]
    pos = 4
    if rope:
        cos_ref, sin_ref = refs[pos:pos + 2]
        pos += 2
    q_ref, ka_ref, kb_ref, va_ref, vb_ref = refs[pos:pos + 5]
    pos += 5
    pa = pa_ref[0]
    ones_bd = ones_ref[...]

    def normed(x, g):
        ss = _group_sumsq(x, ones_bd)
        return x * lax.rsqrt(ss * (1.0 / HEAD_DIM) + EPS) * g

    q0 = normed(pa[:, 0:128], gq_ref[...])
    q1 = normed(pa[:, 128:256], gq_ref[...])
    k = normed(pa[:, 256:384], gk_ref[...])
    v = pa[:, 384:512]
    if want_cache:
        refs[pos][0] = k
    if rope:
        cos = cos_ref[...]
        sin = sin_ref[...]
        q0 = _rope(q0, cos, sin)
        q1 = _rope(q1, cos, sin)
        k = _rope(k, cos, sin)
    scale = HEAD_DIM ** -0.5
    q_ref[0, :, 0:128] = (q0 * scale).astype(BF16)
    q_ref[0, :, 128:256] = (q1 * scale).astype(BF16)
    ka_ref[0] = k.astype(BF16)
    kb_ref[0] = pltpu.roll(k, HEAD_DIM, axis=1).astype(BF16)
    va_ref[0] = v.astype(BF16)
    vb_ref[0] = pltpu.roll(v, HEAD_DIM, axis=1).astype(BF16)


def _attn_prep(pa, ones_bd, gq, gk, rope_tabs, want_cache):
    bsz, seq, _ = pa.shape
    tl = min(512, seq)
    rope = rope_tabs is not None
    const2 = lambda b, i: (0, 0)
    args = [pa, ones_bd, gq, gk]
    specs = [pl.BlockSpec((1, tl, 512), lambda b, i: (b, i, 0)),
             pl.BlockSpec((LANES, LANES), const2),
             pl.BlockSpec((1, LANES), const2),
             pl.BlockSpec((1, LANES), const2)]
    if rope:
        args += list(rope_tabs)
        specs += [pl.BlockSpec((tl, LANES), lambda b, i: (i, 0))] * 2
    outs = [jax.ShapeDtypeStruct((bsz, seq, 256), BF16)]
    outs += [jax.ShapeDtypeStruct((bsz, seq, LANES), BF16)] * 4
    ospecs = [pl.BlockSpec((1, tl, 256), lambda b, i: (b, i, 0))]
    ospecs += [pl.BlockSpec((1, tl, LANES), lambda b, i: (b, i, 0))] * 4
    if want_cache:
        outs.append(jax.ShapeDtypeStruct((bsz, seq, LANES), F32))
        ospecs.append(pl.BlockSpec((1, tl, LANES), lambda b, i: (b, i, 0)))
    return pl.pallas_call(
        functools.partial(_attn_prep_kernel, rope=rope, want_cache=want_cache),
        out_shape=outs,
        grid=(bsz, seq // tl),
        in_specs=specs,
        out_specs=ospecs,
        compiler_params=_cparams(("arbitrary", "arbitrary")),
        name="attn_prep",
    )(*args)


def _attn_kernel(q_ref, ka_ref, kb_ref, va_ref, vb_ref, o_ref):
    tq = q_ref.shape[1]
    lane = lax.broadcasted_iota(jnp.int32, (tq, LANES), 1)
    low = lane < HEAD_DIM
    arrangement = ((ka_ref, va_ref), (kb_ref, vb_ref), (kb_ref, vb_ref), (ka_ref, va_ref))
    outs = []
    for h in range(N_HEADS_A):
        slab = q_ref[0, :, (h // 2) * LANES:(h // 2 + 1) * LANES]
        keep = low if h % 2 == 0 else jnp.logical_not(low)
        qh = jnp.where(keep, slab, jnp.zeros_like(slab))
        k_ref, v_ref = arrangement[h]
        s = _dot_nt(qh, k_ref[0])
        m = jnp.max(s, axis=-1, keepdims=True)
        p = jnp.exp(s - m)
        denom = jnp.sum(p, axis=-1, keepdims=True)
        o = _dot(p.astype(BF16), v_ref[0]) / denom
        outs.append(o)
    o_ref[0, :, 0:128] = jnp.where(low, outs[0], outs[1]).astype(BF16)
    o_ref[0, :, 128:256] = jnp.where(low, outs[2], outs[3]).astype(BF16)


def _attention(q, ka, kb, va, vb):
    bsz, seq, _ = q.shape
    lk = ka.shape[1]
    tq = min(256, seq)
    kv_spec = pl.BlockSpec((1, lk, LANES), lambda b, i: (b, 0, 0))
    return pl.pallas_call(
        _attn_kernel,
        out_shape=jax.ShapeDtypeStruct((bsz, seq, 256), BF16),
        grid=(bsz, seq // tq),
        in_specs=[pl.BlockSpec((1, tq, 256), lambda b, i: (b, i, 0)),
                  kv_spec, kv_spec, kv_spec, kv_spec],
        out_specs=pl.BlockSpec((1, tq, 256), lambda b, i: (b, i, 0)),
        compiler_params=_cparams(("arbitrary", "arbitrary"), big=True),
        name="attention",
    )(q, ka, kb, va, vb)


def _conv3(x, prev_row, next_row, w):
    rows = x.shape[0]
    ridx = lax.broadcasted_iota(jnp.int32, x.shape, 0)
    xm = jnp.where(ridx == 0, prev_row, pltpu.roll(x, 1, axis=0))
    xp = jnp.where(ridx == rows - 1, next_row, pltpu.roll(x, rows - 1, axis=0))
    return xm * w[0:1] + x * w[1:2] + xp * w[2:3]


def _conv_prep_kernel(pb_ref, pbp_ref, pbn_ref, pc_ref, pcp_ref, pcn_ref, ps_ref,
                      wb_ref, wc_ref, ones_ref, alog_ref, dtb_ref, tril_ref, triu_ref,
                      ob_ref, qh_ref, kh_ref, vh_ref, zh_ref, gates_ref):
    i = pl.program_id(1)
    has_prev = i > 0
    has_next = i < pl.num_programs(1) - 1
    gw = GROUP_WIDTH

    pb = pb_ref[0]
    u = pb[:, gw:2 * gw] * pb[:, 2 * gw:3 * gw]
    pbp = pbp_ref[0, SUBLANES - 1:SUBLANES, :]
    pbn = pbn_ref[0, 0:1, :]
    u_prev = jnp.where(has_prev, pbp[:, gw:2 * gw] * pbp[:, 2 * gw:3 * gw], 0.0)
    u_next = jnp.where(has_next, pbn[:, gw:2 * gw] * pbn[:, 2 * gw:3 * gw], 0.0)
    ob_ref[0] = (pb[:, 0:gw] * _conv3(u, u_prev, u_next, wb_ref[...])).astype(BF16)

    pc = pc_ref[0]
    x3 = pc[:, 0:3 * gw]
    x_prev = jnp.where(has_prev, pcp_ref[0, SUBLANES - 1:SUBLANES, 0:3 * gw], 0.0)
    x_next = jnp.where(has_next, pcn_ref[0, 0:1, 0:3 * gw], 0.0)
    qkv = _silu(_conv3(x3, x_prev, x_next, wc_ref[...]))
    ones_bd = ones_ref[...]

    def l2n(x):
        return x * lax.rsqrt(_group_sumsq(x, ones_bd) + EPS)

    q_scale = HEAD_DIM ** -0.5
    for s in range(2):
        qs = l2n(qkv[:, s * LANES:(s + 1) * LANES]) * q_scale
        ks = l2n(qkv[:, gw + s * LANES:gw + (s + 1) * LANES])
        for a in range(2):
            h = 2 * s + a
            qh_ref[0, h] = qs[:, a * HEAD_DIM:(a + 1) * HEAD_DIM]
            kh_ref[0, h] = ks[:, a * HEAD_DIM:(a + 1) * HEAD_DIM]
    for h in range(N_HEADS_C):
        vh_ref[0, h] = qkv[:, 2 * gw + h * HEAD_DIM:2 * gw + (h + 1) * HEAD_DIM]
        zh_ref[0, h] = pc[:, 3 * gw + h * HEAD_DIM:3 * gw + (h + 1) * HEAD_DIM]

    raw = ps_ref[0]
    beta = jax.nn.sigmoid(raw)
    y = raw + dtb_ref[...]
    softplus = jnp.maximum(y, 0.0) + jnp.log(1.0 + jnp.exp(-jnp.abs(y)))
    g = -jnp.exp(alog_ref[...]) * softplus
    g_hi, g_mid, g_lo = _split3(g)
    tril = tril_ref[...]
    triu = triu_ref[...]
    gc_f = _dot(tril, g_hi) + _dot(tril, g_mid) + _dot(tril, g_lo)
    gc_b = _dot(triu, g_hi) + _dot(triu, g_mid) + _dot(triu, g_lo)
    lane = lax.broadcasted_iota(jnp.int32, raw.shape, 1)
    gates_ref[0] = jnp.where(lane < 8, beta, jnp.where(lane < 12, gc_f, gc_b))


def _conv_prep(pb, pc, ps, conv_b_w, conv_c_w, ones_bd, alog_row, dtb_row):
    bsz, seq, _ = pb.shape
    tl = min(512, seq)
    nblk8 = seq // SUBLANES
    r8 = tl // SUBLANES
    main = lambda b, i: (b, i, 0)
    prev = lambda b, i: (b, jnp.maximum(i * r8 - 1, 0), 0)
    nxt = lambda b, i: (b, jnp.minimum((i + 1) * r8, nblk8 - 1), 0)
    const2 = lambda b, i: (0, 0)
    idx = np.arange(tl)
    same_chunk = (idx[:, None] // DELTA_CHUNK) == (idx[None, :] // DELTA_CHUNK)
    tril = jnp.asarray(same_chunk & (idx[:, None] >= idx[None, :]), BF16)
    triu = jnp.asarray(same_chunk & (idx[:, None] <= idx[None, :]), BF16)
    hm = jax.ShapeDtypeStruct((bsz, N_HEADS_C, seq, HEAD_DIM), F32)
    hm_spec = pl.BlockSpec((1, N_HEADS_C, tl, HEAD_DIM), lambda b, i: (b, 0, i, 0))
    return pl.pallas_call(
        _conv_prep_kernel,
        out_shape=[jax.ShapeDtypeStruct((bsz, seq, GROUP_WIDTH), BF16), hm, hm, hm, hm,
                   jax.ShapeDtypeStruct((bsz, seq, LANES), F32)],
        grid=(bsz, seq // tl),
        in_specs=[pl.BlockSpec((1, tl, 768), main),
                  pl.BlockSpec((1, SUBLANES, 768), prev),
                  pl.BlockSpec((1, SUBLANES, 768), nxt),
                  pl.BlockSpec((1, tl, 1024), main),
                  pl.BlockSpec((1, SUBLANES, 1024), prev),
                  pl.BlockSpec((1, SUBLANES, 1024), nxt),
                  pl.BlockSpec((1, tl, LANES), main),
                  pl.BlockSpec((3, GROUP_WIDTH), const2),
                  pl.BlockSpec((3, 3 * GROUP_WIDTH), const2),
                  pl.BlockSpec((LANES, LANES), const2),
                  pl.BlockSpec((1, LANES), const2),
                  pl.BlockSpec((1, LANES), const2),
                  pl.BlockSpec((tl, tl), const2),
                  pl.BlockSpec((tl, tl), const2)],
        out_specs=[pl.BlockSpec((1, tl, GROUP_WIDTH), main), hm_spec, hm_spec, hm_spec,
                   hm_spec, pl.BlockSpec((1, tl, LANES), main)],
        compiler_params=_cparams(("arbitrary", "arbitrary"), big=True),
        name="conv_prep",
    )(pb, pb, pb, pc, pc, pc, ps, conv_b_w, conv_c_w, ones_bd, alog_row, dtb_row, tril, triu)


_TRI_BASE_LOG2 = 3


def _unit_tri_inverse(lmat):
    n = lmat.shape[0]
    r = lax.broadcasted_iota(jnp.int32, (n, n), 0)
    c = lax.broadcasted_iota(jnp.int32, (n, n), 1)

    def same_block(log2_size):
        return lax.shift_right_logical(r, log2_size) == lax.shift_right_logical(c, log2_size)

    a = jnp.where(same_block(_TRI_BASE_LOG2), lmat, 0.0)
    p = jnp.where(r == c, 1.0, 0.0) - a
    for _ in range(_TRI_BASE_LOG2 - 1):
        a = _dot3(a, a)
        p = p + _dot3(p, a)
    for log2_size in range(_TRI_BASE_LOG2, int(np.log2(n))):
        couple = jnp.logical_and(same_block(log2_size + 1),
                                 jnp.logical_not(same_block(log2_size)))
        p = p - _dot3(_dot3(p, jnp.where(couple, lmat, 0.0)), p)
    return p


def _delta_chunk(q, k, v, beta, gc, s_prev, reverse, eye_bf, ones_bf):
    n = q.shape[0]
    r = lax.broadcasted_iota(jnp.int32, (n, n), 0)
    c = lax.broadcasted_iota(jnp.int32, (n, n), 1)
    incl = (r <= c) if reverse else (r >= c)
    strict = (r < c) if reverse else (r > c)
    gc_full = jnp.broadcast_to(gc, (n, n))
    gc_row = _dot_exact_left(ones_bf, jnp.where(r == c, gc_full, 0.0))
    decay = jnp.exp(jnp.where(incl, gc_full - gc_row, -jnp.inf))
    e = jnp.exp(gc)
    gc_last = gc[0:1] if reverse else gc[n - 1:n]
    kb = k * beta
    vb = v * beta
    k_bf = k.astype(BF16)
    lmat = jnp.where(strict, _dot_nt(kb.astype(BF16), k_bf) * decay, 0.0)
    t_inv = _unit_tri_inverse(lmat).astype(BF16)
    u = _dot(t_inv, vb.astype(BF16))
    w = _dot(t_inv, (kb * e).astype(BF16))
    aqk = (_dot_nt(q.astype(BF16), k_bf) * decay).astype(BF16)
    qg = (q * e).astype(BF16)
    kdec = (k * jnp.exp(gc_last - gc)).astype(BF16)
    kdec_t = _dot_nt(eye_bf, kdec).astype(BF16)
    s_bf = s_prev.astype(BF16)
    v_new = u - _dot(w.astype(BF16), s_bf)
    v_new_bf = v_new.astype(BF16)
    o = _dot(qg, s_bf) + _dot(aqk, v_new_bf)
    s_new = s_prev * jnp.exp(gc_last) + _dot(kdec_t, v_new_bf)
    return o, s_new


def _delta_kernel(qh_ref, kh_ref, vh_ref, zh_ref, gates_ref, s0_ref, ng_ref,
                  o_ref, send_ref, of_scr, ob_scr, s_scr):
    hb = pl.program_id(1)
    heads = qh_ref.shape[1]
    seq = qh_ref.shape[2]
    cs = DELTA_CHUNK
    n_chunks = seq // cs
    r = lax.broadcasted_iota(jnp.int32, (cs, cs), 0)
    c = lax.broadcasted_iota(jnp.int32, (cs, cs), 1)
    eye_bf = jnp.where(r == c, 1.0, 0.0).astype(BF16)
    ones_bf = jnp.ones((cs, cs), BF16)
    lane = lax.broadcasted_iota(jnp.int32, (cs, LANES), 1)

    s_scr[...] = s0_ref[0].astype(F32)

    def column(tile, idx):
        return jnp.sum(jnp.where(lane == idx, tile, 0.0), axis=-1, keepdims=True)

    def body(n, carry):
        for d in range(2):
            chunk = n if d == 0 else n_chunks - 1 - n
            row0 = pl.multiple_of(chunk * cs, cs)
            gt = gates_ref[0, pl.ds(row0, cs), :]
            for a in range(heads):
                head = hb * heads + a
                beta = column(gt, 4 * d + head)
                gc = column(gt, 8 + 4 * d + head)
                q = qh_ref[0, a, pl.ds(row0, cs), :]
                k = kh_ref[0, a, pl.ds(row0, cs), :]
                v = vh_ref[0, a, pl.ds(row0, cs), :]
                o, s_new = _delta_chunk(q, k, v, beta, gc, s_scr[d, a], d == 1, eye_bf, ones_bf)
                s_scr[d, a] = s_new
                if d == 0:
                    of_scr[a, pl.ds(row0, cs), :] = o
                else:
                    ob_scr[a, pl.ds(row0, cs), :] = o
        return carry

    lax.fori_loop(0, n_chunks, body, 0)
    send_ref[0] = s_scr[...]

    ng = ng_ref[...]

    def epilogue(n, carry):
        row0 = pl.multiple_of(n * cs, cs)
        parts = []
        for a in range(heads):
            o = of_scr[a, pl.ds(row0, cs), :] + ob_scr[a, pl.ds(row0, cs), :]
            z = zh_ref[0, a, pl.ds(row0, cs), :]
            parts.append(_rms(o, ng) * _silu(z))
        o_ref[0, pl.ds(row0, cs), :] = jnp.concatenate(parts, axis=-1).astype(BF16)
        return carry

    lax.fori_loop(0, n_chunks, epilogue, 0)


def _deltanet(qh, kh, vh, zh, gates, s0, norm_g):
    bsz, nh, seq, hd = qh.shape
    hps = DELTA_HEADS_PER_STEP
    hm_spec = pl.BlockSpec((1, hps, seq, hd), lambda b, j: (b, j, 0, 0))
    st_spec = pl.BlockSpec((1, 2, hps, hd, hd), lambda b, j: (b, 0, j, 0, 0))
    return pl.pallas_call(
        _delta_kernel,
        out_shape=[jax.ShapeDtypeStruct((bsz, seq, nh * hd), BF16),
                   jax.ShapeDtypeStruct((bsz, 2, nh, hd, hd), F32)],
        grid=(bsz, nh // hps),
        in_specs=[hm_spec, hm_spec, hm_spec, hm_spec,
                  pl.BlockSpec((1, seq, LANES), lambda b, j: (b, 0, 0)),
                  st_spec,
                  pl.BlockSpec((1, hd), lambda b, j: (0, 0))],
        out_specs=[pl.BlockSpec((1, seq, hps * hd), lambda b, j: (b, 0, j)), st_spec],
        scratch_shapes=[pltpu.VMEM((hps, seq, hd), F32),
                        pltpu.VMEM((hps, seq, hd), F32),
                        pltpu.VMEM((2, hps, hd, hd), F32)],
        compiler_params=_cparams(("arbitrary", "arbitrary"), big=True),
        name="deltanet",
    )(qh, kh, vh, zh, gates, s0, norm_g.reshape(1, hd))


def _fourier_ch_kernel(x_ref, c_ref, s_ref, w_ref):
    x = x_ref[0].astype(BF16)
    w_ref[0] = _dot(x, c_ref[...].astype(BF16)).astype(BF16)
    w_ref[1] = _dot(x, s_ref[...].astype(BF16)).astype(BF16)


def _fourier_channels(xd, c_bd, s_bd):
    bsz, seq, gw = xd.shape
    tl = min(512, seq)
    const2 = lambda b, i: (0, 0)
    return pl.pallas_call(
        _fourier_ch_kernel,
        out_shape=jax.ShapeDtypeStruct((2, seq, bsz * gw), BF16),
        grid=(bsz, seq // tl),
        in_specs=[pl.BlockSpec((1, tl, gw), lambda b, i: (b, i, 0)),
                  pl.BlockSpec((gw, gw), const2),
                  pl.BlockSpec((gw, gw), const2)],
        out_specs=pl.BlockSpec((2, tl, gw), lambda b, i: (0, i, b)),
        compiler_params=_cparams(("arbitrary", "arbitrary")),
        name="fourier_channels",
    )(xd, c_bd, s_bd)


def _fourier_pos_kernel(w_ref, a1_ref, b1_ref, a0_ref, b0_ref, o_ref, c_scr, s_scr, *, scale):
    i = pl.program_id(1)
    tm = c_scr.shape[0]
    sub = a0_ref.shape[0]
    a0 = a0_ref[...]
    b0 = b0_ref[...]
    for rblk in range(tm // sub):
        j1 = i * (tm // sub) + rblk
        a1 = a1_ref[pl.ds(j1, 1), :]
        b1 = b1_ref[pl.ds(j1, 1), :]
        c_scr[rblk * sub:(rblk + 1) * sub, :] = (a1 * a0 - b1 * b0).astype(BF16)
        s_scr[rblk * sub:(rblk + 1) * sub, :] = (b1 * a0 + a1 * b0).astype(BF16)
    res = (_dot(c_scr[...], w_ref[0]) - _dot(s_scr[...], w_ref[1])) * scale
    gw = o_ref.shape[2]
    for b in range(o_ref.shape[0]):
        o_ref[b] = res[:, b * gw:(b + 1) * gw].astype(BF16)


def _fourier_positions(w, tabs, bsz, gw):
    _, seq, ncol = w.shape
    a1, b1, a0, b0 = tabs
    tm = min(512, seq)
    nsplit = 2 if (bsz % 2 == 0 and seq > 1024) else 1
    bpart = bsz // nsplit
    const2 = lambda p, i: (0, 0)
    single = pl.Buffered(1)
    scale = 1.0 / np.sqrt(float(seq) * HEAD_DIM)
    return pl.pallas_call(
        functools.partial(_fourier_pos_kernel, scale=scale),
        out_shape=jax.ShapeDtypeStruct((bsz, seq, gw), BF16),
        grid=(nsplit, seq // tm),
        in_specs=[pl.BlockSpec((2, seq, bpart * gw), lambda p, i: (0, 0, p), pipeline_mode=single),
                  pl.BlockSpec(a1.shape, const2, pipeline_mode=single),
                  pl.BlockSpec(b1.shape, const2, pipeline_mode=single),
                  pl.BlockSpec(a0.shape, const2, pipeline_mode=single),
                  pl.BlockSpec(b0.shape, const2, pipeline_mode=single)],
        out_specs=pl.BlockSpec((bpart, tm, gw), lambda p, i: (p, i, 0)),
        scratch_shapes=[pltpu.VMEM((tm, seq), BF16), pltpu.VMEM((tm, seq), BF16)],
        compiler_params=_cparams(("arbitrary", "arbitrary"), big=True),
        name="fourier_positions",
    )(w, a1, b1, a0, b0)


def _rope_tables(seq):
    rows = seq // GRID_W
    r, cl = jnp.meshgrid(jnp.arange(rows), jnp.arange(GRID_W), indexing='ij')
    pos_r = r.reshape(-1).astype(F32)
    pos_c = cl.reshape(-1).astype(F32)
    quarter = HEAD_DIM // 4
    inv_freq = ROPE_THETA ** (-jnp.arange(quarter, dtype=F32) / quarter)
    cos_parts, sin_parts = [], []
    for pos in (pos_r, pos_c):
        ang = pos[:, None] * inv_freq[None, :]
        cos_parts += [jnp.cos(ang), jnp.cos(ang)]
        sin_parts += [-jnp.sin(ang), jnp.sin(ang)]
    cos = jnp.concatenate(cos_parts, axis=-1)
    sin = jnp.concatenate(sin_parts, axis=-1)
    return jnp.tile(cos, (1, 2)), jnp.tile(sin, (1, 2))


def _trig_table(num_rows, row_mult, seq):
    j = np.arange(num_rows, dtype=np.int64)[:, None] * row_mult
    k = np.arange(seq, dtype=np.int64)[None, :]
    ang = 2.0 * np.pi * ((j * k) % seq).astype(np.float64) / seq
    return jnp.asarray(np.cos(ang), F32), jnp.asarray(np.sin(ang), F32)


def _fourier_tables(seq):
    sub = 64
    a1, b1 = _trig_table(seq // sub, sub, seq)
    a0, b0 = _trig_table(sub, 1, seq)
    idx = np.arange(GROUP_WIDTH)
    same = (idx[:, None] // HEAD_DIM) == (idx[None, :] // HEAD_DIM)
    ang = 2.0 * np.pi * ((idx[:, None] % HEAD_DIM) * (idx[None, :] % HEAD_DIM) % HEAD_DIM) / HEAD_DIM
    c_bd = jnp.asarray(np.where(same, np.cos(ang), 0.0), F32)
    s_bd = jnp.asarray(np.where(same, np.sin(ang), 0.0), F32)
    return (a1, b1, a0, b0), c_bd, s_bd


def _to_lane_dense(t):
    b, hh, l, d = t.shape
    return jnp.swapaxes(t, 1, 2).reshape(b, l, hh * d)


def _to_head_major(t, heads):
    b, l, w = t.shape
    return jnp.swapaxes(t.reshape(b, l, heads, w // heads), 1, 2)


def _swap_halves(t):
    return jnp.concatenate([t[..., HEAD_DIM:], t[..., :HEAD_DIM]], axis=-1)


def _trunk_layer(x, mod, wts, consts, ctx, final_g):
    bsz, seq, _ = x.shape
    x1 = _ffn(x, mod, wts['norm_ffn1'], wts['ffn1_w1'], wts['ffn1_w3'], wts['ffn1_w2'], which=0)
    pa, pb, pc, pd, ps = _inproj(x1, mod, wts['norm_mix'], wts['w_in'])

    rope_tabs = consts['rope'] if ctx is not None else None
    prep = _attn_prep(pa, consts['ones_bd'], wts['gq'], wts['gk'], rope_tabs, ctx is None)
    q, ka, kb, va, vb = prep[:5]
    k_cache = v_cache = None
    if ctx is None:
        k_cache = _to_head_major(prep[5], KV_HEADS_A)
        v_cache = _to_head_major(pa[..., 384:512], KV_HEADS_A)
        s0 = jnp.zeros((bsz, 2, N_HEADS_C, HEAD_DIM, HEAD_DIM), F32)
    else:
        k_ctx, v_ctx, s0 = ctx
        kc = _to_lane_dense(k_ctx).astype(BF16)
        vc = _to_lane_dense(v_ctx).astype(BF16)
        ka = jnp.concatenate([kc, ka], axis=1)
        kb = jnp.concatenate([_swap_halves(kc), kb], axis=1)
        va = jnp.concatenate([vc, va], axis=1)
        vb = jnp.concatenate([_swap_halves(vc), vb], axis=1)
    o_a = _attention(q, ka, kb, va, vb)

    o_b, qh, kh, vh, zh, gates = _conv_prep(pb, pc, ps, wts['conv_b_w'], wts['conv_c_w'],
                                            consts['ones_bd'], wts['alog_row'], wts['dtb_row'])
    o_c, s_end = _deltanet(qh, kh, vh, zh, gates, s0, wts['delta_norm'])

    w = _fourier_channels(pd, consts['c_bd'], consts['s_bd'])
    o_d = _fourier_positions(w, consts['fourier'], bsz, GROUP_WIDTH)

    x3 = _ffn(x1, mod, wts['norm_ffn2'], wts['ffn2_w1'], wts['ffn2_w3'], wts['ffn2_w2'], which=2,
              mix=(o_a, o_b, o_c, o_d), w_out=wts['w_out'], final_g=final_g)
    return x3, k_cache, v_cache, s_end


def _lane_row(values, start):
    row = jnp.zeros((1, LANES), F32)
    return row.at[0, start:start + values.shape[0]].set(values.astype(F32))


def kernel(x_prompt, x_sample, c, cache_k, cache_v, state_delta, c_ctx, mod_w, mod_b, norm_ffn1, norm_mix, norm_ffn2, ffn1_w1, ffn1_w3, ffn1_w2, ffn2_w1, ffn2_w3, ffn2_w2, w_in, w_out, q_norm, k_norm, conv_b_w, conv_c_w, delta_a_log, delta_dt_bias, delta_norm, final_norm):
    depth = mod_w.shape[0]
    d_model = x_prompt.shape[-1]
    dec_b = x_sample.shape[0]
    assert w_in.shape[-1] == PROJ_MAIN + PROJ_SMALL + GROUP_WIDTH

    n_cond = 1 + dec_b
    rows = -(-n_cond // SUBLANES) * SUBLANES
    cond = jnp.zeros((rows, d_model), F32).at[0].set(c_ctx).at[1:n_cond].set(c)

    idx = np.arange(LANES)
    ones_bd = jnp.asarray((idx[:, None] // HEAD_DIM) == (idx[None, :] // HEAD_DIM), BF16)
    consts_p = {'ones_bd': ones_bd}
    consts_s = {'ones_bd': ones_bd, 'rope': _rope_tables(x_sample.shape[1])}
    consts_p['fourier'], consts_p['c_bd'], consts_p['s_bd'] = _fourier_tables(x_prompt.shape[1])
    consts_s['fourier'], consts_s['c_bd'], consts_s['s_bd'] = _fourier_tables(x_sample.shape[1])

    yp, ys = x_prompt, x_sample
    k_list, v_list, s_list = [], [], []
    for l in range(depth):
        w_in_l = w_in[l]
        small = w_in_l[:, PROJ_MAIN:PROJ_MAIN + PROJ_SMALL]
        w_in_r = jnp.concatenate(
            [w_in_l[:, :PROJ_MAIN], w_in_l[:, PROJ_MAIN + PROJ_SMALL:], small,
             jnp.zeros((d_model, LANES - PROJ_SMALL), F32)], axis=1).astype(BF16)
        wts = {
            'norm_ffn1': norm_ffn1[l], 'norm_mix': norm_mix[l], 'norm_ffn2': norm_ffn2[l],
            'ffn1_w1': ffn1_w1[l].astype(BF16), 'ffn1_w3': ffn1_w3[l].astype(BF16),
            'ffn1_w2': ffn1_w2[l].astype(BF16),
            'ffn2_w1': ffn2_w1[l].astype(BF16), 'ffn2_w3': ffn2_w3[l].astype(BF16),
            'ffn2_w2': ffn2_w2[l].astype(BF16),
            'w_in': w_in_r, 'w_out': w_out[l].astype(BF16),
            'gq': jnp.tile(q_norm[l], 2).reshape(1, LANES),
            'gk': jnp.tile(k_norm[l], 2).reshape(1, LANES),
            'conv_b_w': conv_b_w[l], 'conv_c_w': conv_c_w[l],
            'alog_row': _lane_row(delta_a_log[l].reshape(-1), 8),
            'dtb_row': _lane_row(delta_dt_bias[l].reshape(-1), 8),
            'delta_norm': delta_norm[l],
        }
        mod = _modulation(cond, mod_w[l], mod_b[l]).reshape(rows, N_MOD, d_model)
        fin = final_norm if l == depth - 1 else None
        yp, k_l, v_l, s_l = _trunk_layer(yp, mod[0:1], wts, consts_p, None, fin)
        k_list.append(k_l)
        v_list.append(v_l)
        s_list.append(s_l)
        ys, _, _, _ = _trunk_layer(ys, mod[1:n_cond], wts, consts_s,
                                   (cache_k[:, l], cache_v[:, l], state_delta[:, l]), fin)
    return (yp, ys, jnp.stack(k_list, axis=1), jnp.stack(v_list, axis=1),
            jnp.stack(s_list, axis=1))
```

```python
import functools

import jax
import jax.numpy as jnp
import numpy as np
from jax import lax
from jax.experimental import pallas as pl
from jax.experimental.pallas import tpu as pltpu

F32 = jnp.float32
BF16 = jnp.bfloat16

HEAD_DIM = 64
N_HEADS_A = 4
KV_HEADS_A = 2
N_HEADS_C = 4
GROUP_WIDTH = 256
GRID_W = 64
DELTA_CHUNK = 64
ROPE_THETA = 10000.0
N_MOD = 9
EPS = 1e-6
PROJ_MAIN = 2304
PROJ_SMALL = 16
LANES = 128
SUBLANES = 8
VMEM_LIMIT_BYTES = 56 * 1024 * 1024


def _cparams(sem, big=False):
    return pltpu.CompilerParams(
        dimension_semantics=sem,
        vmem_limit_bytes=VMEM_LIMIT_BYTES if big else None)


def _dot(a, b):
    return jnp.dot(a, b, preferred_element_type=F32)


def _dot_nt(a, b):
    return lax.dot_general(a, b, (((1,), (1,)), ((), ())), preferred_element_type=F32)


def _split2(x):
    hi = x.astype(BF16)
    lo = (x - hi.astype(F32)).astype(BF16)
    return hi, lo


def _split3(x):
    hi = x.astype(BF16)
    r = x - hi.astype(F32)
    mid = r.astype(BF16)
    lo = (r - mid.astype(F32)).astype(BF16)
    return hi, mid, lo


def _rms(x, g):
    return x * lax.rsqrt(jnp.mean(x * x, axis=-1, keepdims=True) + EPS) * g


def _silu(x):
    return x * jax.nn.sigmoid(x)


def _group_sumsq(x, ones_bd):
    hi, lo = _split2(x * x)
    return _dot(hi, ones_bd) + _dot(lo, ones_bd)


def _mod_kernel(c_ref, w_ref, b_ref, o_ref):
    s = _silu(c_ref[...]).astype(BF16)
    o_ref[...] = _dot(s, w_ref[...].astype(BF16)) + b_ref[...]


def _modulation(cond, mod_w, mod_b):
    rows, d = cond.shape
    n = mod_w.shape[1]
    tn = d
    return pl.pallas_call(
        _mod_kernel,
        out_shape=jax.ShapeDtypeStruct((rows, n), F32),
        grid=(n // tn,),
        in_specs=[pl.BlockSpec((rows, d), lambda j: (0, 0)),
                  pl.BlockSpec((d, tn), lambda j: (0, j)),
                  pl.BlockSpec((1, tn), lambda j: (0, j))],
        out_specs=pl.BlockSpec((rows, tn), lambda j: (0, j)),
        compiler_params=_cparams(("arbitrary",)),
        name="modulation",
    )(cond, mod_w, mod_b.reshape(1, n))


def _ffn_chunks(f):
    step = 1024
    return [(s, min(s + step, f)) for s in range(0, f, step)]


def _ffn_kernel(*refs, which, n_mix, final):
    x_ref, mod_ref, g_ref, w1_ref, w3_ref, w2_ref = refs[:6]
    pos = 6
    mix_refs = refs[pos:pos + n_mix]
    pos += n_mix
    wout_ref = None
    if n_mix:
        wout_ref = refs[pos]
        pos += 1
    gf_ref = None
    if final:
        gf_ref = refs[pos]
        pos += 1
    o_ref = refs[pos]

    x = x_ref[0]
    mod = mod_ref[0]
    if n_mix:
        acc = None
        for i, m_ref in enumerate(mix_refs):
            w = m_ref.shape[-1]
            part = _dot(m_ref[0], wout_ref[i * w:(i + 1) * w, :])
            acc = part if acc is None else acc + part
        x = x + mod[5:6] * acc
    sh = mod[3 * which:3 * which + 1]
    sc = mod[3 * which + 1:3 * which + 2]
    gt = mod[3 * which + 2:3 * which + 3]
    h = (_rms(x, g_ref[...]) * (1.0 + sc) + sh).astype(BF16)
    out = None
    for s, e in _ffn_chunks(w1_ref.shape[1]):
        a = _dot(h, w1_ref[:, s:e])
        b = _dot(h, w3_ref[:, s:e])
        act = (_silu(a) * b).astype(BF16)
        part = _dot(act, w2_ref[s:e, :])
        out = part if out is None else out + part
    xn = x + 0.5 * gt * out
    if final:
        xn = _rms(xn, gf_ref[...])
    o_ref[0] = xn


def _ffn(x, mod, g, w1, w3, w2, *, which, mix=None, w_out=None, final_g=None):
    bsz, seq, d = x.shape
    f = w1.shape[1]
    tm = min(512, seq)
    per_batch_mod = mod.shape[0] > 1
    mod_map = (lambda b, i: (b, 0, 0)) if per_batch_mod else (lambda b, i: (0, 0, 0))
    const2 = lambda b, i: (0, 0)
    single = pl.Buffered(1)
    args = [x, mod, g.reshape(1, d), w1, w3, w2]
    specs = [pl.BlockSpec((1, tm, d), lambda b, i: (b, i, 0)),
             pl.BlockSpec((1, N_MOD, d), mod_map),
             pl.BlockSpec((1, d), const2),
             pl.BlockSpec((d, f), const2, pipeline_mode=single),
             pl.BlockSpec((d, f), const2, pipeline_mode=single),
             pl.BlockSpec((f, d), const2, pipeline_mode=single)]
    n_mix = 0
    if mix is not None:
        n_mix = len(mix)
        for m in mix:
            args.append(m)
            specs.append(pl.BlockSpec((1, tm, m.shape[-1]), lambda b, i: (b, i, 0)))
        args.append(w_out)
        specs.append(pl.BlockSpec(w_out.shape, const2, pipeline_mode=single))
    if final_g is not None:
        args.append(final_g.reshape(1, d))
        specs.append(pl.BlockSpec((1, d), const2))
    return pl.pallas_call(
        functools.partial(_ffn_kernel, which=which, n_mix=n_mix, final=final_g is not None),
        out_shape=jax.ShapeDtypeStruct((bsz, seq, d), F32),
        grid=(bsz, seq // tm),
        in_specs=specs,
        out_specs=pl.BlockSpec((1, tm, d), lambda b, i: (b, i, 0)),
        compiler_params=_cparams(("arbitrary", "arbitrary"), big=True),
        name="ffn",
    )(*args)


_PROJ_SPLITS = ((0, 512), (512, 1280), (1280, 2304), (2304, 2560), (2560, 2688))


def _inproj_kernel(x_ref, mod_ref, g_ref, w_ref, *o_refs):
    mod = mod_ref[0]
    h = (_rms(x_ref[0], g_ref[...]) * (1.0 + mod[4:5]) + mod[3:4]).astype(BF16)
    for (s, e), o_ref in zip(_PROJ_SPLITS, o_refs):
        o_ref[0] = _dot(h, w_ref[:, s:e])


def _inproj(x, mod, g, w_in_r):
    bsz, seq, d = x.shape
    tm = min(512, seq)
    per_batch_mod = mod.shape[0] > 1
    mod_map = (lambda b, i: (b, 0, 0)) if per_batch_mod else (lambda b, i: (0, 0, 0))
    const2 = lambda b, i: (0, 0)
    widths = [e - s for s, e in _PROJ_SPLITS]
    return pl.pallas_call(
        _inproj_kernel,
        out_shape=[jax.ShapeDtypeStruct((bsz, seq, w), F32) for w in widths],
        grid=(bsz, seq // tm),
        in_specs=[pl.BlockSpec((1, tm, d), lambda b, i: (b, i, 0)),
                  pl.BlockSpec((1, N_MOD, d), mod_map),
                  pl.BlockSpec((1, d), const2),
                  pl.BlockSpec(w_in_r.shape, const2, pipeline_mode=pl.Buffered(1))],
        out_specs=[pl.BlockSpec((1, tm, w), lambda b, i: (b, i, 0)) for w in widths],
        compiler_params=_cparams(("arbitrary", "arbitrary"), big=True),
        name="inproj",
    )(x, mod, g.reshape(1, d), w_in_r)


def _rope(x, cos, sin_signed):
    lane = lax.broadcasted_iota(jnp.int32, x.shape, 1)
    up = pltpu.roll(x, LANES - 16, axis=1)
    down = pltpu.roll(x, 16, axis=1)
    partner = jnp.where((lane % 32) < 16, up, down)
    return x * cos + partner * sin_signed


def _attn_prep_kernel(*refs, rope, want_cache):
    pa_ref, ones_ref, gq_ref, gk_ref = refs[:4]
    pos = 4
    if rope:
        cos_ref, sin_ref = refs[pos:pos + 2]
        pos += 2
    q_ref, ka_ref, kb_ref, va_ref, vb_ref = refs[pos:pos + 5]
    pos += 5
    pa = pa_ref[0]
    ones_bd = ones_ref[...]

    def normed(x, g):
        ss = _group_sumsq(x, ones_bd)
        return x * lax.rsqrt(ss * (1.0 / HEAD_DIM) + EPS) * g

    q0 = normed(pa[:, 0:128], gq_ref[...])
    q1 = normed(pa[:, 128:256], gq_ref[...])
    k = normed(pa[:, 256:384], gk_ref[...])
    v = pa[:, 384:512]
    if want_cache:
        refs[pos][0] = k
    if rope:
        cos = cos_ref[...]
        sin = sin_ref[...]
        q0 = _rope(q0, cos, sin)
        q1 = _rope(q1, cos, sin)
        k = _rope(k, cos, sin)
    scale = HEAD_DIM ** -0.5
    q_ref[0, :, 0:128] = (q0 * scale).astype(BF16)
    q_ref[0, :, 128:256] = (q1 * scale).astype(BF16)
    ka_ref[0] = k.astype(BF16)
    kb_ref[0] = pltpu.roll(k, HEAD_DIM, axis=1).astype(BF16)
    va_ref[0] = v.astype(BF16)
    vb_ref[0] = pltpu.roll(v, HEAD_DIM, axis=1).astype(BF16)


def _attn_prep(pa, ones_bd, gq, gk, rope_tabs, want_cache):
    bsz, seq, _ = pa.shape
    tl = min(512, seq)
    rope = rope_tabs is not None
    const2 = lambda b, i: (0, 0)
    args = [pa, ones_bd, gq, gk]
    specs = [pl.BlockSpec((1, tl, 512), lambda b, i: (b, i, 0)),
             pl.BlockSpec((LANES, LANES), const2),
             pl.BlockSpec((1, LANES), const2),
             pl.BlockSpec((1, LANES), const2)]
    if rope:
        args += list(rope_tabs)
        specs += [pl.BlockSpec((tl, LANES), lambda b, i: (i, 0))] * 2
    outs = [jax.ShapeDtypeStruct((bsz, seq, 256), BF16)]
    outs += [jax.ShapeDtypeStruct((bsz, seq, LANES), BF16)] * 4
    ospecs = [pl.BlockSpec((1, tl, 256), lambda b, i: (b, i, 0))]
    ospecs += [pl.BlockSpec((1, tl, LANES), lambda b, i: (b, i, 0))] * 4
    if want_cache:
        outs.append(jax.ShapeDtypeStruct((bsz, seq, LANES), F32))
        ospecs.append(pl.BlockSpec((1, tl, LANES), lambda b, i: (b, i, 0)))
    return pl.pallas_call(
        functools.partial(_attn_prep_kernel, rope=rope, want_cache=want_cache),
        out_shape=outs,
        grid=(bsz, seq // tl),
        in_specs=specs,
        out_specs=ospecs,
        compiler_params=_cparams(("arbitrary", "arbitrary")),
        name="attn_prep",
    )(*args)


def _attn_kernel(q_ref, ka_ref, kb_ref, va_ref, vb_ref, o_ref):
    tq = q_ref.shape[1]
    lane = lax.broadcasted_iota(jnp.int32, (tq, LANES), 1)
    low = lane < HEAD_DIM
    arrangement = ((ka_ref, va_ref), (kb_ref, vb_ref), (kb_ref, vb_ref), (ka_ref, va_ref))
    outs = []
    for h in range(N_HEADS_A):
        slab = q_ref[0, :, (h // 2) * LANES:(h // 2 + 1) * LANES]
        keep = low if h % 2 == 0 else jnp.logical_not(low)
        qh = jnp.where(keep, slab, jnp.zeros_like(slab))
        k_ref, v_ref = arrangement[h]
        s = _dot_nt(qh, k_ref[0])
        m = jnp.max(s, axis=-1, keepdims=True)
        p = jnp.exp(s - m)
        denom = jnp.sum(p, axis=-1, keepdims=True)
        o = _dot(p.astype(BF16), v_ref[0]) / denom
        outs.append(o)
    o_ref[0, :, 0:128] = jnp.where(low, outs[0], outs[1]).astype(BF16)
    o_ref[0, :, 128:256] = jnp.where(low, outs[2], outs[3]).astype(BF16)


def _attention(q, ka, kb, va, vb):
    bsz, seq, _ = q.shape
    lk = ka.shape[1]
    tq = min(256, seq)
    kv_spec = pl.BlockSpec((1, lk, LANES), lambda b, i: (b, 0, 0))
    return pl.pallas_call(
        _attn_kernel,
        out_shape=jax.ShapeDtypeStruct((bsz, seq, 256), BF16),
        grid=(bsz, seq // tq),
        in_specs=[pl.BlockSpec((1, tq, 256), lambda b, i: (b, i, 0)),
                  kv_spec, kv_spec, kv_spec, kv_spec],
        out_specs=pl.BlockSpec((1, tq, 256), lambda b, i: (b, i, 0)),
        compiler_params=_cparams(("arbitrary", "arbitrary"), big=True),
        name="attention",
    )(q, ka, kb, va, vb)


def _conv3(x, prev_row, next_row, w):
    rows = x.shape[0]
    ridx = lax.broadcasted_iota(jnp.int32, x.shape, 0)
    xm = jnp.where(ridx == 0, prev_row, pltpu.roll(x, 1, axis=0))
    xp = jnp.where(ridx == rows - 1, next_row, pltpu.roll(x, rows - 1, axis=0))
    return xm * w[0:1] + x * w[1:2] + xp * w[2:3]


def _conv_prep_kernel(pb_ref, pbp_ref, pbn_ref, pc_ref, pcp_ref, pcn_ref, ps_ref,
                      wb_ref, wc_ref, ones_ref, alog_ref, dtb_ref, tril_ref, triu_ref,
                      ob_ref, q_ref, k_ref, v_ref, gates_ref, gatest_ref):
    i = pl.program_id(1)
    has_prev = i > 0
    has_next = i < pl.num_programs(1) - 1
    gw = GROUP_WIDTH

    pb = pb_ref[0]
    u = pb[:, gw:2 * gw] * pb[:, 2 * gw:3 * gw]
    pbp = pbp_ref[0, SUBLANES - 1:SUBLANES, :]
    pbn = pbn_ref[0, 0:1, :]
    u_prev = jnp.where(has_prev, pbp[:, gw:2 * gw] * pbp[:, 2 * gw:3 * gw], 0.0)
    u_next = jnp.where(has_next, pbn[:, gw:2 * gw] * pbn[:, 2 * gw:3 * gw], 0.0)
    ob_ref[0] = (pb[:, 0:gw] * _conv3(u, u_prev, u_next, wb_ref[...])).astype(BF16)

    pc = pc_ref[0]
    x3 = pc[:, 0:3 * gw]
    x_prev = jnp.where(has_prev, pcp_ref[0, SUBLANES - 1:SUBLANES, 0:3 * gw], 0.0)
    x_next = jnp.where(has_next, pcn_ref[0, 0:1, 0:3 * gw], 0.0)
    qkv = _silu(_conv3(x3, x_prev, x_next, wc_ref[...]))
    ones_bd = ones_ref[...]

    def l2n(x):
        return x * lax.rsqrt(_group_sumsq(x, ones_bd) + EPS)

    q_scale = HEAD_DIM ** -0.5
    for s in range(2):
        q_ref[0, :, s * LANES:(s + 1) * LANES] = l2n(qkv[:, s * LANES:(s + 1) * LANES]) * q_scale
        k_ref[0, :, s * LANES:(s + 1) * LANES] = l2n(qkv[:, gw + s * LANES:gw + (s + 1) * LANES])
    v_ref[0] = qkv[:, 2 * gw:3 * gw]

    raw = ps_ref[0]
    beta = jax.nn.sigmoid(raw)
    y = raw + dtb_ref[...]
    softplus = jnp.maximum(y, 0.0) + jnp.log(1.0 + jnp.exp(-jnp.abs(y)))
    g = -jnp.exp(alog_ref[...]) * softplus
    g_hi, g_mid, g_lo = _split3(g)
    tril = tril_ref[...]
    triu = triu_ref[...]
    gc_f = _dot(tril, g_hi) + _dot(tril, g_mid) + _dot(tril, g_lo)
    gc_b = _dot(triu, g_hi) + _dot(triu, g_mid) + _dot(triu, g_lo)
    lane = lax.broadcasted_iota(jnp.int32, raw.shape, 1)
    gates = jnp.where(lane < 8, beta, jnp.where(lane < 12, gc_f, gc_b))
    gates_ref[0] = gates
    gatest_ref[0] = jnp.transpose(gates)[0:2 * SUBLANES, :]


def _conv_prep(pb, pc, ps, conv_b_w, conv_c_w, ones_bd, alog_row, dtb_row):
    bsz, seq, _ = pb.shape
    tl = min(512, seq)
    nblk8 = seq // SUBLANES
    r8 = tl // SUBLANES
    main = lambda b, i: (b, i, 0)
    prev = lambda b, i: (b, jnp.maximum(i * r8 - 1, 0), 0)
    nxt = lambda b, i: (b, jnp.minimum((i + 1) * r8, nblk8 - 1), 0)
    const2 = lambda b, i: (0, 0)
    idx = np.arange(tl)
    same_chunk = (idx[:, None] // DELTA_CHUNK) == (idx[None, :] // DELTA_CHUNK)
    tril = jnp.asarray(same_chunk & (idx[:, None] >= idx[None, :]), BF16)
    triu = jnp.asarray(same_chunk & (idx[:, None] <= idx[None, :]), BF16)
    dense = jax.ShapeDtypeStruct((bsz, seq, GROUP_WIDTH), F32)
    dense_spec = pl.BlockSpec((1, tl, GROUP_WIDTH), main)
    return pl.pallas_call(
        _conv_prep_kernel,
        out_shape=[jax.ShapeDtypeStruct((bsz, seq, GROUP_WIDTH), BF16), dense, dense, dense,
                   jax.ShapeDtypeStruct((bsz, seq, LANES), F32),
                   jax.ShapeDtypeStruct((bsz, 2 * SUBLANES, seq), F32)],
        grid=(bsz, seq // tl),
        in_specs=[pl.BlockSpec((1, tl, 768), main),
                  pl.BlockSpec((1, SUBLANES, 768), prev),
                  pl.BlockSpec((1, SUBLANES, 768), nxt),
                  pl.BlockSpec((1, tl, 1024), main),
                  pl.BlockSpec((1, SUBLANES, 1024), prev),
                  pl.BlockSpec((1, SUBLANES, 1024), nxt),
                  pl.BlockSpec((1, tl, LANES), main),
                  pl.BlockSpec((3, GROUP_WIDTH), const2),
                  pl.BlockSpec((3, 3 * GROUP_WIDTH), const2),
                  pl.BlockSpec((LANES, LANES), const2),
                  pl.BlockSpec((1, LANES), const2),
                  pl.BlockSpec((1, LANES), const2),
                  pl.BlockSpec((tl, tl), const2),
                  pl.BlockSpec((tl, tl), const2)],
        out_specs=[pl.BlockSpec((1, tl, GROUP_WIDTH), main), dense_spec, dense_spec, dense_spec,
                   pl.BlockSpec((1, tl, LANES), main),
                   pl.BlockSpec((1, 2 * SUBLANES, tl), lambda b, i: (b, 0, i))],
        compiler_params=_cparams(("arbitrary", "arbitrary"), big=True),
        name="conv_prep",
    )(pb, pb, pb, pc, pc, pc, ps, conv_b_w, conv_c_w, ones_bd, alog_row, dtb_row, tril, triu)


_TRI_BASE_LOG2 = 3
DELTA_TILE = 2 * DELTA_CHUNK


def _dot3_many(a_list, b_list):
    a_split = [_split2(a) for a in a_list]
    b_split = [_split2(b) for b in b_list]
    return [_dot(ah, bh) + (_dot(ah, bl) + _dot(al, bh))
            for (ah, al), (bh, bl) in zip(a_split, b_split)]


def _unit_tri_inverse_many(lmats):
    n = lmats[0].shape[0]
    r = lax.broadcasted_iota(jnp.int32, (n, n), 0)
    c = lax.broadcasted_iota(jnp.int32, (n, n), 1)

    def same_block(log2_size):
        return lax.shift_right_logical(r, log2_size) == lax.shift_right_logical(c, log2_size)

    base = same_block(_TRI_BASE_LOG2)
    eye = jnp.where(r == c, 1.0, 0.0)
    a_list = [jnp.where(base, m, 0.0) for m in lmats]
    p_list = [eye - a for a in a_list]
    for _ in range(_TRI_BASE_LOG2 - 1):
        a_list = _dot3_many(a_list, a_list)
        p_list = [p + pa for p, pa in zip(p_list, _dot3_many(p_list, a_list))]
    for log2_size in range(_TRI_BASE_LOG2, int(np.log2(n))):
        couple = jnp.logical_and(same_block(log2_size + 1),
                                 jnp.logical_not(same_block(log2_size)))
        c_list = [jnp.where(couple, m, 0.0) for m in lmats]
        pc_list = _dot3_many(p_list, c_list)
        p_list = [p - pcp for p, pcp in zip(p_list, _dot3_many(pc_list, p_list))]
    return p_list


def _lane_concat(parts):
    return jnp.concatenate(parts, axis=-1)


def _delta_prep_kernel(q_ref, k_ref, v_ref, gates_ref, gt_ref, *out_refs):
    cs = DELTA_CHUNK
    hd = HEAD_DIM
    nc = q_ref.shape[1] // cs
    heads = N_HEADS_C
    r = lax.broadcasted_iota(jnp.int32, (cs, cs), 0)
    c = lax.broadcasted_iota(jnp.int32, (cs, cs), 1)
    eye_bf = jnp.where(r == c, 1.0, 0.0).astype(BF16)
    incl = (r >= c, r <= c)
    strict = (r > c, r < c)

    q_all = q_ref[0]
    k_all = k_ref[0]
    v_all = v_ref[0]
    gates = gates_ref[0]
    gt = gt_ref[0]

    pairs = [(ci, h) for ci in range(nc) for h in range(heads)]
    probs = [(ci, h, d) for d in range(2) for ci in range(nc) for h in range(heads)]

    def tile(x, ci, h):
        return x[ci * cs:(ci + 1) * cs, h * hd:(h + 1) * hd]

    q = {p: tile(q_all, *p) for p in pairs}
    k = {p: tile(k_all, *p) for p in pairs}
    v = {p: tile(v_all, *p) for p in pairs}
    k_bf = {p: k[p].astype(BF16) for p in pairs}
    kk = {p: _dot_nt(k_bf[p], k_bf[p]) for p in pairs}
    qk = {p: _dot_nt(q[p].astype(BF16), k_bf[p]) for p in pairs}

    beta, gc, decay = {}, {}, {}
    for (ci, h, d) in probs:
        rows = slice(ci * cs, (ci + 1) * cs)
        beta[ci, h, d] = gates[rows, 4 * d + h:4 * d + h + 1]
        gc_col = gates[rows, 8 + 4 * d + h:8 + 4 * d + h + 1]
        gc_row = gt[8 + 4 * d + h:8 + 4 * d + h + 1, rows]
        gc[ci, h, d] = gc_col
        decay[ci, h, d] = jnp.exp(jnp.where(incl[d], gc_col - gc_row, -jnp.inf))

    lmats = [jnp.where(strict[d], beta[ci, h, d] * kk[ci, h] * decay[ci, h, d], 0.0)
             for (ci, h, d) in probs]
    t_inv = [t.astype(BF16) for t in _unit_tri_inverse_many(lmats)]

    e = {p: jnp.exp(gc[p]) for p in probs}
    gc_last = {(ci, h, d): (gc[ci, h, d][0:1] if d == 1 else gc[ci, h, d][cs - 1:cs])
               for (ci, h, d) in probs}
    u = {p: _dot(t, (v[p[0], p[1]] * beta[p]).astype(BF16)) for p, t in zip(probs, t_inv)}
    w = {p: _dot(t, (k[p[0], p[1]] * (beta[p] * e[p])).astype(BF16)) for p, t in zip(probs, t_inv)}
    kdec_t = {p: _dot_nt(eye_bf, (k[p[0], p[1]] * jnp.exp(gc_last[p] - gc[p])).astype(BF16))
              for p in probs}

    for d in range(2):
        u_ref, w_ref, qg_ref, aqk_ref, kdt_ref, gl_ref = out_refs[6 * d:6 * d + 6]
        gl_rows = []
        for ci in range(nc):
            rows = slice(ci * cs, (ci + 1) * cs)
            u_ref[0, rows, :] = _lane_concat([u[ci, h, d] for h in range(heads)])
            w_ref[0, rows, :] = _lane_concat([w[ci, h, d] for h in range(heads)]).astype(BF16)
            qg_ref[0, rows, :] = _lane_concat(
                [q[ci, h] * e[ci, h, d] for h in range(heads)]).astype(BF16)
            aqk_ref[0, rows, :] = _lane_concat(
                [qk[ci, h] * decay[ci, h, d] for h in range(heads)]).astype(BF16)
            gl_rows.append(_lane_concat(
                [jnp.broadcast_to(jnp.exp(gc_last[ci, h, d]), (1, hd)) for h in range(heads)]))
        kdt_ref[0, 0] = jnp.concatenate(
            [_lane_concat([kdec_t[ci, h, d] for ci in range(nc)]) for h in range(heads)],
            axis=0).astype(BF16)
        gl_rows.append(jnp.zeros((SUBLANES - nc, heads * hd), F32))
        gl_ref[0, 0] = jnp.concatenate(gl_rows, axis=0)


def _delta_prep(q, k, v, gates, gates_t):
    bsz, seq, width = q.shape
    ta = DELTA_TILE
    nt = seq // ta
    main = lambda b, i: (b, i, 0)
    per_dir_shapes = [jax.ShapeDtypeStruct((bsz, seq, width), F32),
                      jax.ShapeDtypeStruct((bsz, seq, width), BF16),
                      jax.ShapeDtypeStruct((bsz, seq, width), BF16),
                      jax.ShapeDtypeStruct((bsz, seq, width), BF16),
                      jax.ShapeDtypeStruct((bsz, nt, width, ta), BF16),
                      jax.ShapeDtypeStruct((bsz, nt, SUBLANES, width), F32)]
    per_dir_specs = [pl.BlockSpec((1, ta, width), main)] * 4 + [
        pl.BlockSpec((1, 1, width, ta), lambda b, i: (b, i, 0, 0)),
        pl.BlockSpec((1, 1, SUBLANES, width), lambda b, i: (b, i, 0, 0))]
    return pl.pallas_call(
        _delta_prep_kernel,
        out_shape=per_dir_shapes * 2,
        grid=(bsz, nt),
        in_specs=[pl.BlockSpec((1, ta, width), main)] * 3 + [
            pl.BlockSpec((1, ta, LANES), main),
            pl.BlockSpec((1, 2 * SUBLANES, ta), lambda b, i: (b, 0, i))],
        out_specs=per_dir_specs * 2,
        compiler_params=_cparams(("arbitrary", "arbitrary"), big=True),
        name="delta_prep",
    )(q, k, v, gates, gates_t)


def _delta_scan_kernel(*refs):
    fwd = refs[0:6]
    bwd = refs[6:12]
    s0_ref = refs[12]
    of_ref, ob_ref, send_ref, s_scr = refs[13:17]
    t = pl.program_id(1)
    cs = DELTA_CHUNK
    hd = HEAD_DIM
    ta = fwd[0].shape[1]
    nc = ta // cs
    width = fwd[0].shape[2]
    heads = width // hd
    rr = lax.broadcasted_iota(jnp.int32, (width, width), 0)
    cc = lax.broadcasted_iota(jnp.int32, (width, width), 1)
    same_head = lax.shift_right_logical(rr, 6) == lax.shift_right_logical(cc, 6)
    lane_head = lax.shift_right_logical(lax.broadcasted_iota(jnp.int32, (cs, width), 1), 6)

    @pl.when(t == 0)
    def _():
        s_scr[...] = s0_ref[0]

    dirs = ((fwd, of_ref), (bwd, ob_ref))
    for step in range(nc):
        chunk = (step, nc - 1 - step)
        rows = [slice(chunk[d] * cs, (chunk[d] + 1) * cs) for d in range(2)]
        s_prev = [s_scr[d] for d in range(2)]
        lhs = [jnp.concatenate([dirs[d][0][1][0, rows[d], :], dirs[d][0][2][0, rows[d], :]], axis=0)
               for d in range(2)]
        res = [_dot(lhs[d], s_prev[d].astype(BF16)) for d in range(2)]
        v_new = [(dirs[d][0][0][0, rows[d], :] - res[d][0:cs]).astype(BF16) for d in range(2)]
        v_bd = [jnp.concatenate([jnp.where(lane_head == h, v_new[d], jnp.zeros_like(v_new[d]))
                                 for h in range(heads)], axis=0) for d in range(2)]
        zeros = jnp.zeros((cs, width), BF16)
        v_tile = [jnp.concatenate([v_new[d] if ci == chunk[d] else zeros for ci in range(nc)], axis=0)
                  for d in range(2)]
        o = [res[d][cs:2 * cs] + _dot(dirs[d][0][3][0, rows[d], :], v_bd[d]) for d in range(2)]
        upd = [_dot(dirs[d][0][4][0, 0], v_tile[d]) for d in range(2)]
        for d in range(2):
            gl = dirs[d][0][5][0, 0, chunk[d]:chunk[d] + 1, :]
            s_scr[d] = s_prev[d] * gl + jnp.where(same_head, upd[d], 0.0)
            dirs[d][1][0, rows[d], :] = o[d]

    @pl.when(t == pl.num_programs(1) - 1)
    def _():
        send_ref[0] = s_scr[...]


def _delta_scan(prep, s0_bd):
    u_f = prep[0]
    bsz, seq, width = u_f.shape
    ta = DELTA_TILE
    nt = seq // ta

    def specs(index3, index4):
        return [pl.BlockSpec((1, ta, width), index3)] * 4 + [
            pl.BlockSpec((1, 1, width, ta), index4),
            pl.BlockSpec((1, 1, SUBLANES, width), index4)]

    fwd3 = lambda b, t: (b, t, 0)
    fwd4 = lambda b, t: (b, t, 0, 0)
    bwd3 = lambda b, t: (b, nt - 1 - t, 0)
    bwd4 = lambda b, t: (b, nt - 1 - t, 0, 0)
    state_spec = pl.BlockSpec((1, 2, width, width), lambda b, t: (b, 0, 0, 0))
    return pl.pallas_call(
        _delta_scan_kernel,
        out_shape=[jax.ShapeDtypeStruct((bsz, seq, width), F32),
                   jax.ShapeDtypeStruct((bsz, seq, width), F32),
                   jax.ShapeDtypeStruct((bsz, 2, width, width), F32)],
        grid=(bsz, nt),
        in_specs=specs(fwd3, fwd4) + specs(bwd3, bwd4) + [state_spec],
        out_specs=[pl.BlockSpec((1, ta, width), fwd3), pl.BlockSpec((1, ta, width), bwd3),
                   state_spec],
        scratch_shapes=[pltpu.VMEM((2, width, width), F32)],
        compiler_params=_cparams(("arbitrary", "arbitrary")),
        name="delta_scan",
    )(*prep, s0_bd)


def _delta_out_kernel(of_ref, ob_ref, z_ref, ones_ref, g_ref, o_ref):
    ones_bd = ones_ref[...]
    for s in range(of_ref.shape[2] // LANES):
        lanes = slice(s * LANES, (s + 1) * LANES)
        o = of_ref[0, :, lanes] + ob_ref[0, :, lanes]
        ms = _group_sumsq(o, ones_bd) * (1.0 / HEAD_DIM)
        y = o * lax.rsqrt(ms + EPS) * g_ref[...]
        o_ref[0, :, lanes] = (y * _silu(z_ref[0, :, lanes])).astype(BF16)


def _delta_out(o_f, o_b, pc, ones_bd, norm_g):
    bsz, seq, width = o_f.shape
    tl = min(512, seq)
    main = lambda b, i: (b, i, 0)
    const2 = lambda b, i: (0, 0)
    z_block = pc.shape[2] // width - 1
    return pl.pallas_call(
        _delta_out_kernel,
        out_shape=jax.ShapeDtypeStruct((bsz, seq, width), BF16),
        grid=(bsz, seq // tl),
        in_specs=[pl.BlockSpec((1, tl, width), main), pl.BlockSpec((1, tl, width), main),
                  pl.BlockSpec((1, tl, width), lambda b, i: (b, i, z_block)),
                  pl.BlockSpec((LANES, LANES), const2),
                  pl.BlockSpec((1, LANES), const2)],
        out_specs=pl.BlockSpec((1, tl, width), main),
        compiler_params=_cparams(("arbitrary", "arbitrary")),
        name="delta_out",
    )(o_f, o_b, pc, ones_bd, jnp.tile(norm_g, LANES // HEAD_DIM).reshape(1, LANES))


def _deltanet(q, k, v, pc, gates, gates_t, s0, norm_g, ones_bd):
    bsz = q.shape[0]
    heads, hd = N_HEADS_C, HEAD_DIM
    eye_h = jnp.eye(heads, dtype=F32)
    s0_bd = jnp.einsum('bdhij,hg->bdhigj', s0.astype(F32), eye_h).reshape(
        bsz, 2, heads * hd, heads * hd)
    prep = _delta_prep(q, k, v, gates, gates_t)
    o_f, o_b, s_bd = _delta_scan(prep, s0_bd)
    o_c = _delta_out(o_f, o_b, pc, ones_bd, norm_g)
    s_blocks = s_bd.reshape(bsz, 2, heads, hd, heads, hd)
    s_end = jnp.stack([s_blocks[:, :, h, :, h, :] for h in range(heads)], axis=2)
    return o_c, s_end


def _fourier_ch_kernel(x_ref, c_ref, s_ref, w_ref):
    x = x_ref[0].astype(BF16)
    w_ref[0] = _dot(x, c_ref[...].astype(BF16)).astype(BF16)
    w_ref[1] = _dot(x, s_ref[...].astype(BF16)).astype(BF16)


def _fourier_channels(xd, c_bd, s_bd):
    bsz, seq, gw = xd.shape
    tl = min(512, seq)
    const2 = lambda b, i: (0, 0)
    return pl.pallas_call(
        _fourier_ch_kernel,
        out_shape=jax.ShapeDtypeStruct((2, seq, bsz * gw), BF16),
        grid=(bsz, seq // tl),
        in_specs=[pl.BlockSpec((1, tl, gw), lambda b, i: (b, i, 0)),
                  pl.BlockSpec((gw, gw), const2),
                  pl.BlockSpec((gw, gw), const2)],
        out_specs=pl.BlockSpec((2, tl, gw), lambda b, i: (0, i, b)),
        compiler_params=_cparams(("arbitrary", "arbitrary")),
        name="fourier_channels",
    )(xd, c_bd, s_bd)


def _fourier_pos_kernel(w_ref, a1_ref, b1_ref, a0_ref, b0_ref, o_ref, c_scr, s_scr, *, scale):
    i = pl.program_id(1)
    tm = c_scr.shape[0]
    sub = a0_ref.shape[0]
    a0 = a0_ref[...]
    b0 = b0_ref[...]
    for rblk in range(tm // sub):
        j1 = i * (tm // sub) + rblk
        a1 = a1_ref[pl.ds(j1, 1), :]
        b1 = b1_ref[pl.ds(j1, 1), :]
        c_scr[rblk * sub:(rblk + 1) * sub, :] = (a1 * a0 - b1 * b0).astype(BF16)
        s_scr[rblk * sub:(rblk + 1) * sub, :] = (b1 * a0 + a1 * b0).astype(BF16)
    res = (_dot(c_scr[...], w_ref[0]) - _dot(s_scr[...], w_ref[1])) * scale
    gw = o_ref.shape[2]
    for b in range(o_ref.shape[0]):
        o_ref[b] = res[:, b * gw:(b + 1) * gw].astype(BF16)


def _fourier_positions(w, tabs, bsz, gw):
    _, seq, ncol = w.shape
    a1, b1, a0, b0 = tabs
    tm = min(512, seq)
    nsplit = 2 if (bsz % 2 == 0 and seq > 1024) else 1
    bpart = bsz // nsplit
    const2 = lambda p, i: (0, 0)
    single = pl.Buffered(1)
    scale = 1.0 / np.sqrt(float(seq) * HEAD_DIM)
    return pl.pallas_call(
        functools.partial(_fourier_pos_kernel, scale=scale),
        out_shape=jax.ShapeDtypeStruct((bsz, seq, gw), BF16),
        grid=(nsplit, seq // tm),
        in_specs=[pl.BlockSpec((2, seq, bpart * gw), lambda p, i: (0, 0, p), pipeline_mode=single),
                  pl.BlockSpec(a1.shape, const2, pipeline_mode=single),
                  pl.BlockSpec(b1.shape, const2, pipeline_mode=single),
                  pl.BlockSpec(a0.shape, const2, pipeline_mode=single),
                  pl.BlockSpec(b0.shape, const2, pipeline_mode=single)],
        out_specs=pl.BlockSpec((bpart, tm, gw), lambda p, i: (p, i, 0)),
        scratch_shapes=[pltpu.VMEM((tm, seq), BF16), pltpu.VMEM((tm, seq), BF16)],
        compiler_params=_cparams(("arbitrary", "arbitrary"), big=True),
        name="fourier_positions",
    )(w, a1, b1, a0, b0)


def _rope_tables(seq):
    rows = seq // GRID_W
    r, cl = jnp.meshgrid(jnp.arange(rows), jnp.arange(GRID_W), indexing='ij')
    pos_r = r.reshape(-1).astype(F32)
    pos_c = cl.reshape(-1).astype(F32)
    quarter = HEAD_DIM // 4
    inv_freq = ROPE_THETA ** (-jnp.arange(quarter, dtype=F32) / quarter)
    cos_parts, sin_parts = [], []
    for pos in (pos_r, pos_c):
        ang = pos[:, None] * inv_freq[None, :]
        cos_parts += [jnp.cos(ang), jnp.cos(ang)]
        sin_parts += [-jnp.sin(ang), jnp.sin(ang)]
    cos = jnp.concatenate(cos_parts, axis=-1)
    sin = jnp.concatenate(sin_parts, axis=-1)
    return jnp.tile(cos, (1, 2)), jnp.tile(sin, (1, 2))


def _trig_table(num_rows, row_mult, seq):
    j = np.arange(num_rows, dtype=np.int64)[:, None] * row_mult
    k = np.arange(seq, dtype=np.int64)[None, :]
    ang = 2.0 * np.pi * ((j * k) % seq).astype(np.float64) / seq
    return jnp.asarray(np.cos(ang), F32), jnp.asarray(np.sin(ang), F32)


def _fourier_tables(seq):
    sub = 64
    a1, b1 = _trig_table(seq // sub, sub, seq)
    a0, b0 = _trig_table(sub, 1, seq)
    idx = np.arange(GROUP_WIDTH)
    same = (idx[:, None] // HEAD_DIM) == (idx[None, :] // HEAD_DIM)
    ang = 2.0 * np.pi * ((idx[:, None] % HEAD_DIM) * (idx[None, :] % HEAD_DIM) % HEAD_DIM) / HEAD_DIM
    c_bd = jnp.asarray(np.where(same, np.cos(ang), 0.0), F32)
    s_bd = jnp.asarray(np.where(same, np.sin(ang), 0.0), F32)
    return (a1, b1, a0, b0), c_bd, s_bd


def _to_lane_dense(t):
    b, hh, l, d = t.shape
    return jnp.swapaxes(t, 1, 2).reshape(b, l, hh * d)


def _to_head_major(t, heads):
    b, l, w = t.shape
    return jnp.swapaxes(t.reshape(b, l, heads, w // heads), 1, 2)


def _swap_halves(t):
    return jnp.concatenate([t[..., HEAD_DIM:], t[..., :HEAD_DIM]], axis=-1)


def _trunk_layer(x, mod, wts, consts, ctx, final_g):
    bsz, seq, _ = x.shape
    x1 = _ffn(x, mod, wts['norm_ffn1'], wts['ffn1_w1'], wts['ffn1_w3'], wts['ffn1_w2'], which=0)
    pa, pb, pc, pd, ps = _inproj(x1, mod, wts['norm_mix'], wts['w_in'])

    rope_tabs = consts['rope'] if ctx is not None else None
    prep = _attn_prep(pa, consts['ones_bd'], wts['gq'], wts['gk'], rope_tabs, ctx is None)
    q, ka, kb, va, vb = prep[:5]
    k_cache = v_cache = None
    if ctx is None:
        k_cache = _to_head_major(prep[5], KV_HEADS_A)
        v_cache = _to_head_major(pa[..., 384:512], KV_HEADS_A)
        s0 = jnp.zeros((bsz, 2, N_HEADS_C, HEAD_DIM, HEAD_DIM), F32)
    else:
        k_ctx, v_ctx, s0 = ctx
        k_past = _to_lane_dense(k_ctx).astype(BF16)
        v_past = _to_lane_dense(v_ctx).astype(BF16)
        ka = jnp.concatenate([k_past, ka], axis=1)
        kb = jnp.concatenate([_swap_halves(k_past), kb], axis=1)
        va = jnp.concatenate([v_past, va], axis=1)
        vb = jnp.concatenate([_swap_halves(v_past), vb], axis=1)
    o_a = _attention(q, ka, kb, va, vb)

    o_b, qc, kc, vc, gates, gates_t = _conv_prep(pb, pc, ps, wts['conv_b_w'], wts['conv_c_w'],
                                                 consts['ones_bd'], wts['alog_row'], wts['dtb_row'])
    o_c, s_end = _deltanet(qc, kc, vc, pc, gates, gates_t, s0, wts['delta_norm'], consts['ones_bd'])

    w = _fourier_channels(pd, consts['c_bd'], consts['s_bd'])
    o_d = _fourier_positions(w, consts['fourier'], bsz, GROUP_WIDTH)

    x3 = _ffn(x1, mod, wts['norm_ffn2'], wts['ffn2_w1'], wts['ffn2_w3'], wts['ffn2_w2'], which=2,
              mix=(o_a, o_b, o_c, o_d), w_out=wts['w_out'], final_g=final_g)
    return x3, k_cache, v_cache, s_end


def _lane_row(values, start):
    row = jnp.zeros((1, LANES), F32)
    return row.at[0, start:start + values.shape[0]].set(values.astype(F32))


def kernel(x_prompt, x_sample, c, cache_k, cache_v, state_delta, c_ctx, mod_w, mod_b, norm_ffn1, norm_mix, norm_ffn2, ffn1_w1, ffn1_w3, ffn1_w2, ffn2_w1, ffn2_w3, ffn2_w2, w_in, w_out, q_norm, k_norm, conv_b_w, conv_c_w, delta_a_log, delta_dt_bias, delta_norm, final_norm):
    depth = mod_w.shape[0]
    d_model = x_prompt.shape[-1]
    dec_b = x_sample.shape[0]
    assert w_in.shape[-1] == PROJ_MAIN + PROJ_SMALL + GROUP_WIDTH

    n_cond = 1 + dec_b
    rows = -(-n_cond // SUBLANES) * SUBLANES
    cond = jnp.zeros((rows, d_model), F32).at[0].set(c_ctx).at[1:n_cond].set(c)

    idx = np.arange(LANES)
    ones_bd = jnp.asarray((idx[:, None] // HEAD_DIM) == (idx[None, :] // HEAD_DIM), BF16)
    consts_p = {'ones_bd': ones_bd}
    consts_s = {'ones_bd': ones_bd, 'rope': _rope_tables(x_sample.shape[1])}
    consts_p['fourier'], consts_p['c_bd'], consts_p['s_bd'] = _fourier_tables(x_prompt.shape[1])
    consts_s['fourier'], consts_s['c_bd'], consts_s['s_bd'] = _fourier_tables(x_sample.shape[1])

    yp, ys = x_prompt, x_sample
    k_list, v_list, s_list = [], [], []
    for l in range(depth):
        w_in_l = w_in[l]
        small = w_in_l[:, PROJ_MAIN:PROJ_MAIN + PROJ_SMALL]
        w_in_r = jnp.concatenate(
            [w_in_l[:, :PROJ_MAIN], w_in_l[:, PROJ_MAIN + PROJ_SMALL:], small,
             jnp.zeros((d_model, LANES - PROJ_SMALL), F32)], axis=1).astype(BF16)
        wts = {
            'norm_ffn1': norm_ffn1[l], 'norm_mix': norm_mix[l], 'norm_ffn2': norm_ffn2[l],
            'ffn1_w1': ffn1_w1[l].astype(BF16), 'ffn1_w3': ffn1_w3[l].astype(BF16),
            'ffn1_w2': ffn1_w2[l].astype(BF16),
            'ffn2_w1': ffn2_w1[l].astype(BF16), 'ffn2_w3': ffn2_w3[l].astype(BF16),
            'ffn2_w2': ffn2_w2[l].astype(BF16),
            'w_in': w_in_r, 'w_out': w_out[l].astype(BF16),
            'gq': jnp.tile(q_norm[l], 2).reshape(1, LANES),
            'gk': jnp.tile(k_norm[l], 2).reshape(1, LANES),
            'conv_b_w': conv_b_w[l], 'conv_c_w': conv_c_w[l],
            'alog_row': _lane_row(delta_a_log[l].reshape(-1), 8),
            'dtb_row': _lane_row(delta_dt_bias[l].reshape(-1), 8),
            'delta_norm': delta_norm[l],
        }
        mod = _modulation(cond, mod_w[l], mod_b[l]).reshape(rows, N_MOD, d_model)
        fin = final_norm if l == depth - 1 else None
        yp, k_l, v_l, s_l = _trunk_layer(yp, mod[0:1], wts, consts_p, None, fin)
        k_list.append(k_l)
        v_list.append(v_l)
        s_list.append(s_l)
        ys, _, _, _ = _trunk_layer(ys, mod[1:n_cond], wts, consts_s,
                                   (cache_k[:, l], cache_v[:, l], state_delta[:, l]), fin)
    return (yp, ys, jnp.stack(k_list, axis=1), jnp.stack(v_list, axis=1),
            jnp.stack(s_list, axis=1))
```

```python
import functools

import jax
import jax.numpy as jnp
import numpy as np
from jax import lax
from jax.experimental import pallas as pl
from jax.experimental.pallas import tpu as pltpu

F32 = jnp.float32
BF16 = jnp.bfloat16

HEAD_DIM = 64
N_HEADS_A = 4
KV_HEADS_A = 2
N_HEADS_C = 4
GROUP_WIDTH = 256
GRID_W = 64
DELTA_CHUNK = 64
ROPE_THETA = 10000.0
N_MOD = 9
EPS = 1e-6
PROJ_MAIN = 2304
PROJ_SMALL = 16
LANES = 128
SUBLANES = 8
VMEM_LIMIT_BYTES = 56 * 1024 * 1024


def _cparams(sem, big=False):
    return pltpu.CompilerParams(
        dimension_semantics=sem,
        vmem_limit_bytes=VMEM_LIMIT_BYTES if big else None)


def _dot(a, b):
    return jnp.dot(a, b, preferred_element_type=F32)


def _dot_nt(a, b):
    return lax.dot_general(a, b, (((1,), (1,)), ((), ())), preferred_element_type=F32)


def _split2(x):
    hi = x.astype(BF16)
    lo = (x - hi.astype(F32)).astype(BF16)
    return hi, lo


def _split3(x):
    hi = x.astype(BF16)
    r = x - hi.astype(F32)
    mid = r.astype(BF16)
    lo = (r - mid.astype(F32)).astype(BF16)
    return hi, mid, lo


def _rms(x, g):
    return x * lax.rsqrt(jnp.mean(x * x, axis=-1, keepdims=True) + EPS) * g


def _silu(x):
    return x * jax.nn.sigmoid(x)


def _group_sumsq(x, ones_bd):
    hi, lo = _split2(x * x)
    return _dot(hi, ones_bd) + _dot(lo, ones_bd)


def _mod_kernel(c_ref, w_ref, b_ref, o_ref):
    s = _silu(c_ref[...]).astype(BF16)
    o_ref[...] = _dot(s, w_ref[...].astype(BF16)) + b_ref[...]


def _modulation(cond, mod_w, mod_b):
    rows, d = cond.shape
    n = mod_w.shape[1]
    tn = d
    return pl.pallas_call(
        _mod_kernel,
        out_shape=jax.ShapeDtypeStruct((rows, n), F32),
        grid=(n // tn,),
        in_specs=[pl.BlockSpec((rows, d), lambda j: (0, 0)),
                  pl.BlockSpec((d, tn), lambda j: (0, j)),
                  pl.BlockSpec((1, tn), lambda j: (0, j))],
        out_specs=pl.BlockSpec((rows, tn), lambda j: (0, j)),
        compiler_params=_cparams(("arbitrary",)),
        name="modulation",
    )(cond, mod_w, mod_b.reshape(1, n))


def _ffn_chunks(f):
    step = 1024
    return [(s, min(s + step, f)) for s in range(0, f, step)]


def _ffn_kernel(*refs, which, n_mix, final):
    x_ref, mod_ref, g_ref, w1_ref, w3_ref, w2_ref = refs[:6]
    pos = 6
    mix_refs = refs[pos:pos + n_mix]
    pos += n_mix
    wout_ref = None
    if n_mix:
        wout_ref = refs[pos]
        pos += 1
    gf_ref = None
    if final:
        gf_ref = refs[pos]
        pos += 1
    o_ref = refs[pos]

    x = x_ref[0]
    mod = mod_ref[0]
    if n_mix:
        acc = None
        for i, m_ref in enumerate(mix_refs):
            w = m_ref.shape[-1]
            part = _dot(m_ref[0], wout_ref[i * w:(i + 1) * w, :])
            acc = part if acc is None else acc + part
        x = x + mod[5:6] * acc
    sh = mod[3 * which:3 * which + 1]
    sc = mod[3 * which + 1:3 * which + 2]
    gt = mod[3 * which + 2:3 * which + 3]
    h = (_rms(x, g_ref[...]) * (1.0 + sc) + sh).astype(BF16)
    out = None
    for s, e in _ffn_chunks(w1_ref.shape[1]):
        a = _dot(h, w1_ref[:, s:e])
        b = _dot(h, w3_ref[:, s:e])
        act = (_silu(a) * b).astype(BF16)
        part = _dot(act, w2_ref[s:e, :])
        out = part if out is None else out + part
    xn = x + 0.5 * gt * out
    if final:
        xn = _rms(xn, gf_ref[...])
    o_ref[0] = xn


def _ffn(x, mod, g, w1, w3, w2, *, which, mix=None, w_out=None, final_g=None):
    bsz, seq, d = x.shape
    f = w1.shape[1]
    tm = min(512, seq)
    per_batch_mod = mod.shape[0] > 1
    mod_map = (lambda b, i: (b, 0, 0)) if per_batch_mod else (lambda b, i: (0, 0, 0))
    const2 = lambda b, i: (0, 0)
    single = pl.Buffered(1)
    args = [x, mod, g.reshape(1, d), w1, w3, w2]
    specs = [pl.BlockSpec((1, tm, d), lambda b, i: (b, i, 0)),
             pl.BlockSpec((1, N_MOD, d), mod_map),
             pl.BlockSpec((1, d), const2),
             pl.BlockSpec((d, f), const2, pipeline_mode=single),
             pl.BlockSpec((d, f), const2, pipeline_mode=single),
             pl.BlockSpec((f, d), const2, pipeline_mode=single)]
    n_mix = 0
    if mix is not None:
        n_mix = len(mix)
        for m in mix:
            args.append(m)
            specs.append(pl.BlockSpec((1, tm, m.shape[-1]), lambda b, i: (b, i, 0)))
        args.append(w_out)
        specs.append(pl.BlockSpec(w_out.shape, const2, pipeline_mode=single))
    if final_g is not None:
        args.append(final_g.reshape(1, d))
        specs.append(pl.BlockSpec((1, d), const2))
    return pl.pallas_call(
        functools.partial(_ffn_kernel, which=which, n_mix=n_mix, final=final_g is not None),
        out_shape=jax.ShapeDtypeStruct((bsz, seq, d), F32),
        grid=(bsz, seq // tm),
        in_specs=specs,
        out_specs=pl.BlockSpec((1, tm, d), lambda b, i: (b, i, 0)),
        compiler_params=_cparams(("arbitrary", "arbitrary"), big=True),
        name="ffn",
    )(*args)


_PROJ_SPLITS = ((0, 512), (512, 1280), (1280, 2304), (2304, 2560), (2560, 2688))


def _inproj_kernel(x_ref, mod_ref, g_ref, w_ref, *o_refs):
    mod = mod_ref[0]
    h = (_rms(x_ref[0], g_ref[...]) * (1.0 + mod[4:5]) + mod[3:4]).astype(BF16)
    for (s, e), o_ref in zip(_PROJ_SPLITS, o_refs):
        o_ref[0] = _dot(h, w_ref[:, s:e])


def _inproj(x, mod, g, w_in_r):
    bsz, seq, d = x.shape
    tm = min(512, seq)
    per_batch_mod = mod.shape[0] > 1
    mod_map = (lambda b, i: (b, 0, 0)) if per_batch_mod else (lambda b, i: (0, 0, 0))
    const2 = lambda b, i: (0, 0)
    widths = [e - s for s, e in _PROJ_SPLITS]
    return pl.pallas_call(
        _inproj_kernel,
        out_shape=[jax.ShapeDtypeStruct((bsz, seq, w), F32) for w in widths],
        grid=(bsz, seq // tm),
        in_specs=[pl.BlockSpec((1, tm, d), lambda b, i: (b, i, 0)),
                  pl.BlockSpec((1, N_MOD, d), mod_map),
                  pl.BlockSpec((1, d), const2),
                  pl.BlockSpec(w_in_r.shape, const2, pipeline_mode=pl.Buffered(1))],
        out_specs=[pl.BlockSpec((1, tm, w), lambda b, i: (b, i, 0)) for w in widths],
        compiler_params=_cparams(("arbitrary", "arbitrary"), big=True),
        name="inproj",
    )(x, mod, g.reshape(1, d), w_in_r)


def _rope(x, cos, sin_signed):
    lane = lax.broadcasted_iota(jnp.int32, x.shape, 1)
    up = pltpu.roll(x, LANES - 16, axis=1)
    down = pltpu.roll(x, 16, axis=1)
    partner = jnp.where((lane % 32) < 16, up, down)
    return x * cos + partner * sin_signed


def _attn_prep_kernel(*refs, rope, want_cache):
    pa_ref, ones_ref, gq_ref, gk_ref = refs[:4]
    pos = 4
    if rope:
        cos_ref, sin_ref = refs[pos:pos + 2]
        pos += 2
    q_ref, ka_ref, kb_ref, va_ref, vb_ref = refs[pos:pos + 5]
    pos += 5
    pa = pa_ref[0]
    ones_bd = ones_ref[...]

    def normed(x, g):
        ss = _group_sumsq(x, ones_bd)
        return x * lax.rsqrt(ss * (1.0 / HEAD_DIM) + EPS) * g

    q0 = normed(pa[:, 0:128], gq_ref[...])
    q1 = normed(pa[:, 128:256], gq_ref[...])
    k = normed(pa[:, 256:384], gk_ref[...])
    v = pa[:, 384:512]
    if want_cache:
        refs[pos][0] = k
    if rope:
        cos = cos_ref[...]
        sin = sin_ref[...]
        q0 = _rope(q0, cos, sin)
        q1 = _rope(q1, cos, sin)
        k = _rope(k, cos, sin)
    scale = HEAD_DIM ** -0.5
    q_ref[0, :, 0:128] = (q0 * scale).astype(BF16)
    q_ref[0, :, 128:256] = (q1 * scale).astype(BF16)
    ka_ref[0] = k.astype(BF16)
    kb_ref[0] = pltpu.roll(k, HEAD_DIM, axis=1).astype(BF16)
    ones = jnp.ones(v.shape, BF16)
    va_ref[0, :, 0:LANES] = v.astype(BF16)
    va_ref[0, :, LANES:2 * LANES] = ones
    vb_ref[0, :, 0:LANES] = pltpu.roll(v, HEAD_DIM, axis=1).astype(BF16)
    vb_ref[0, :, LANES:2 * LANES] = ones


def _attn_prep(pa, ones_bd, gq, gk, rope_tabs, want_cache):
    bsz, seq, _ = pa.shape
    tl = min(512, seq)
    rope = rope_tabs is not None
    const2 = lambda b, i: (0, 0)
    args = [pa, ones_bd, gq, gk]
    specs = [pl.BlockSpec((1, tl, 512), lambda b, i: (b, i, 0)),
             pl.BlockSpec((LANES, LANES), const2),
             pl.BlockSpec((1, LANES), const2),
             pl.BlockSpec((1, LANES), const2)]
    if rope:
        args += list(rope_tabs)
        specs += [pl.BlockSpec((tl, LANES), lambda b, i: (i, 0))] * 2
    outs = [jax.ShapeDtypeStruct((bsz, seq, 256), BF16)]
    outs += [jax.ShapeDtypeStruct((bsz, seq, LANES), BF16)] * 2
    outs += [jax.ShapeDtypeStruct((bsz, seq, 2 * LANES), BF16)] * 2
    ospecs = [pl.BlockSpec((1, tl, 256), lambda b, i: (b, i, 0))]
    ospecs += [pl.BlockSpec((1, tl, LANES), lambda b, i: (b, i, 0))] * 2
    ospecs += [pl.BlockSpec((1, tl, 2 * LANES), lambda b, i: (b, i, 0))] * 2
    if want_cache:
        outs.append(jax.ShapeDtypeStruct((bsz, seq, LANES), F32))
        ospecs.append(pl.BlockSpec((1, tl, LANES), lambda b, i: (b, i, 0)))
    return pl.pallas_call(
        functools.partial(_attn_prep_kernel, rope=rope, want_cache=want_cache),
        out_shape=outs,
        grid=(bsz, seq // tl),
        in_specs=specs,
        out_specs=ospecs,
        compiler_params=_cparams(("arbitrary", "arbitrary")),
        name="attn_prep",
    )(*args)


def _attn_kernel(q_ref, ka_ref, kb_ref, va_ref, vb_ref, o_ref):
    tq = q_ref.shape[1]
    lane = lax.broadcasted_iota(jnp.int32, (tq, LANES), 1)
    low = lane < HEAD_DIM
    arrangement = ((ka_ref, va_ref), (kb_ref, vb_ref), (kb_ref, vb_ref), (ka_ref, va_ref))
    def scores(h):
        slab = q_ref[0, :, (h // 2) * LANES:(h // 2 + 1) * LANES]
        keep = low if h % 2 == 0 else jnp.logical_not(low)
        qh = jnp.where(keep, slab, jnp.zeros_like(slab))
        return _dot_nt(qh, arrangement[h][0][0])

    def weighted_values(h, s):
        m = jnp.max(s, axis=-1, keepdims=True)
        p = jnp.exp(s - m)
        pv = _dot(p.astype(BF16), arrangement[h][1][0])
        return pv[:, 0:LANES] / pv[:, LANES:LANES + 1]

    outs = []
    s_next = scores(0)
    for h in range(N_HEADS_A):
        s_cur = s_next
        if h + 1 < N_HEADS_A:
            s_next = scores(h + 1)
        outs.append(weighted_values(h, s_cur))
    o_ref[0, :, 0:128] = jnp.where(low, outs[0], outs[1]).astype(BF16)
    o_ref[0, :, 128:256] = jnp.where(low, outs[2], outs[3]).astype(BF16)


def _attention(q, ka, kb, va, vb):
    bsz, seq, _ = q.shape
    lk = ka.shape[1]
    tq = min(256, seq)
    k_spec = pl.BlockSpec((1, lk, LANES), lambda b, i: (b, 0, 0))
    v_spec = pl.BlockSpec((1, lk, 2 * LANES), lambda b, i: (b, 0, 0))
    return pl.pallas_call(
        _attn_kernel,
        out_shape=jax.ShapeDtypeStruct((bsz, seq, 256), BF16),
        grid=(bsz, seq // tq),
        in_specs=[pl.BlockSpec((1, tq, 256), lambda b, i: (b, i, 0)),
                  k_spec, k_spec, v_spec, v_spec],
        out_specs=pl.BlockSpec((1, tq, 256), lambda b, i: (b, i, 0)),
        compiler_params=_cparams(("arbitrary", "arbitrary"), big=True),
        name="attention",
    )(q, ka, kb, va, vb)


def _conv3(x, prev_row, next_row, w):
    rows = x.shape[0]
    ridx = lax.broadcasted_iota(jnp.int32, x.shape, 0)
    xm = jnp.where(ridx == 0, prev_row, pltpu.roll(x, 1, axis=0))
    xp = jnp.where(ridx == rows - 1, next_row, pltpu.roll(x, rows - 1, axis=0))
    return xm * w[0:1] + x * w[1:2] + xp * w[2:3]


def _conv_prep_kernel(pb_ref, pbp_ref, pbn_ref, pc_ref, pcp_ref, pcn_ref, ps_ref,
                      wb_ref, wc_ref, ones_ref, alog_ref, dtb_ref, tril_ref, triu_ref,
                      ob_ref, q_ref, k_ref, v_ref, gates_ref, gatest_ref):
    i = pl.program_id(1)
    has_prev = i > 0
    has_next = i < pl.num_programs(1) - 1
    gw = GROUP_WIDTH

    pb = pb_ref[0]
    u = pb[:, gw:2 * gw] * pb[:, 2 * gw:3 * gw]
    pbp = pbp_ref[0, SUBLANES - 1:SUBLANES, :]
    pbn = pbn_ref[0, 0:1, :]
    u_prev = jnp.where(has_prev, pbp[:, gw:2 * gw] * pbp[:, 2 * gw:3 * gw], 0.0)
    u_next = jnp.where(has_next, pbn[:, gw:2 * gw] * pbn[:, 2 * gw:3 * gw], 0.0)
    ob_ref[0] = (pb[:, 0:gw] * _conv3(u, u_prev, u_next, wb_ref[...])).astype(BF16)

    pc = pc_ref[0]
    x3 = pc[:, 0:3 * gw]
    x_prev = jnp.where(has_prev, pcp_ref[0, SUBLANES - 1:SUBLANES, 0:3 * gw], 0.0)
    x_next = jnp.where(has_next, pcn_ref[0, 0:1, 0:3 * gw], 0.0)
    qkv = _silu(_conv3(x3, x_prev, x_next, wc_ref[...]))
    ones_bd = ones_ref[...]

    def l2n(x):
        return x * lax.rsqrt(_group_sumsq(x, ones_bd) + EPS)

    q_scale = HEAD_DIM ** -0.5
    for s in range(2):
        q_ref[0, :, s * LANES:(s + 1) * LANES] = l2n(qkv[:, s * LANES:(s + 1) * LANES]) * q_scale
        k_ref[0, :, s * LANES:(s + 1) * LANES] = l2n(qkv[:, gw + s * LANES:gw + (s + 1) * LANES])
    v_ref[0] = qkv[:, 2 * gw:3 * gw]

    raw = ps_ref[0]
    beta = jax.nn.sigmoid(raw)
    y = raw + dtb_ref[...]
    softplus = jnp.maximum(y, 0.0) + jnp.log(1.0 + jnp.exp(-jnp.abs(y)))
    g = -jnp.exp(alog_ref[...]) * softplus
    g_hi, g_mid, g_lo = _split3(g)
    tril = tril_ref[...]
    triu = triu_ref[...]
    gc_f = _dot(tril, g_hi) + _dot(tril, g_mid) + _dot(tril, g_lo)
    gc_b = _dot(triu, g_hi) + _dot(triu, g_mid) + _dot(triu, g_lo)
    lane = lax.broadcasted_iota(jnp.int32, raw.shape, 1)
    gates = jnp.where(lane < 8, beta, jnp.where(lane < 12, gc_f, gc_b))
    gates_ref[0] = gates
    gatest_ref[0] = jnp.transpose(gates)[0:2 * SUBLANES, :]


def _conv_prep(pb, pc, ps, conv_b_w, conv_c_w, ones_bd, alog_row, dtb_row):
    bsz, seq, _ = pb.shape
    tl = min(512, seq)
    nblk8 = seq // SUBLANES
    r8 = tl // SUBLANES
    main = lambda b, i: (b, i, 0)
    prev = lambda b, i: (b, jnp.maximum(i * r8 - 1, 0), 0)
    nxt = lambda b, i: (b, jnp.minimum((i + 1) * r8, nblk8 - 1), 0)
    const2 = lambda b, i: (0, 0)
    idx = np.arange(tl)
    same_chunk = (idx[:, None] // DELTA_CHUNK) == (idx[None, :] // DELTA_CHUNK)
    tril = jnp.asarray(same_chunk & (idx[:, None] >= idx[None, :]), BF16)
    triu = jnp.asarray(same_chunk & (idx[:, None] <= idx[None, :]), BF16)
    dense = jax.ShapeDtypeStruct((bsz, seq, GROUP_WIDTH), F32)
    dense_spec = pl.BlockSpec((1, tl, GROUP_WIDTH), main)
    return pl.pallas_call(
        _conv_prep_kernel,
        out_shape=[jax.ShapeDtypeStruct((bsz, seq, GROUP_WIDTH), BF16), dense, dense, dense,
                   jax.ShapeDtypeStruct((bsz, seq, LANES), F32),
                   jax.ShapeDtypeStruct((bsz, 2 * SUBLANES, seq), F32)],
        grid=(bsz, seq // tl),
        in_specs=[pl.BlockSpec((1, tl, 768), main),
                  pl.BlockSpec((1, SUBLANES, 768), prev),
                  pl.BlockSpec((1, SUBLANES, 768), nxt),
                  pl.BlockSpec((1, tl, 1024), main),
                  pl.BlockSpec((1, SUBLANES, 1024), prev),
                  pl.BlockSpec((1, SUBLANES, 1024), nxt),
                  pl.BlockSpec((1, tl, LANES), main),
                  pl.BlockSpec((3, GROUP_WIDTH), const2),
                  pl.BlockSpec((3, 3 * GROUP_WIDTH), const2),
                  pl.BlockSpec((LANES, LANES), const2),
                  pl.BlockSpec((1, LANES), const2),
                  pl.BlockSpec((1, LANES), const2),
                  pl.BlockSpec((tl, tl), const2),
                  pl.BlockSpec((tl, tl), const2)],
        out_specs=[pl.BlockSpec((1, tl, GROUP_WIDTH), main), dense_spec, dense_spec, dense_spec,
                   pl.BlockSpec((1, tl, LANES), main),
                   pl.BlockSpec((1, 2 * SUBLANES, tl), lambda b, i: (b, 0, i))],
        compiler_params=_cparams(("arbitrary", "arbitrary"), big=True),
        name="conv_prep",
    )(pb, pb, pb, pc, pc, pc, ps, conv_b_w, conv_c_w, ones_bd, alog_row, dtb_row, tril, triu)


_TRI_BASE_LOG2 = 3
DELTA_TILE = 4 * DELTA_CHUNK


def _pair_blockdiag(y):
    low = lax.broadcasted_iota(jnp.int32, y.shape, 1) < HEAD_DIM
    zero = jnp.zeros_like(y)
    return jnp.concatenate([jnp.where(low, y, zero), jnp.where(low, zero, y)], axis=0)


def _pair_dot3_many(a_list, b_list):
    n = a_list[0].shape[0]
    lhs = [jnp.concatenate(_split2(a), axis=0) for a in a_list]
    rhs = [jnp.concatenate([_pair_blockdiag(part) for part in _split2(b)], axis=1) for b in b_list]
    res = [_dot(l, rr) for l, rr in zip(lhs, rhs)]
    return [(x[0:n, 0:LANES] + x[n:2 * n, LANES:2 * LANES])
            + (x[0:n, LANES:2 * LANES] + x[n:2 * n, 0:LANES]) for x in res]


def _pair_tri_inverse_many(lmats, r, c):
    def same_block(log2_size):
        return lax.shift_right_logical(r, log2_size) == lax.shift_right_logical(c, log2_size)

    base = same_block(_TRI_BASE_LOG2)
    eye = jnp.where(r == c, 1.0, 0.0)
    a_list = [jnp.where(base, m, 0.0) for m in lmats]
    p_list = [eye - a for a in a_list]
    for _ in range(_TRI_BASE_LOG2 - 1):
        a_list = _pair_dot3_many(a_list, a_list)
        p_list = [p + pa for p, pa in zip(p_list, _pair_dot3_many(p_list, a_list))]
    for log2_size in range(_TRI_BASE_LOG2, int(np.log2(DELTA_CHUNK))):
        couple = jnp.logical_and(same_block(log2_size + 1),
                                 jnp.logical_not(same_block(log2_size)))
        c_list = [jnp.where(couple, m, 0.0) for m in lmats]
        pc_list = _pair_dot3_many(p_list, c_list)
        p_list = [p - pcp for p, pcp in zip(p_list, _pair_dot3_many(pc_list, p_list))]
    return p_list


def _delta_prep_kernel(q_ref, k_ref, v_ref, gates_ref, gt_ref, *out_refs):
    cs = DELTA_CHUNK
    nc = q_ref.shape[1] // cs
    n_slabs = q_ref.shape[2] // LANES
    r = lax.broadcasted_iota(jnp.int32, (cs, LANES), 0)
    lane = lax.broadcasted_iota(jnp.int32, (cs, LANES), 1)
    c = jnp.bitwise_and(lane, HEAD_DIM - 1)
    low = lane < HEAD_DIM
    er = lax.broadcasted_iota(jnp.int32, (LANES, LANES), 0)
    ec = lax.broadcasted_iota(jnp.int32, (LANES, LANES), 1)
    eye_bf = jnp.where(er == ec, 1.0, 0.0).astype(BF16)
    incl = (r >= c, r <= c)
    strict = (r > c, r < c)

    gates = gates_ref[0]
    gt = gt_ref[0]

    pairs = [(ci, s) for ci in range(nc) for s in range(n_slabs)]
    probs = [(ci, s, d) for d in range(2) for ci in range(nc) for s in range(n_slabs)]

    def tile(ref, ci, s):
        return ref[0, ci * cs:(ci + 1) * cs, s * LANES:(s + 1) * LANES]

    q = {p: tile(q_ref, *p) for p in pairs}
    k = {p: tile(k_ref, *p) for p in pairs}
    v = {p: tile(v_ref, *p) for p in pairs}
    k_bf = {p: k[p].astype(BF16) for p in pairs}
    kq = {p: _dot_nt(jnp.concatenate([k_bf[p], q[p].astype(BF16)], axis=0), _pair_blockdiag(k_bf[p]))
          for p in pairs}
    kk = {p: kq[p][0:cs] for p in pairs}
    qk = {p: kq[p][cs:2 * cs] for p in pairs}

    beta, gc, decay = {}, {}, {}
    for (ci, s, d) in probs:
        rows = slice(ci * cs, (ci + 1) * cs)
        col0 = 4 * d + 2 * s
        beta[ci, s, d] = jnp.where(low, gates[rows, col0:col0 + 1], gates[rows, col0 + 1:col0 + 2])
        gc_col = jnp.where(low, gates[rows, 8 + col0:9 + col0], gates[rows, 9 + col0:10 + col0])
        gc_row = jnp.concatenate([gt[8 + col0:9 + col0, rows], gt[9 + col0:10 + col0, rows]], axis=1)
        gc[ci, s, d] = gc_col
        decay[ci, s, d] = jnp.exp(jnp.where(incl[d], gc_col - gc_row, -jnp.inf))

    lmats = [jnp.where(strict[d], beta[ci, s, d] * kk[ci, s] * decay[ci, s, d], 0.0)
             for (ci, s, d) in probs]
    t_inv = [t.astype(BF16) for t in _pair_tri_inverse_many(lmats, r, c)]

    e = {p: jnp.exp(gc[p]) for p in probs}
    gc_last = {(ci, s, d): (gc[ci, s, d][0:1] if d == 1 else gc[ci, s, d][cs - 1:cs])
               for (ci, s, d) in probs}
    uw = {p: _dot(t, jnp.concatenate(
        [_pair_blockdiag((v[p[0], p[1]] * beta[p]).astype(BF16)),
         _pair_blockdiag((k[p[0], p[1]] * (beta[p] * e[p])).astype(BF16))], axis=1))
        for p, t in zip(probs, t_inv)}
    u = {p: uw[p][:, 0:LANES] for p in probs}
    w = {p: uw[p][:, LANES:2 * LANES] for p in probs}
    kdec_t = {p: _dot_nt(eye_bf, (k[p[0], p[1]] * jnp.exp(gc_last[p] - gc[p])).astype(BF16))
              for p in probs}

    for d in range(2):
        u_ref, w_ref, qg_ref, aqk_ref, kdt_ref, gl_ref = out_refs[6 * d:6 * d + 6]
        for (ci, s) in pairs:
            rows = slice(ci * cs, (ci + 1) * cs)
            lanes = slice(s * LANES, (s + 1) * LANES)
            u_ref[0, rows, lanes] = u[ci, s, d]
            w_ref[0, rows, lanes] = w[ci, s, d].astype(BF16)
            qg_ref[0, rows, lanes] = (q[ci, s] * e[ci, s, d]).astype(BF16)
            aqk_ref[0, rows, lanes] = (qk[ci, s] * decay[ci, s, d]).astype(BF16)
            gl_ref[0, 0, ci:ci + 1, lanes] = jnp.exp(gc_last[ci, s, d])
        for s in range(n_slabs):
            kdt_ref[0, 0, s * LANES:(s + 1) * LANES, :] = jnp.concatenate(
                [kdec_t[ci, s, d] for ci in range(nc)], axis=1).astype(BF16)
        gl_ref[0, 0, nc:SUBLANES, :] = jnp.zeros((SUBLANES - nc, gl_ref.shape[3]), F32)


def _delta_prep(q, k, v, gates, gates_t):
    bsz, seq, width = q.shape
    ta = min(DELTA_TILE, seq)
    nt = seq // ta
    main = lambda b, i: (b, i, 0)
    per_dir_shapes = [jax.ShapeDtypeStruct((bsz, seq, width), F32),
                      jax.ShapeDtypeStruct((bsz, seq, width), BF16),
                      jax.ShapeDtypeStruct((bsz, seq, width), BF16),
                      jax.ShapeDtypeStruct((bsz, seq, width), BF16),
                      jax.ShapeDtypeStruct((bsz, nt, width, ta), BF16),
                      jax.ShapeDtypeStruct((bsz, nt, SUBLANES, width), F32)]
    per_dir_specs = [pl.BlockSpec((1, ta, width), main)] * 4 + [
        pl.BlockSpec((1, 1, width, ta), lambda b, i: (b, i, 0, 0)),
        pl.BlockSpec((1, 1, SUBLANES, width), lambda b, i: (b, i, 0, 0))]
    return pl.pallas_call(
        _delta_prep_kernel,
        out_shape=per_dir_shapes * 2,
        grid=(bsz, nt),
        in_specs=[pl.BlockSpec((1, ta, width), main)] * 3 + [
            pl.BlockSpec((1, ta, LANES), main),
            pl.BlockSpec((1, 2 * SUBLANES, ta), lambda b, i: (b, 0, i))],
        out_specs=per_dir_specs * 2,
        compiler_params=_cparams(("arbitrary", "arbitrary"), big=True),
        name="delta_prep",
    )(q, k, v, gates, gates_t)


def _delta_scan_kernel(*refs):
    fwd = refs[0:6]
    bwd = refs[6:12]
    s0_ref = refs[12]
    of_ref, ob_ref, send_ref, s_scr = refs[13:17]
    t = pl.program_id(1)
    cs = DELTA_CHUNK
    hd = HEAD_DIM
    ta = fwd[0].shape[1]
    nc = ta // cs
    width = fwd[0].shape[2]
    heads = width // hd
    rr = lax.broadcasted_iota(jnp.int32, (width, width), 0)
    cc = lax.broadcasted_iota(jnp.int32, (width, width), 1)
    same_head = lax.shift_right_logical(rr, 6) == lax.shift_right_logical(cc, 6)
    lane_head = lax.shift_right_logical(lax.broadcasted_iota(jnp.int32, (cs, width), 1), 6)

    @pl.when(t == 0)
    def _():
        s_scr[...] = s0_ref[0]

    dirs = ((fwd, of_ref), (bwd, ob_ref))
    for step in range(nc):
        chunk = (step, nc - 1 - step)
        rows = [slice(chunk[d] * cs, (chunk[d] + 1) * cs) for d in range(2)]
        s_prev = [s_scr[d] for d in range(2)]
        lhs = [jnp.concatenate([dirs[d][0][1][0, rows[d], :], dirs[d][0][2][0, rows[d], :]], axis=0)
               for d in range(2)]
        res = [_dot(lhs[d], s_prev[d].astype(BF16)) for d in range(2)]
        v_new = [(dirs[d][0][0][0, rows[d], :] - res[d][0:cs]).astype(BF16) for d in range(2)]
        v_bd = [jnp.concatenate([jnp.where(lane_head == h, v_new[d], jnp.zeros_like(v_new[d]))
                                 for h in range(heads)], axis=0) for d in range(2)]
        zeros = jnp.zeros((cs, width), BF16)
        v_tile = [jnp.concatenate([v_new[d] if ci == chunk[d] else zeros for ci in range(nc)], axis=0)
                  for d in range(2)]
        o = [res[d][cs:2 * cs] + _dot(dirs[d][0][3][0, rows[d], :], v_bd[d]) for d in range(2)]
        upd = [_dot(dirs[d][0][4][0, 0], v_tile[d]) for d in range(2)]
        for d in range(2):
            gl = dirs[d][0][5][0, 0, chunk[d]:chunk[d] + 1, :]
            s_scr[d] = s_prev[d] * gl + jnp.where(same_head, upd[d], 0.0)
            dirs[d][1][0, rows[d], :] = o[d]

    @pl.when(t == pl.num_programs(1) - 1)
    def _():
        send_ref[0] = s_scr[...]


def _delta_scan(prep, s0_bd):
    u_f = prep[0]
    bsz, seq, width = u_f.shape
    ta = min(DELTA_TILE, seq)
    nt = seq // ta

    def specs(index3, index4):
        return [pl.BlockSpec((1, ta, width), index3)] * 4 + [
            pl.BlockSpec((1, 1, width, ta), index4),
            pl.BlockSpec((1, 1, SUBLANES, width), index4)]

    fwd3 = lambda b, t: (b, t, 0)
    fwd4 = lambda b, t: (b, t, 0, 0)
    bwd3 = lambda b, t: (b, nt - 1 - t, 0)
    bwd4 = lambda b, t: (b, nt - 1 - t, 0, 0)
    state_spec = pl.BlockSpec((1, 2, width, width), lambda b, t: (b, 0, 0, 0))
    return pl.pallas_call(
        _delta_scan_kernel,
        out_shape=[jax.ShapeDtypeStruct((bsz, seq, width), F32),
                   jax.ShapeDtypeStruct((bsz, seq, width), F32),
                   jax.ShapeDtypeStruct((bsz, 2, width, width), F32)],
        grid=(bsz, nt),
        in_specs=specs(fwd3, fwd4) + specs(bwd3, bwd4) + [state_spec],
        out_specs=[pl.BlockSpec((1, ta, width), fwd3), pl.BlockSpec((1, ta, width), bwd3),
                   state_spec],
        scratch_shapes=[pltpu.VMEM((2, width, width), F32)],
        compiler_params=_cparams(("arbitrary", "arbitrary")),
        name="delta_scan",
    )(*prep, s0_bd)


def _delta_out_kernel(of_ref, ob_ref, z_ref, ones_ref, g_ref, o_ref):
    ones_bd = ones_ref[...]
    for s in range(of_ref.shape[2] // LANES):
        lanes = slice(s * LANES, (s + 1) * LANES)
        o = of_ref[0, :, lanes] + ob_ref[0, :, lanes]
        ms = _group_sumsq(o, ones_bd) * (1.0 / HEAD_DIM)
        y = o * lax.rsqrt(ms + EPS) * g_ref[...]
        o_ref[0, :, lanes] = (y * _silu(z_ref[0, :, lanes])).astype(BF16)


def _delta_out(o_f, o_b, pc, ones_bd, norm_g):
    bsz, seq, width = o_f.shape
    tl = min(512, seq)
    main = lambda b, i: (b, i, 0)
    const2 = lambda b, i: (0, 0)
    z_block = pc.shape[2] // width - 1
    return pl.pallas_call(
        _delta_out_kernel,
        out_shape=jax.ShapeDtypeStruct((bsz, seq, width), BF16),
        grid=(bsz, seq // tl),
        in_specs=[pl.BlockSpec((1, tl, width), main), pl.BlockSpec((1, tl, width), main),
                  pl.BlockSpec((1, tl, width), lambda b, i: (b, i, z_block)),
                  pl.BlockSpec((LANES, LANES), const2),
                  pl.BlockSpec((1, LANES), const2)],
        out_specs=pl.BlockSpec((1, tl, width), main),
        compiler_params=_cparams(("arbitrary", "arbitrary")),
        name="delta_out",
    )(o_f, o_b, pc, ones_bd, jnp.tile(norm_g, LANES // HEAD_DIM).reshape(1, LANES))


def _deltanet(q, k, v, pc, gates, gates_t, s0, norm_g, ones_bd):
    bsz = q.shape[0]
    heads, hd = N_HEADS_C, HEAD_DIM
    eye_h = jnp.eye(heads, dtype=F32)
    s0_bd = jnp.einsum('bdhij,hg->bdhigj', s0.astype(F32), eye_h).reshape(
        bsz, 2, heads * hd, heads * hd)
    prep = _delta_prep(q, k, v, gates, gates_t)
    o_f, o_b, s_bd = _delta_scan(prep, s0_bd)
    o_c = _delta_out(o_f, o_b, pc, ones_bd, norm_g)
    s_blocks = s_bd.reshape(bsz, 2, heads, hd, heads, hd)
    s_end = jnp.stack([s_blocks[:, :, h, :, h, :] for h in range(heads)], axis=2)
    return o_c, s_end


def _fourier_ch_kernel(x_ref, c_ref, s_ref, w_ref):
    x = x_ref[0].astype(BF16)
    w_ref[0] = _dot(x, c_ref[...].astype(BF16)).astype(BF16)
    w_ref[1] = _dot(x, s_ref[...].astype(BF16)).astype(BF16)


def _fourier_channels(xd, c_bd, s_bd):
    bsz, seq, gw = xd.shape
    tl = min(512, seq)
    const2 = lambda b, i: (0, 0)
    return pl.pallas_call(
        _fourier_ch_kernel,
        out_shape=jax.ShapeDtypeStruct((2, seq, bsz * gw), BF16),
        grid=(bsz, seq // tl),
        in_specs=[pl.BlockSpec((1, tl, gw), lambda b, i: (b, i, 0)),
                  pl.BlockSpec((gw, gw), const2),
                  pl.BlockSpec((gw, gw), const2)],
        out_specs=pl.BlockSpec((2, tl, gw), lambda b, i: (0, i, b)),
        compiler_params=_cparams(("arbitrary", "arbitrary")),
        name="fourier_channels",
    )(xd, c_bd, s_bd)


def _fourier_pos_kernel(w_ref, a1_ref, b1_ref, a0_ref, b0_ref, o_ref, c_scr, s_scr, *, scale):
    i = pl.program_id(1)
    tm = c_scr.shape[0]
    sub = a0_ref.shape[0]
    a0 = a0_ref[...]
    b0 = b0_ref[...]
    for rblk in range(tm // sub):
        j1 = i * (tm // sub) + rblk
        a1 = a1_ref[pl.ds(j1, 1), :]
        b1 = b1_ref[pl.ds(j1, 1), :]
        c_scr[rblk * sub:(rblk + 1) * sub, :] = (a1 * a0 - b1 * b0).astype(BF16)
        s_scr[rblk * sub:(rblk + 1) * sub, :] = (b1 * a0 + a1 * b0).astype(BF16)
    res = (_dot(c_scr[...], w_ref[0]) - _dot(s_scr[...], w_ref[1])) * scale
    gw = o_ref.shape[2]
    for b in range(o_ref.shape[0]):
        o_ref[b] = res[:, b * gw:(b + 1) * gw].astype(BF16)


def _fourier_positions(w, tabs, bsz, gw):
    _, seq, ncol = w.shape
    a1, b1, a0, b0 = tabs
    tm = min(512, seq)
    nsplit = 2 if (bsz % 2 == 0 and seq > 1024) else 1
    bpart = bsz // nsplit
    const2 = lambda p, i: (0, 0)
    single = pl.Buffered(1)
    scale = 1.0 / np.sqrt(float(seq) * HEAD_DIM)
    return pl.pallas_call(
        functools.partial(_fourier_pos_kernel, scale=scale),
        out_shape=jax.ShapeDtypeStruct((bsz, seq, gw), BF16),
        grid=(nsplit, seq // tm),
        in_specs=[pl.BlockSpec((2, seq, bpart * gw), lambda p, i: (0, 0, p), pipeline_mode=single),
                  pl.BlockSpec(a1.shape, const2, pipeline_mode=single),
                  pl.BlockSpec(b1.shape, const2, pipeline_mode=single),
                  pl.BlockSpec(a0.shape, const2, pipeline_mode=single),
                  pl.BlockSpec(b0.shape, const2, pipeline_mode=single)],
        out_specs=pl.BlockSpec((bpart, tm, gw), lambda p, i: (p, i, 0)),
        scratch_shapes=[pltpu.VMEM((tm, seq), BF16), pltpu.VMEM((tm, seq), BF16)],
        compiler_params=_cparams(("arbitrary", "arbitrary"), big=True),
        name="fourier_positions",
    )(w, a1, b1, a0, b0)


def _rope_tables(seq):
    rows = seq // GRID_W
    r, cl = jnp.meshgrid(jnp.arange(rows), jnp.arange(GRID_W), indexing='ij')
    pos_r = r.reshape(-1).astype(F32)
    pos_c = cl.reshape(-1).astype(F32)
    quarter = HEAD_DIM // 4
    inv_freq = ROPE_THETA ** (-jnp.arange(quarter, dtype=F32) / quarter)
    cos_parts, sin_parts = [], []
    for pos in (pos_r, pos_c):
        ang = pos[:, None] * inv_freq[None, :]
        cos_parts += [jnp.cos(ang), jnp.cos(ang)]
        sin_parts += [-jnp.sin(ang), jnp.sin(ang)]
    cos = jnp.concatenate(cos_parts, axis=-1)
    sin = jnp.concatenate(sin_parts, axis=-1)
    return jnp.tile(cos, (1, 2)), jnp.tile(sin, (1, 2))


def _trig_table(num_rows, row_mult, seq):
    j = np.arange(num_rows, dtype=np.int64)[:, None] * row_mult
    k = np.arange(seq, dtype=np.int64)[None, :]
    ang = 2.0 * np.pi * ((j * k) % seq).astype(np.float64) / seq
    return jnp.asarray(np.cos(ang), F32), jnp.asarray(np.sin(ang), F32)


def _fourier_tables(seq):
    sub = 64
    a1, b1 = _trig_table(seq // sub, sub, seq)
    a0, b0 = _trig_table(sub, 1, seq)
    idx = np.arange(GROUP_WIDTH)
    same = (idx[:, None] // HEAD_DIM) == (idx[None, :] // HEAD_DIM)
    ang = 2.0 * np.pi * ((idx[:, None] % HEAD_DIM) * (idx[None, :] % HEAD_DIM) % HEAD_DIM) / HEAD_DIM
    c_bd = jnp.asarray(np.where(same, np.cos(ang), 0.0), F32)
    s_bd = jnp.asarray(np.where(same, np.sin(ang), 0.0), F32)
    return (a1, b1, a0, b0), c_bd, s_bd


def _to_lane_dense(t):
    b, hh, l, d = t.shape
    return jnp.swapaxes(t, 1, 2).reshape(b, l, hh * d)


def _to_head_major(t, heads):
    b, l, w = t.shape
    return jnp.swapaxes(t.reshape(b, l, heads, w // heads), 1, 2)


def _swap_halves(t):
    return jnp.concatenate([t[..., HEAD_DIM:], t[..., :HEAD_DIM]], axis=-1)


def _trunk_layer(x, mod, wts, consts, ctx, final_g):
    bsz, seq, _ = x.shape
    x1 = _ffn(x, mod, wts['norm_ffn1'], wts['ffn1_w1'], wts['ffn1_w3'], wts['ffn1_w2'], which=0)
    pa, pb, pc, pd, ps = _inproj(x1, mod, wts['norm_mix'], wts['w_in'])

    rope_tabs = consts['rope'] if ctx is not None else None
    prep = _attn_prep(pa, consts['ones_bd'], wts['gq'], wts['gk'], rope_tabs, ctx is None)
    q, ka, kb, va, vb = prep[:5]
    k_cache = v_cache = None
    if ctx is None:
        k_cache = _to_head_major(prep[5], KV_HEADS_A)
        v_cache = _to_head_major(pa[..., 384:512], KV_HEADS_A)
        s0 = jnp.zeros((bsz, 2, N_HEADS_C, HEAD_DIM, HEAD_DIM), F32)
    else:
        k_ctx, v_ctx, s0 = ctx
        k_past = _to_lane_dense(k_ctx).astype(BF16)
        v_past = _to_lane_dense(v_ctx).astype(BF16)
        ka = jnp.concatenate([k_past, ka], axis=1)
        kb = jnp.concatenate([_swap_halves(k_past), kb], axis=1)
        ones = jnp.ones_like(v_past)
        va = jnp.concatenate([jnp.concatenate([v_past, ones], axis=-1), va], axis=1)
        vb = jnp.concatenate([jnp.concatenate([_swap_halves(v_past), ones], axis=-1), vb], axis=1)
    o_a = _attention(q, ka, kb, va, vb)

    o_b, qc, kc, vc, gates, gates_t = _conv_prep(pb, pc, ps, wts['conv_b_w'], wts['conv_c_w'],
                                                 consts['ones_bd'], wts['alog_row'], wts['dtb_row'])
    o_c, s_end = _deltanet(qc, kc, vc, pc, gates, gates_t, s0, wts['delta_norm'], consts['ones_bd'])

    w = _fourier_channels(pd, consts['c_bd'], consts['s_bd'])
    o_d = _fourier_positions(w, consts['fourier'], bsz, GROUP_WIDTH)

    x3 = _ffn(x1, mod, wts['norm_ffn2'], wts['ffn2_w1'], wts['ffn2_w3'], wts['ffn2_w2'], which=2,
              mix=(o_a, o_b, o_c, o_d), w_out=wts['w_out'], final_g=final_g)
    return x3, k_cache, v_cache, s_end


def _lane_row(values, start):
    row = jnp.zeros((1, LANES), F32)
    return row.at[0, start:start + values.shape[0]].set(values.astype(F32))


def kernel(x_prompt, x_sample, c, cache_k, cache_v, state_delta, c_ctx, mod_w, mod_b, norm_ffn1, norm_mix, norm_ffn2, ffn1_w1, ffn1_w3, ffn1_w2, ffn2_w1, ffn2_w3, ffn2_w2, w_in, w_out, q_norm, k_norm, conv_b_w, conv_c_w, delta_a_log, delta_dt_bias, delta_norm, final_norm):
    depth = mod_w.shape[0]
    d_model = x_prompt.shape[-1]
    dec_b = x_sample.shape[0]
    assert w_in.shape[-1] == PROJ_MAIN + PROJ_SMALL + GROUP_WIDTH

    n_cond = 1 + dec_b
    rows = -(-n_cond // SUBLANES) * SUBLANES
    cond = jnp.zeros((rows, d_model), F32).at[0].set(c_ctx).at[1:n_cond].set(c)

    idx = np.arange(LANES)
    ones_bd = jnp.asarray((idx[:, None] // HEAD_DIM) == (idx[None, :] // HEAD_DIM), BF16)
    consts_p = {'ones_bd': ones_bd}
    consts_s = {'ones_bd': ones_bd, 'rope': _rope_tables(x_sample.shape[1])}
    consts_p['fourier'], consts_p['c_bd'], consts_p['s_bd'] = _fourier_tables(x_prompt.shape[1])
    consts_s['fourier'], consts_s['c_bd'], consts_s['s_bd'] = _fourier_tables(x_sample.shape[1])

    yp, ys = x_prompt, x_sample
    k_list, v_list, s_list = [], [], []
    for l in range(depth):
        w_in_l = w_in[l]
        small = w_in_l[:, PROJ_MAIN:PROJ_MAIN + PROJ_SMALL]
        w_in_r = jnp.concatenate(
            [w_in_l[:, :PROJ_MAIN], w_in_l[:, PROJ_MAIN + PROJ_SMALL:], small,
             jnp.zeros((d_model, LANES - PROJ_SMALL), F32)], axis=1).astype(BF16)
        wts = {
            'norm_ffn1': norm_ffn1[l], 'norm_mix': norm_mix[l], 'norm_ffn2': norm_ffn2[l],
            'ffn1_w1': ffn1_w1[l].astype(BF16), 'ffn1_w3': ffn1_w3[l].astype(BF16),
            'ffn1_w2': ffn1_w2[l].astype(BF16),
            'ffn2_w1': ffn2_w1[l].astype(BF16), 'ffn2_w3': ffn2_w3[l].astype(BF16),
            'ffn2_w2': ffn2_w2[l].astype(BF16),
            'w_in': w_in_r, 'w_out': w_out[l].astype(BF16),
            'gq': jnp.tile(q_norm[l], 2).reshape(1, LANES),
            'gk': jnp.tile(k_norm[l], 2).reshape(1, LANES),
            'conv_b_w': conv_b_w[l], 'conv_c_w': conv_c_w[l],
            'alog_row': _lane_row(delta_a_log[l].reshape(-1), 8),
            'dtb_row': _lane_row(delta_dt_bias[l].reshape(-1), 8),
            'delta_norm': delta_norm[l],
        }
        mod = _modulation(cond, mod_w[l], mod_b[l]).reshape(rows, N_MOD, d_model)
        fin = final_norm if l == depth - 1 else None
        yp, k_l, v_l, s_l = _trunk_layer(yp, mod[0:1], wts, consts_p, None, fin)
        k_list.append(k_l)
        v_list.append(v_l)
        s_list.append(s_l)
        ys, _, _, _ = _trunk_layer(ys, mod[1:n_cond], wts, consts_s,
                                   (cache_k[:, l], cache_v[:, l], state_delta[:, l]), fin)
    return (yp, ys, jnp.stack(k_list, axis=1), jnp.stack(v_list, axis=1),
            jnp.stack(s_list, axis=1))
```

```python
import functools

import jax
import jax.numpy as jnp
import numpy as np
from jax import lax
from jax.experimental import pallas as pl
from jax.experimental.pallas import tpu as pltpu

F32 = jnp.float32
BF16 = jnp.bfloat16

HEAD_DIM = 64
N_HEADS_A = 4
KV_HEADS_A = 2
N_HEADS_C = 4
GROUP_WIDTH = 256
GRID_W = 64
DELTA_CHUNK = 64
ROPE_THETA = 10000.0
N_MOD = 9
EPS = 1e-6
PROJ_MAIN = 2304
PROJ_SMALL = 16
LANES = 128
SUBLANES = 8
VMEM_LIMIT_BYTES = 56 * 1024 * 1024


def _cparams(sem, big=False):
    return pltpu.CompilerParams(
        dimension_semantics=sem,
        vmem_limit_bytes=VMEM_LIMIT_BYTES if big else None)


def _dot(a, b):
    return jnp.dot(a, b, preferred_element_type=F32)


def _dot_nt(a, b):
    return lax.dot_general(a, b, (((1,), (1,)), ((), ())), preferred_element_type=F32)


def _split2(x):
    hi = x.astype(BF16)
    lo = (x - hi.astype(F32)).astype(BF16)
    return hi, lo


def _split3(x):
    hi = x.astype(BF16)
    r = x - hi.astype(F32)
    mid = r.astype(BF16)
    lo = (r - mid.astype(F32)).astype(BF16)
    return hi, mid, lo


def _rms(x, g):
    return x * lax.rsqrt(jnp.mean(x * x, axis=-1, keepdims=True) + EPS) * g


def _silu(x):
    return x * jax.nn.sigmoid(x)


def _group_sumsq(x, ones_bd):
    hi, lo = _split2(x * x)
    return _dot(hi, ones_bd) + _dot(lo, ones_bd)


def _mod_kernel(c_ref, w_ref, b_ref, o_ref):
    s = _silu(c_ref[...]).astype(BF16)
    o_ref[...] = _dot(s, w_ref[...].astype(BF16)) + b_ref[...]


def _modulation(cond, mod_w, mod_b):
    rows, d = cond.shape
    n = mod_w.shape[1]
    tn = d
    return pl.pallas_call(
        _mod_kernel,
        out_shape=jax.ShapeDtypeStruct((rows, n), F32),
        grid=(n // tn,),
        in_specs=[pl.BlockSpec((rows, d), lambda j: (0, 0)),
                  pl.BlockSpec((d, tn), lambda j: (0, j)),
                  pl.BlockSpec((1, tn), lambda j: (0, j))],
        out_specs=pl.BlockSpec((rows, tn), lambda j: (0, j)),
        compiler_params=_cparams(("arbitrary",)),
        name="modulation",
    )(cond, mod_w, mod_b.reshape(1, n))


def _ffn_chunks(f):
    step = 1024
    return [(s, min(s + step, f)) for s in range(0, f, step)]


def _ffn_kernel(*refs, which, n_mix, final):
    x_ref, mod_ref, g_ref, w1_ref, w3_ref, w2_ref = refs[:6]
    pos = 6
    mix_refs = refs[pos:pos + n_mix]
    pos += n_mix
    wout_ref = None
    if n_mix:
        wout_ref = refs[pos]
        pos += 1
    gf_ref = None
    if final:
        gf_ref = refs[pos]
        pos += 1
    o_ref = refs[pos]

    x = x_ref[0]
    mod = mod_ref[0]
    if n_mix:
        acc = None
        for i, m_ref in enumerate(mix_refs):
            w = m_ref.shape[-1]
            part = _dot(m_ref[0], wout_ref[i * w:(i + 1) * w, :])
            acc = part if acc is None else acc + part
        x = x + mod[5:6] * acc
    sh = mod[3 * which:3 * which + 1]
    sc = mod[3 * which + 1:3 * which + 2]
    gt = mod[3 * which + 2:3 * which + 3]
    h = (_rms(x, g_ref[...]) * (1.0 + sc) + sh).astype(BF16)
    out = None
    for s, e in _ffn_chunks(w1_ref.shape[1]):
        a = _dot(h, w1_ref[:, s:e])
        b = _dot(h, w3_ref[:, s:e])
        act = (_silu(a) * b).astype(BF16)
        part = _dot(act, w2_ref[s:e, :])
        out = part if out is None else out + part
    xn = x + 0.5 * gt * out
    if final:
        xn = _rms(xn, gf_ref[...])
    o_ref[0] = xn


def _ffn(x, mod, g, w1, w3, w2, *, which, mix=None, w_out=None, final_g=None):
    bsz, seq, d = x.shape
    f = w1.shape[1]
    tm = min(512, seq)
    per_batch_mod = mod.shape[0] > 1
    mod_map = (lambda b, i: (b, 0, 0)) if per_batch_mod else (lambda b, i: (0, 0, 0))
    const2 = lambda b, i: (0, 0)
    single = pl.Buffered(1)
    args = [x, mod, g.reshape(1, d), w1, w3, w2]
    specs = [pl.BlockSpec((1, tm, d), lambda b, i: (b, i, 0)),
             pl.BlockSpec((1, N_MOD, d), mod_map),
             pl.BlockSpec((1, d), const2),
             pl.BlockSpec((d, f), const2, pipeline_mode=single),
             pl.BlockSpec((d, f), const2, pipeline_mode=single),
             pl.BlockSpec((f, d), const2, pipeline_mode=single)]
    n_mix = 0
    if mix is not None:
        n_mix = len(mix)
        for m in mix:
            args.append(m)
            specs.append(pl.BlockSpec((1, tm, m.shape[-1]), lambda b, i: (b, i, 0)))
        args.append(w_out)
        specs.append(pl.BlockSpec(w_out.shape, const2, pipeline_mode=single))
    if final_g is not None:
        args.append(final_g.reshape(1, d))
        specs.append(pl.BlockSpec((1, d), const2))
    return pl.pallas_call(
        functools.partial(_ffn_kernel, which=which, n_mix=n_mix, final=final_g is not None),
        out_shape=jax.ShapeDtypeStruct((bsz, seq, d), F32),
        grid=(bsz, seq // tm),
        in_specs=specs,
        out_specs=pl.BlockSpec((1, tm, d), lambda b, i: (b, i, 0)),
        compiler_params=_cparams(("arbitrary", "arbitrary"), big=True),
        name="ffn",
    )(*args)


_PROJ_SPLITS = ((0, 512), (512, 1280), (1280, 2304), (2304, 2560), (2560, 2688))
_PROJ_DTYPES = (F32, F32, F32, BF16, F32)


def _inproj_kernel(x_ref, mod_ref, g_ref, w_ref, *o_refs):
    mod = mod_ref[0]
    h = (_rms(x_ref[0], g_ref[...]) * (1.0 + mod[4:5]) + mod[3:4]).astype(BF16)
    for (s, e), o_ref in zip(_PROJ_SPLITS, o_refs):
        o_ref[0] = _dot(h, w_ref[:, s:e]).astype(o_ref.dtype)


def _inproj(x, mod, g, w_in_r):
    bsz, seq, d = x.shape
    tm = min(512, seq)
    per_batch_mod = mod.shape[0] > 1
    mod_map = (lambda b, i: (b, 0, 0)) if per_batch_mod else (lambda b, i: (0, 0, 0))
    const2 = lambda b, i: (0, 0)
    widths = [e - s for s, e in _PROJ_SPLITS]
    return pl.pallas_call(
        _inproj_kernel,
        out_shape=[jax.ShapeDtypeStruct((bsz, seq, w), dt) for w, dt in zip(widths, _PROJ_DTYPES)],
        grid=(bsz, seq // tm),
        in_specs=[pl.BlockSpec((1, tm, d), lambda b, i: (b, i, 0)),
                  pl.BlockSpec((1, N_MOD, d), mod_map),
                  pl.BlockSpec((1, d), const2),
                  pl.BlockSpec(w_in_r.shape, const2, pipeline_mode=pl.Buffered(1))],
        out_specs=[pl.BlockSpec((1, tm, w), lambda b, i: (b, i, 0)) for w in widths],
        compiler_params=_cparams(("arbitrary", "arbitrary"), big=True),
        name="inproj",
    )(x, mod, g.reshape(1, d), w_in_r)


def _rope(x, cos, sin_signed):
    lane = lax.broadcasted_iota(jnp.int32, x.shape, 1)
    up = pltpu.roll(x, LANES - 16, axis=1)
    down = pltpu.roll(x, 16, axis=1)
    partner = jnp.where((lane % 32) < 16, up, down)
    return x * cos + partner * sin_signed


def _attn_prep_kernel(*refs, rope, want_cache):
    pa_ref, ones_ref, gq_ref, gk_ref = refs[:4]
    pos = 4
    if rope:
        cos_ref, sin_ref = refs[pos:pos + 2]
        pos += 2
    q_ref, ka_ref, kb_ref, va_ref, vb_ref = refs[pos:pos + 5]
    pos += 5
    pa = pa_ref[0]
    ones_bd = ones_ref[...]

    def normed(x, g):
        ss = _group_sumsq(x, ones_bd)
        return x * lax.rsqrt(ss * (1.0 / HEAD_DIM) + EPS) * g

    q0 = normed(pa[:, 0:128], gq_ref[...])
    q1 = normed(pa[:, 128:256], gq_ref[...])
    k = normed(pa[:, 256:384], gk_ref[...])
    v = pa[:, 384:512]
    if want_cache:
        refs[pos][0] = k
    if rope:
        cos = cos_ref[...]
        sin = sin_ref[...]
        q0 = _rope(q0, cos, sin)
        q1 = _rope(q1, cos, sin)
        k = _rope(k, cos, sin)
    scale = HEAD_DIM ** -0.5
    q_ref[0, :, 0:128] = (q0 * scale).astype(BF16)
    q_ref[0, :, 128:256] = (q1 * scale).astype(BF16)
    ka_ref[0] = k.astype(BF16)
    kb_ref[0] = pltpu.roll(k, HEAD_DIM, axis=1).astype(BF16)
    ones = jnp.ones(v.shape, BF16)
    va_ref[0, :, 0:LANES] = v.astype(BF16)
    va_ref[0, :, LANES:2 * LANES] = ones
    vb_ref[0, :, 0:LANES] = pltpu.roll(v, HEAD_DIM, axis=1).astype(BF16)
    vb_ref[0, :, LANES:2 * LANES] = ones


def _attn_prep(pa, ones_bd, gq, gk, rope_tabs, want_cache):
    bsz, seq, _ = pa.shape
    tl = min(512, seq)
    rope = rope_tabs is not None
    const2 = lambda b, i: (0, 0)
    args = [pa, ones_bd, gq, gk]
    specs = [pl.BlockSpec((1, tl, 512), lambda b, i: (b, i, 0)),
             pl.BlockSpec((LANES, LANES), const2),
             pl.BlockSpec((1, LANES), const2),
             pl.BlockSpec((1, LANES), const2)]
    if rope:
        args += list(rope_tabs)
        specs += [pl.BlockSpec((tl, LANES), lambda b, i: (i, 0))] * 2
    outs = [jax.ShapeDtypeStruct((bsz, seq, 256), BF16)]
    outs += [jax.ShapeDtypeStruct((bsz, seq, LANES), BF16)] * 2
    outs += [jax.ShapeDtypeStruct((bsz, seq, 2 * LANES), BF16)] * 2
    ospecs = [pl.BlockSpec((1, tl, 256), lambda b, i: (b, i, 0))]
    ospecs += [pl.BlockSpec((1, tl, LANES), lambda b, i: (b, i, 0))] * 2
    ospecs += [pl.BlockSpec((1, tl, 2 * LANES), lambda b, i: (b, i, 0))] * 2
    if want_cache:
        outs.append(jax.ShapeDtypeStruct((bsz, seq, LANES), F32))
        ospecs.append(pl.BlockSpec((1, tl, LANES), lambda b, i: (b, i, 0)))
    return pl.pallas_call(
        functools.partial(_attn_prep_kernel, rope=rope, want_cache=want_cache),
        out_shape=outs,
        grid=(bsz, seq // tl),
        in_specs=specs,
        out_specs=ospecs,
        compiler_params=_cparams(("arbitrary", "arbitrary")),
        name="attn_prep",
    )(*args)


def _attn_kernel(q_ref, ka_ref, kb_ref, va_ref, vb_ref, o_ref):
    tq = q_ref.shape[1]
    lane = lax.broadcasted_iota(jnp.int32, (tq, LANES), 1)
    low = lane < HEAD_DIM
    arrangement = ((ka_ref, va_ref), (kb_ref, vb_ref), (kb_ref, vb_ref), (ka_ref, va_ref))
    def scores(h):
        slab = q_ref[0, :, (h // 2) * LANES:(h // 2 + 1) * LANES]
        keep = low if h % 2 == 0 else jnp.logical_not(low)
        qh = jnp.where(keep, slab, jnp.zeros_like(slab))
        return _dot_nt(qh, arrangement[h][0][0])

    def weighted_values(h, s):
        m = jnp.max(s, axis=-1, keepdims=True)
        p = jnp.exp(s - m)
        pv = _dot(p.astype(BF16), arrangement[h][1][0])
        return pv[:, 0:LANES] / pv[:, LANES:LANES + 1]

    outs = []
    s_next = scores(0)
    for h in range(N_HEADS_A):
        s_cur = s_next
        if h + 1 < N_HEADS_A:
            s_next = scores(h + 1)
        outs.append(weighted_values(h, s_cur))
    o_ref[0, :, 0:128] = jnp.where(low, outs[0], outs[1]).astype(BF16)
    o_ref[0, :, 128:256] = jnp.where(low, outs[2], outs[3]).astype(BF16)


def _attention(q, ka, kb, va, vb):
    bsz, seq, _ = q.shape
    lk = ka.shape[1]
    tq = min(512, seq)
    k_spec = pl.BlockSpec((1, lk, LANES), lambda b, i: (b, 0, 0))
    v_spec = pl.BlockSpec((1, lk, 2 * LANES), lambda b, i: (b, 0, 0))
    return pl.pallas_call(
        _attn_kernel,
        out_shape=jax.ShapeDtypeStruct((bsz, seq, 256), BF16),
        grid=(bsz, seq // tq),
        in_specs=[pl.BlockSpec((1, tq, 256), lambda b, i: (b, i, 0)),
                  k_spec, k_spec, v_spec, v_spec],
        out_specs=pl.BlockSpec((1, tq, 256), lambda b, i: (b, i, 0)),
        compiler_params=_cparams(("arbitrary", "arbitrary"), big=True),
        name="attention",
    )(q, ka, kb, va, vb)


def _conv3(x, prev_row, next_row, w):
    rows = x.shape[0]
    ridx = lax.broadcasted_iota(jnp.int32, x.shape, 0)
    xm = jnp.where(ridx == 0, prev_row, pltpu.roll(x, 1, axis=0))
    xp = jnp.where(ridx == rows - 1, next_row, pltpu.roll(x, rows - 1, axis=0))
    return xm * w[0:1] + x * w[1:2] + xp * w[2:3]


def _conv_prep_kernel(pb_ref, pbp_ref, pbn_ref, pc_ref, pcp_ref, pcn_ref, ps_ref,
                      wb_ref, wc_ref, ones_ref, alog_ref, dtb_ref, tril_ref, triu_ref,
                      ob_ref, q_ref, k_ref, v_ref, gates_ref, gatest_ref):
    i = pl.program_id(1)
    has_prev = i > 0
    has_next = i < pl.num_programs(1) - 1
    gw = GROUP_WIDTH

    pb = pb_ref[0]
    u = pb[:, gw:2 * gw] * pb[:, 2 * gw:3 * gw]
    pbp = pbp_ref[0, SUBLANES - 1:SUBLANES, :]
    pbn = pbn_ref[0, 0:1, :]
    u_prev = jnp.where(has_prev, pbp[:, gw:2 * gw] * pbp[:, 2 * gw:3 * gw], 0.0)
    u_next = jnp.where(has_next, pbn[:, gw:2 * gw] * pbn[:, 2 * gw:3 * gw], 0.0)
    ob_ref[0] = (pb[:, 0:gw] * _conv3(u, u_prev, u_next, wb_ref[...])).astype(BF16)

    pc = pc_ref[0]
    x3 = pc[:, 0:3 * gw]
    x_prev = jnp.where(has_prev, pcp_ref[0, SUBLANES - 1:SUBLANES, 0:3 * gw], 0.0)
    x_next = jnp.where(has_next, pcn_ref[0, 0:1, 0:3 * gw], 0.0)
    qkv = _silu(_conv3(x3, x_prev, x_next, wc_ref[...]))
    ones_bd = ones_ref[...]

    def l2n(x):
        return x * lax.rsqrt(_group_sumsq(x, ones_bd) + EPS)

    q_scale = HEAD_DIM ** -0.5
    for s in range(2):
        q_ref[0, :, s * LANES:(s + 1) * LANES] = l2n(qkv[:, s * LANES:(s + 1) * LANES]) * q_scale
        k_ref[0, :, s * LANES:(s + 1) * LANES] = l2n(qkv[:, gw + s * LANES:gw + (s + 1) * LANES])
    v_ref[0] = qkv[:, 2 * gw:3 * gw]

    raw = ps_ref[0]
    beta = jax.nn.sigmoid(raw)
    y = raw + dtb_ref[...]
    softplus = jnp.maximum(y, 0.0) + jnp.log(1.0 + jnp.exp(-jnp.abs(y)))
    g = -jnp.exp(alog_ref[...]) * softplus
    g_hi, g_mid, g_lo = _split3(g)
    tril = tril_ref[...]
    triu = triu_ref[...]
    gc_f = _dot(tril, g_hi) + _dot(tril, g_mid) + _dot(tril, g_lo)
    gc_b = _dot(triu, g_hi) + _dot(triu, g_mid) + _dot(triu, g_lo)
    lane = lax.broadcasted_iota(jnp.int32, raw.shape, 1)
    gates = jnp.where(lane < 8, beta, jnp.where(lane < 12, gc_f, gc_b))
    gates_ref[0] = gates
    gatest_ref[0] = jnp.transpose(gates)[0:2 * SUBLANES, :]


def _conv_prep(pb, pc, ps, conv_b_w, conv_c_w, ones_bd, alog_row, dtb_row):
    bsz, seq, _ = pb.shape
    tl = min(512, seq)
    nblk8 = seq // SUBLANES
    r8 = tl // SUBLANES
    main = lambda b, i: (b, i, 0)
    prev = lambda b, i: (b, jnp.maximum(i * r8 - 1, 0), 0)
    nxt = lambda b, i: (b, jnp.minimum((i + 1) * r8, nblk8 - 1), 0)
    const2 = lambda b, i: (0, 0)
    idx = np.arange(tl)
    same_chunk = (idx[:, None] // DELTA_CHUNK) == (idx[None, :] // DELTA_CHUNK)
    tril = jnp.asarray(same_chunk & (idx[:, None] >= idx[None, :]), BF16)
    triu = jnp.asarray(same_chunk & (idx[:, None] <= idx[None, :]), BF16)
    dense = jax.ShapeDtypeStruct((bsz, seq, GROUP_WIDTH), F32)
    dense_spec = pl.BlockSpec((1, tl, GROUP_WIDTH), main)
    return pl.pallas_call(
        _conv_prep_kernel,
        out_shape=[jax.ShapeDtypeStruct((bsz, seq, GROUP_WIDTH), BF16), dense, dense, dense,
                   jax.ShapeDtypeStruct((bsz, seq, LANES), F32),
                   jax.ShapeDtypeStruct((bsz, 2 * SUBLANES, seq), F32)],
        grid=(bsz, seq // tl),
        in_specs=[pl.BlockSpec((1, tl, 768), main),
                  pl.BlockSpec((1, SUBLANES, 768), prev),
                  pl.BlockSpec((1, SUBLANES, 768), nxt),
                  pl.BlockSpec((1, tl, 1024), main),
                  pl.BlockSpec((1, SUBLANES, 1024), prev),
                  pl.BlockSpec((1, SUBLANES, 1024), nxt),
                  pl.BlockSpec((1, tl, LANES), main),
                  pl.BlockSpec((3, GROUP_WIDTH), const2),
                  pl.BlockSpec((3, 3 * GROUP_WIDTH), const2),
                  pl.BlockSpec((LANES, LANES), const2),
                  pl.BlockSpec((1, LANES), const2),
                  pl.BlockSpec((1, LANES), const2),
                  pl.BlockSpec((tl, tl), const2),
                  pl.BlockSpec((tl, tl), const2)],
        out_specs=[pl.BlockSpec((1, tl, GROUP_WIDTH), main), dense_spec, dense_spec, dense_spec,
                   pl.BlockSpec((1, tl, LANES), main),
                   pl.BlockSpec((1, 2 * SUBLANES, tl), lambda b, i: (b, 0, i))],
        compiler_params=_cparams(("arbitrary", "arbitrary"), big=True),
        name="conv_prep",
    )(pb, pb, pb, pc, pc, pc, ps, conv_b_w, conv_c_w, ones_bd, alog_row, dtb_row, tril, triu)


_TRI_BASE_LOG2 = 3
_TRI_NEWTON_STEPS = 1
DELTA_TILE = 4 * DELTA_CHUNK


def _pair_blockdiag(y):
    low = lax.broadcasted_iota(jnp.int32, y.shape, 1) < HEAD_DIM
    zero = jnp.zeros_like(y)
    return jnp.concatenate([jnp.where(low, y, zero), jnp.where(low, zero, y)], axis=0)


def _pair_dot3_many(a_list, b_list):
    n = a_list[0].shape[0]
    lhs = [jnp.concatenate(_split2(a), axis=0) for a in a_list]
    rhs = [jnp.concatenate([_pair_blockdiag(part) for part in _split2(b)], axis=1) for b in b_list]
    res = [_dot(l, rr) for l, rr in zip(lhs, rhs)]
    return [(x[0:n, 0:LANES] + x[n:2 * n, LANES:2 * LANES])
            + (x[0:n, LANES:2 * LANES] + x[n:2 * n, 0:LANES]) for x in res]


def _pair_dot1_many(a_list, b_list):
    return [_dot(a.astype(BF16), _pair_blockdiag(b.astype(BF16))) for a, b in zip(a_list, b_list)]


def _pair_tri_inverse_many(lmats, r, c):
    def same_block(log2_size):
        return lax.shift_right_logical(r, log2_size) == lax.shift_right_logical(c, log2_size)

    base = same_block(_TRI_BASE_LOG2)
    eye = jnp.where(r == c, 1.0, 0.0)
    a_list = [jnp.where(base, m, 0.0) for m in lmats]
    p_list = [eye - a for a in a_list]
    for _ in range(_TRI_BASE_LOG2 - 1):
        a_list = _pair_dot1_many(a_list, a_list)
        p_list = [p + pa for p, pa in zip(p_list, _pair_dot1_many(p_list, a_list))]
    for log2_size in range(_TRI_BASE_LOG2, int(np.log2(DELTA_CHUNK))):
        couple = jnp.logical_and(same_block(log2_size + 1),
                                 jnp.logical_not(same_block(log2_size)))
        c_list = [jnp.where(couple, m, 0.0) for m in lmats]
        pc_list = _pair_dot1_many(p_list, c_list)
        p_list = [p - pcp for p, pcp in zip(p_list, _pair_dot1_many(pc_list, p_list))]
    for _ in range(_TRI_NEWTON_STEPS):
        lx_list = _pair_dot3_many(lmats, p_list)
        res_list = [(eye - p) - lx for p, lx in zip(p_list, lx_list)]
        p_list = [p + pr for p, pr in zip(p_list, _pair_dot1_many(p_list, res_list))]
    return p_list


def _delta_prep_kernel(q_ref, k_ref, v_ref, gates_ref, gt_ref, *out_refs):
    cs = DELTA_CHUNK
    nc = q_ref.shape[1] // cs
    n_slabs = q_ref.shape[2] // LANES
    r = lax.broadcasted_iota(jnp.int32, (cs, LANES), 0)
    lane = lax.broadcasted_iota(jnp.int32, (cs, LANES), 1)
    c = jnp.bitwise_and(lane, HEAD_DIM - 1)
    low = lane < HEAD_DIM
    er = lax.broadcasted_iota(jnp.int32, (LANES, LANES), 0)
    ec = lax.broadcasted_iota(jnp.int32, (LANES, LANES), 1)
    eye_bf = jnp.where(er == ec, 1.0, 0.0).astype(BF16)
    incl = (r >= c, r <= c)
    strict = (r > c, r < c)

    gates = gates_ref[0]
    gt = gt_ref[0]

    pairs = [(ci, s) for ci in range(nc) for s in range(n_slabs)]
    probs = [(ci, s, d) for d in range(2) for ci in range(nc) for s in range(n_slabs)]

    def tile(ref, ci, s):
        return ref[0, ci * cs:(ci + 1) * cs, s * LANES:(s + 1) * LANES]

    q = {p: tile(q_ref, *p) for p in pairs}
    k = {p: tile(k_ref, *p) for p in pairs}
    v = {p: tile(v_ref, *p) for p in pairs}
    k_bf = {p: k[p].astype(BF16) for p in pairs}
    kq = {p: _dot_nt(jnp.concatenate([k_bf[p], q[p].astype(BF16)], axis=0), _pair_blockdiag(k_bf[p]))
          for p in pairs}
    kk = {p: kq[p][0:cs] for p in pairs}
    qk = {p: kq[p][cs:2 * cs] for p in pairs}

    beta, gc, decay = {}, {}, {}
    for (ci, s, d) in probs:
        rows = slice(ci * cs, (ci + 1) * cs)
        col0 = 4 * d + 2 * s
        beta[ci, s, d] = jnp.where(low, gates[rows, col0:col0 + 1], gates[rows, col0 + 1:col0 + 2])
        gc_col = jnp.where(low, gates[rows, 8 + col0:9 + col0], gates[rows, 9 + col0:10 + col0])
        gc_row = jnp.concatenate([gt[8 + col0:9 + col0, rows], gt[9 + col0:10 + col0, rows]], axis=1)
        gc[ci, s, d] = gc_col
        decay[ci, s, d] = jnp.exp(jnp.where(incl[d], gc_col - gc_row, -jnp.inf))

    lmats = [jnp.where(strict[d], beta[ci, s, d] * kk[ci, s] * decay[ci, s, d], 0.0)
             for (ci, s, d) in probs]
    t_inv = [t.astype(BF16) for t in _pair_tri_inverse_many(lmats, r, c)]

    e = {p: jnp.exp(gc[p]) for p in probs}
    gc_last = {(ci, s, d): (gc[ci, s, d][0:1] if d == 1 else gc[ci, s, d][cs - 1:cs])
               for (ci, s, d) in probs}
    uw = {p: _dot(t, jnp.concatenate(
        [_pair_blockdiag((v[p[0], p[1]] * beta[p]).astype(BF16)),
         _pair_blockdiag((k[p[0], p[1]] * (beta[p] * e[p])).astype(BF16))], axis=1))
        for p, t in zip(probs, t_inv)}
    u = {p: uw[p][:, 0:LANES] for p in probs}
    w = {p: uw[p][:, LANES:2 * LANES] for p in probs}
    kdec_t = {p: _dot_nt(eye_bf, (k[p[0], p[1]] * jnp.exp(gc_last[p] - gc[p])).astype(BF16))
              for p in probs}

    for d in range(2):
        u_ref, w_ref, qg_ref, aqk_ref, kdt_ref, gl_ref = out_refs[6 * d:6 * d + 6]
        for (ci, s) in pairs:
            rows = slice(ci * cs, (ci + 1) * cs)
            lanes = slice(s * LANES, (s + 1) * LANES)
            u_ref[0, rows, lanes] = u[ci, s, d]
            w_ref[0, rows, lanes] = w[ci, s, d].astype(BF16)
            qg_ref[0, rows, lanes] = (q[ci, s] * e[ci, s, d]).astype(BF16)
            aqk_ref[0, rows, lanes] = (qk[ci, s] * decay[ci, s, d]).astype(BF16)
            gl_ref[0, 0, ci:ci + 1, lanes] = jnp.exp(gc_last[ci, s, d])
        for s in range(n_slabs):
            kdt_ref[0, 0, s * LANES:(s + 1) * LANES, :] = jnp.concatenate(
                [kdec_t[ci, s, d] for ci in range(nc)], axis=1).astype(BF16)
        gl_ref[0, 0, nc:SUBLANES, :] = jnp.zeros((SUBLANES - nc, gl_ref.shape[3]), F32)


def _delta_prep(q, k, v, gates, gates_t):
    bsz, seq, width = q.shape
    ta = min(DELTA_TILE, seq)
    nt = seq // ta
    main = lambda b, i: (b, i, 0)
    per_dir_shapes = [jax.ShapeDtypeStruct((bsz, seq, width), F32),
                      jax.ShapeDtypeStruct((bsz, seq, width), BF16),
                      jax.ShapeDtypeStruct((bsz, seq, width), BF16),
                      jax.ShapeDtypeStruct((bsz, seq, width), BF16),
                      jax.ShapeDtypeStruct((bsz, nt, width, ta), BF16),
                      jax.ShapeDtypeStruct((bsz, nt, SUBLANES, width), F32)]
    per_dir_specs = [pl.BlockSpec((1, ta, width), main)] * 4 + [
        pl.BlockSpec((1, 1, width, ta), lambda b, i: (b, i, 0, 0)),
        pl.BlockSpec((1, 1, SUBLANES, width), lambda b, i: (b, i, 0, 0))]
    return pl.pallas_call(
        _delta_prep_kernel,
        out_shape=per_dir_shapes * 2,
        grid=(bsz, nt),
        in_specs=[pl.BlockSpec((1, ta, width), main)] * 3 + [
            pl.BlockSpec((1, ta, LANES), main),
            pl.BlockSpec((1, 2 * SUBLANES, ta), lambda b, i: (b, 0, i))],
        out_specs=per_dir_specs * 2,
        compiler_params=_cparams(("arbitrary", "arbitrary"), big=True),
        name="delta_prep",
    )(q, k, v, gates, gates_t)


DELTA_SCAN_BATCH = 4


def _delta_scan_kernel(*refs):
    ins = (refs[0:6], refs[6:12])
    s0_ref = refs[12]
    outs = refs[13:15]
    send_ref, s_scr = refs[15:17]
    t = pl.program_id(1)
    cs = DELTA_CHUNK
    hd = HEAD_DIM
    nb, ta, width = ins[0][0].shape
    nc = ta // cs
    heads = width // hd
    rr = lax.broadcasted_iota(jnp.int32, (width, width), 0)
    cc = lax.broadcasted_iota(jnp.int32, (width, width), 1)
    same_head = lax.shift_right_logical(rr, 6) == lax.shift_right_logical(cc, 6)
    lane_head = lax.shift_right_logical(lax.broadcasted_iota(jnp.int32, (cs, width), 1), 6)
    zeros = jnp.zeros((cs, width), BF16)

    @pl.when(t == 0)
    def _():
        s_scr[...] = s0_ref[...]

    chains = [(bi, d) for bi in range(nb) for d in range(2)]
    for step in range(nc):
        chunk = (step, nc - 1 - step)
        rows = [slice(chunk[d] * cs, (chunk[d] + 1) * cs) for d in range(2)]
        s_prev = {(bi, d): s_scr[bi, d] for (bi, d) in chains}
        lhs = {(bi, d): jnp.concatenate([ins[d][1][bi, rows[d], :], ins[d][2][bi, rows[d], :]], axis=0)
               for (bi, d) in chains}
        res = {ch: _dot(lhs[ch], s_prev[ch].astype(BF16)) for ch in chains}
        v_new = {(bi, d): (ins[d][0][bi, rows[d], :] - res[bi, d][0:cs]).astype(BF16)
                 for (bi, d) in chains}
        v_bd = {ch: jnp.concatenate([jnp.where(lane_head == h, v_new[ch], zeros)
                                     for h in range(heads)], axis=0) for ch in chains}
        v_tile = {(bi, d): jnp.concatenate(
            [v_new[bi, d] if ci == chunk[d] else zeros for ci in range(nc)], axis=0)
            for (bi, d) in chains}
        o = {(bi, d): res[bi, d][cs:2 * cs] + _dot(ins[d][3][bi, rows[d], :], v_bd[bi, d])
             for (bi, d) in chains}
        upd = {(bi, d): _dot(ins[d][4][bi, 0], v_tile[bi, d]) for (bi, d) in chains}
        for (bi, d) in chains:
            gl = ins[d][5][bi, 0, chunk[d]:chunk[d] + 1, :]
            s_scr[bi, d] = s_prev[bi, d] * gl + jnp.where(same_head, upd[bi, d], 0.0)
            outs[d][bi, rows[d], :] = o[bi, d]

    @pl.when(t == pl.num_programs(1) - 1)
    def _():
        send_ref[...] = s_scr[...]


def _delta_scan(prep, s0_bd):
    u_f = prep[0]
    bsz, seq, width = u_f.shape
    ta = min(DELTA_TILE, seq)
    nt = seq // ta
    nb = DELTA_SCAN_BATCH if bsz % DELTA_SCAN_BATCH == 0 else 1

    def specs(index3, index4):
        return [pl.BlockSpec((nb, ta, width), index3)] * 4 + [
            pl.BlockSpec((nb, 1, width, ta), index4),
            pl.BlockSpec((nb, 1, SUBLANES, width), index4)]

    fwd3 = lambda b, t: (b, t, 0)
    fwd4 = lambda b, t: (b, t, 0, 0)
    bwd3 = lambda b, t: (b, nt - 1 - t, 0)
    bwd4 = lambda b, t: (b, nt - 1 - t, 0, 0)
    state_spec = pl.BlockSpec((nb, 2, width, width), lambda b, t: (b, 0, 0, 0))
    return pl.pallas_call(
        _delta_scan_kernel,
        out_shape=[jax.ShapeDtypeStruct((bsz, seq, width), F32),
                   jax.ShapeDtypeStruct((bsz, seq, width), F32),
                   jax.ShapeDtypeStruct((bsz, 2, width, width), F32)],
        grid=(bsz // nb, nt),
        in_specs=specs(fwd3, fwd4) + specs(bwd3, bwd4) + [state_spec],
        out_specs=[pl.BlockSpec((nb, ta, width), fwd3), pl.BlockSpec((nb, ta, width), bwd3),
                   state_spec],
        scratch_shapes=[pltpu.VMEM((nb, 2, width, width), F32)],
        compiler_params=_cparams(("arbitrary", "arbitrary")),
        name="delta_scan",
    )(*prep, s0_bd)


def _delta_out_kernel(of_ref, ob_ref, z_ref, ones_ref, g_ref, o_ref):
    ones_bd = ones_ref[...]
    for s in range(of_ref.shape[2] // LANES):
        lanes = slice(s * LANES, (s + 1) * LANES)
        o = of_ref[0, :, lanes] + ob_ref[0, :, lanes]
        ms = _group_sumsq(o, ones_bd) * (1.0 / HEAD_DIM)
        y = o * lax.rsqrt(ms + EPS) * g_ref[...]
        o_ref[0, :, lanes] = (y * _silu(z_ref[0, :, lanes])).astype(BF16)


def _delta_out(o_f, o_b, pc, ones_bd, norm_g):
    bsz, seq, width = o_f.shape
    tl = min(512, seq)
    main = lambda b, i: (b, i, 0)
    const2 = lambda b, i: (0, 0)
    z_block = pc.shape[2] // width - 1
    return pl.pallas_call(
        _delta_out_kernel,
        out_shape=jax.ShapeDtypeStruct((bsz, seq, width), BF16),
        grid=(bsz, seq // tl),
        in_specs=[pl.BlockSpec((1, tl, width), main), pl.BlockSpec((1, tl, width), main),
                  pl.BlockSpec((1, tl, width), lambda b, i: (b, i, z_block)),
                  pl.BlockSpec((LANES, LANES), const2),
                  pl.BlockSpec((1, LANES), const2)],
        out_specs=pl.BlockSpec((1, tl, width), main),
        compiler_params=_cparams(("arbitrary", "arbitrary")),
        name="delta_out",
    )(o_f, o_b, pc, ones_bd, jnp.tile(norm_g, LANES // HEAD_DIM).reshape(1, LANES))


def _deltanet(q, k, v, pc, gates, gates_t, s0, norm_g, ones_bd):
    bsz = q.shape[0]
    heads, hd = N_HEADS_C, HEAD_DIM
    eye_h = jnp.eye(heads, dtype=F32)
    s0_bd = jnp.einsum('bdhij,hg->bdhigj', s0.astype(F32), eye_h).reshape(
        bsz, 2, heads * hd, heads * hd)
    prep = _delta_prep(q, k, v, gates, gates_t)
    o_f, o_b, s_bd = _delta_scan(prep, s0_bd)
    o_c = _delta_out(o_f, o_b, pc, ones_bd, norm_g)
    s_blocks = s_bd.reshape(bsz, 2, heads, hd, heads, hd)
    s_end = jnp.stack([s_blocks[:, :, h, :, h, :] for h in range(heads)], axis=2)
    return o_c, s_end


def _fourier_ch_kernel(x_ref, c_ref, s_ref, w_ref):
    x = x_ref[0].astype(BF16)
    w_ref[0] = _dot(x, c_ref[...].astype(BF16)).astype(BF16)
    w_ref[1] = _dot(x, s_ref[...].astype(BF16)).astype(BF16)


def _fourier_channels(xd, c_bd, s_bd):
    bsz, seq, gw = xd.shape
    tl = min(512, seq)
    const2 = lambda b, i: (0, 0)
    return pl.pallas_call(
        _fourier_ch_kernel,
        out_shape=jax.ShapeDtypeStruct((2, seq, bsz * gw), BF16),
        grid=(bsz, seq // tl),
        in_specs=[pl.BlockSpec((1, tl, gw), lambda b, i: (b, i, 0)),
                  pl.BlockSpec((gw, gw), const2),
                  pl.BlockSpec((gw, gw), const2)],
        out_specs=pl.BlockSpec((2, tl, gw), lambda b, i: (0, i, b)),
        compiler_params=_cparams(("arbitrary", "arbitrary")),
        name="fourier_channels",
    )(xd, c_bd, s_bd)


def _fourier_pos_kernel(w_ref, a1_ref, b1_ref, a0_ref, b0_ref, o_ref, c_scr, s_scr, *, scale):
    i = pl.program_id(1)
    tm = c_scr.shape[0]
    sub = a0_ref.shape[0]
    a0 = a0_ref[...]
    b0 = b0_ref[...]
    for rblk in range(tm // sub):
        j1 = i * (tm // sub) + rblk
        a1 = a1_ref[pl.ds(j1, 1), :]
        b1 = b1_ref[pl.ds(j1, 1), :]
        c_scr[rblk * sub:(rblk + 1) * sub, :] = (a1 * a0 - b1 * b0).astype(BF16)
        s_scr[rblk * sub:(rblk + 1) * sub, :] = (b1 * a0 + a1 * b0).astype(BF16)
    res = (_dot(c_scr[...], w_ref[0]) - _dot(s_scr[...], w_ref[1])) * scale
    gw = o_ref.shape[2]
    for b in range(o_ref.shape[0]):
        o_ref[b] = res[:, b * gw:(b + 1) * gw].astype(BF16)


def _fourier_positions(w, tabs, bsz, gw):
    _, seq, ncol = w.shape
    a1, b1, a0, b0 = tabs
    tm = min(512, seq)
    nsplit = 2 if (bsz % 2 == 0 and seq > 1024) else 1
    bpart = bsz // nsplit
    const2 = lambda p, i: (0, 0)
    single = pl.Buffered(1)
    scale = 1.0 / np.sqrt(float(seq) * HEAD_DIM)
    return pl.pallas_call(
        functools.partial(_fourier_pos_kernel, scale=scale),
        out_shape=jax.ShapeDtypeStruct((bsz, seq, gw), BF16),
        grid=(nsplit, seq // tm),
        in_specs=[pl.BlockSpec((2, seq, bpart * gw), lambda p, i: (0, 0, p), pipeline_mode=single),
                  pl.BlockSpec(a1.shape, const2, pipeline_mode=single),
                  pl.BlockSpec(b1.shape, const2, pipeline_mode=single),
                  pl.BlockSpec(a0.shape, const2, pipeline_mode=single),
                  pl.BlockSpec(b0.shape, const2, pipeline_mode=single)],
        out_specs=pl.BlockSpec((bpart, tm, gw), lambda p, i: (p, i, 0)),
        scratch_shapes=[pltpu.VMEM((tm, seq), BF16), pltpu.VMEM((tm, seq), BF16)],
        compiler_params=_cparams(("arbitrary", "arbitrary"), big=True),
        name="fourier_positions",
    )(w, a1, b1, a0, b0)


def _rope_tables(seq):
    rows = seq // GRID_W
    r, cl = jnp.meshgrid(jnp.arange(rows), jnp.arange(GRID_W), indexing='ij')
    pos_r = r.reshape(-1).astype(F32)
    pos_c = cl.reshape(-1).astype(F32)
    quarter = HEAD_DIM // 4
    inv_freq = ROPE_THETA ** (-jnp.arange(quarter, dtype=F32) / quarter)
    cos_parts, sin_parts = [], []
    for pos in (pos_r, pos_c):
        ang = pos[:, None] * inv_freq[None, :]
        cos_parts += [jnp.cos(ang), jnp.cos(ang)]
        sin_parts += [-jnp.sin(ang), jnp.sin(ang)]
    cos = jnp.concatenate(cos_parts, axis=-1)
    sin = jnp.concatenate(sin_parts, axis=-1)
    return jnp.tile(cos, (1, 2)), jnp.tile(sin, (1, 2))


def _trig_table(num_rows, row_mult, seq):
    j = np.arange(num_rows, dtype=np.int64)[:, None] * row_mult
    k = np.arange(seq, dtype=np.int64)[None, :]
    ang = 2.0 * np.pi * ((j * k) % seq).astype(np.float64) / seq
    return jnp.asarray(np.cos(ang), F32), jnp.asarray(np.sin(ang), F32)


def _fourier_tables(seq):
    sub = 64
    a1, b1 = _trig_table(seq // sub, sub, seq)
    a0, b0 = _trig_table(sub, 1, seq)
    idx = np.arange(GROUP_WIDTH)
    same = (idx[:, None] // HEAD_DIM) == (idx[None, :] // HEAD_DIM)
    ang = 2.0 * np.pi * ((idx[:, None] % HEAD_DIM) * (idx[None, :] % HEAD_DIM) % HEAD_DIM) / HEAD_DIM
    c_bd = jnp.asarray(np.where(same, np.cos(ang), 0.0), F32)
    s_bd = jnp.asarray(np.where(same, np.sin(ang), 0.0), F32)
    return (a1, b1, a0, b0), c_bd, s_bd


def _to_lane_dense(t):
    b, hh, l, d = t.shape
    return jnp.swapaxes(t, 1, 2).reshape(b, l, hh * d)


def _to_head_major(t, heads):
    b, l, w = t.shape
    return jnp.swapaxes(t.reshape(b, l, heads, w // heads), 1, 2)


def _swap_halves(t):
    return jnp.concatenate([t[..., HEAD_DIM:], t[..., :HEAD_DIM]], axis=-1)


def _trunk_layer(x, mod, wts, consts, ctx, final_g):
    bsz, seq, _ = x.shape
    x1 = _ffn(x, mod, wts['norm_ffn1'], wts['ffn1_w1'], wts['ffn1_w3'], wts['ffn1_w2'], which=0)
    pa, pb, pc, pd, ps = _inproj(x1, mod, wts['norm_mix'], wts['w_in'])

    rope_tabs = consts['rope'] if ctx is not None else None
    prep = _attn_prep(pa, consts['ones_bd'], wts['gq'], wts['gk'], rope_tabs, ctx is None)
    q, ka, kb, va, vb = prep[:5]
    k_cache = v_cache = None
    if ctx is None:
        k_cache = _to_head_major(prep[5], KV_HEADS_A)
        v_cache = _to_head_major(pa[..., 384:512], KV_HEADS_A)
        s0 = jnp.zeros((bsz, 2, N_HEADS_C, HEAD_DIM, HEAD_DIM), F32)
    else:
        k_ctx, v_ctx, s0 = ctx
        k_past = _to_lane_dense(k_ctx).astype(BF16)
        v_past = _to_lane_dense(v_ctx).astype(BF16)
        ka = jnp.concatenate([k_past, ka], axis=1)
        kb = jnp.concatenate([_swap_halves(k_past), kb], axis=1)
        ones = jnp.ones_like(v_past)
        va = jnp.concatenate([jnp.concatenate([v_past, ones], axis=-1), va], axis=1)
        vb = jnp.concatenate([jnp.concatenate([_swap_halves(v_past), ones], axis=-1), vb], axis=1)
    o_a = _attention(q, ka, kb, va, vb)

    o_b, qc, kc, vc, gates, gates_t = _conv_prep(pb, pc, ps, wts['conv_b_w'], wts['conv_c_w'],
                                                 consts['ones_bd'], wts['alog_row'], wts['dtb_row'])
    o_c, s_end = _deltanet(qc, kc, vc, pc, gates, gates_t, s0, wts['delta_norm'], consts['ones_bd'])

    w = _fourier_channels(pd, consts['c_bd'], consts['s_bd'])
    o_d = _fourier_positions(w, consts['fourier'], bsz, GROUP_WIDTH)

    x3 = _ffn(x1, mod, wts['norm_ffn2'], wts['ffn2_w1'], wts['ffn2_w3'], wts['ffn2_w2'], which=2,
              mix=(o_a, o_b, o_c, o_d), w_out=wts['w_out'], final_g=final_g)
    return x3, k_cache, v_cache, s_end


def _lane_row(values, start):
    row = jnp.zeros((1, LANES), F32)
    return row.at[0, start:start + values.shape[0]].set(values.astype(F32))


def kernel(x_prompt, x_sample, c, cache_k, cache_v, state_delta, c_ctx, mod_w, mod_b, norm_ffn1, norm_mix, norm_ffn2, ffn1_w1, ffn1_w3, ffn1_w2, ffn2_w1, ffn2_w3, ffn2_w2, w_in, w_out, q_norm, k_norm, conv_b_w, conv_c_w, delta_a_log, delta_dt_bias, delta_norm, final_norm):
    depth = mod_w.shape[0]
    d_model = x_prompt.shape[-1]
    dec_b = x_sample.shape[0]
    assert w_in.shape[-1] == PROJ_MAIN + PROJ_SMALL + GROUP_WIDTH

    n_cond = 1 + dec_b
    rows = -(-n_cond // SUBLANES) * SUBLANES
    cond = jnp.zeros((rows, d_model), F32).at[0].set(c_ctx).at[1:n_cond].set(c)

    idx = np.arange(LANES)
    ones_bd = jnp.asarray((idx[:, None] // HEAD_DIM) == (idx[None, :] // HEAD_DIM), BF16)
    consts_p = {'ones_bd': ones_bd}
    consts_s = {'ones_bd': ones_bd, 'rope': _rope_tables(x_sample.shape[1])}
    consts_p['fourier'], consts_p['c_bd'], consts_p['s_bd'] = _fourier_tables(x_prompt.shape[1])
    consts_s['fourier'], consts_s['c_bd'], consts_s['s_bd'] = _fourier_tables(x_sample.shape[1])

    yp, ys = x_prompt, x_sample
    k_list, v_list, s_list = [], [], []
    for l in range(depth):
        w_in_l = w_in[l]
        small = w_in_l[:, PROJ_MAIN:PROJ_MAIN + PROJ_SMALL]
        w_in_r = jnp.concatenate(
            [w_in_l[:, :PROJ_MAIN], w_in_l[:, PROJ_MAIN + PROJ_SMALL:], small,
             jnp.zeros((d_model, LANES - PROJ_SMALL), F32)], axis=1).astype(BF16)
        wts = {
            'norm_ffn1': norm_ffn1[l], 'norm_mix': norm_mix[l], 'norm_ffn2': norm_ffn2[l],
            'ffn1_w1': ffn1_w1[l].astype(BF16), 'ffn1_w3': ffn1_w3[l].astype(BF16),
            'ffn1_w2': ffn1_w2[l].astype(BF16),
            'ffn2_w1': ffn2_w1[l].astype(BF16), 'ffn2_w3': ffn2_w3[l].astype(BF16),
            'ffn2_w2': ffn2_w2[l].astype(BF16),
            'w_in': w_in_r, 'w_out': w_out[l].astype(BF16),
            'gq': jnp.tile(q_norm[l], 2).reshape(1, LANES),
            'gk': jnp.tile(k_norm[l], 2).reshape(1, LANES),
            'conv_b_w': conv_b_w[l], 'conv_c_w': conv_c_w[l],
            'alog_row': _lane_row(delta_a_log[l].reshape(-1), 8),
            'dtb_row': _lane_row(delta_dt_bias[l].reshape(-1), 8),
            'delta_norm': delta_norm[l],
        }
        mod = _modulation(cond, mod_w[l], mod_b[l]).reshape(rows, N_MOD, d_model)
        fin = final_norm if l == depth - 1 else None
        yp, k_l, v_l, s_l = _trunk_layer(yp, mod[0:1], wts, consts_p, None, fin)
        k_list.append(k_l)
        v_list.append(v_l)
        s_list.append(s_l)
        ys, _, _, _ = _trunk_layer(ys, mod[1:n_cond], wts, consts_s,
                                   (cache_k[:, l], cache_v[:, l], state_delta[:, l]), fin)
    return (yp, ys, jnp.stack(k_list, axis=1), jnp.stack(v_list, axis=1),
            jnp.stack(s_list, axis=1))
```

```python
import functools

import jax
import jax.numpy as jnp
import numpy as np
from jax import lax
from jax.experimental import pallas as pl
from jax.experimental.pallas import tpu as pltpu

F32 = jnp.float32
BF16 = jnp.bfloat16

HEAD_DIM = 64
N_HEADS_A = 4
KV_HEADS_A = 2
N_HEADS_C = 4
GROUP_WIDTH = 256
GRID_W = 64
DELTA_CHUNK = 64
ROPE_THETA = 10000.0
N_MOD = 9
EPS = 1e-6
PROJ_MAIN = 2304
PROJ_SMALL = 16
LANES = 128
SUBLANES = 8
VMEM_LIMIT_BYTES = 56 * 1024 * 1024


def _cparams(sem, big=False):
    return pltpu.CompilerParams(
        dimension_semantics=sem,
        vmem_limit_bytes=VMEM_LIMIT_BYTES if big else None)


def _dot(a, b):
    return jnp.dot(a, b, preferred_element_type=F32)


def _dot_nt(a, b):
    return lax.dot_general(a, b, (((1,), (1,)), ((), ())), preferred_element_type=F32)


def _split2(x):
    hi = x.astype(BF16)
    lo = (x - hi.astype(F32)).astype(BF16)
    return hi, lo


def _split3(x):
    hi = x.astype(BF16)
    r = x - hi.astype(F32)
    mid = r.astype(BF16)
    lo = (r - mid.astype(F32)).astype(BF16)
    return hi, mid, lo


def _rms(x, g):
    return x * lax.rsqrt(jnp.mean(x * x, axis=-1, keepdims=True) + EPS) * g


def _silu(x):
    return x * jax.nn.sigmoid(x)


def _group_sumsq(x, ones_bd):
    hi, lo = _split2(x * x)
    return _dot(hi, ones_bd) + _dot(lo, ones_bd)


def _mod_kernel(c_ref, w_ref, b_ref, o_ref):
    s = _silu(c_ref[...]).astype(BF16)
    o_ref[...] = _dot(s, w_ref[...].astype(BF16)) + b_ref[...]


def _modulation(cond, mod_w, mod_b):
    rows, d = cond.shape
    n = mod_w.shape[1]
    tn = d
    return pl.pallas_call(
        _mod_kernel,
        out_shape=jax.ShapeDtypeStruct((rows, n), F32),
        grid=(n // tn,),
        in_specs=[pl.BlockSpec((rows, d), lambda j: (0, 0)),
                  pl.BlockSpec((d, tn), lambda j: (0, j)),
                  pl.BlockSpec((1, tn), lambda j: (0, j))],
        out_specs=pl.BlockSpec((rows, tn), lambda j: (0, j)),
        compiler_params=_cparams(("arbitrary",)),
        name="modulation",
    )(cond, mod_w, mod_b.reshape(1, n))


def _ffn_chunks(f):
    step = 1024
    return [(s, min(s + step, f)) for s in range(0, f, step)]


def _gated_delta_out(of_ref, ob_ref, z_ref, ones_ref, g_ref):
    ones_bd = ones_ref[...]
    parts = []
    for s in range(of_ref.shape[2] // LANES):
        lanes = slice(s * LANES, (s + 1) * LANES)
        o = of_ref[0, :, lanes] + ob_ref[0, :, lanes]
        ms = _group_sumsq(o, ones_bd) * (1.0 / HEAD_DIM)
        y = o * lax.rsqrt(ms + EPS) * g_ref[...]
        parts.append((y * _silu(z_ref[0, :, lanes])).astype(BF16))
    return jnp.concatenate(parts, axis=-1)


def _ffn_kernel(*refs, which, n_mix, final):
    x_ref, mod_ref, g_ref, w1_ref, w3_ref, w2_ref = refs[:6]
    pos = 6
    mix_vals = []
    wout_ref = None
    if n_mix:
        oa_ref, ob_ref, dof_ref, dob_ref, z_ref, ones_ref, dg_ref, od_ref, wout_ref = refs[pos:pos + 9]
        pos += 9
        mix_vals = [oa_ref[0], ob_ref[0],
                    _gated_delta_out(dof_ref, dob_ref, z_ref, ones_ref, dg_ref), od_ref[0]]
    gf_ref = None
    if final:
        gf_ref = refs[pos]
        pos += 1
    o_ref = refs[pos]

    x = x_ref[0]
    mod = mod_ref[0]
    if n_mix:
        acc = None
        for i, m in enumerate(mix_vals):
            w = m.shape[-1]
            part = _dot(m, wout_ref[i * w:(i + 1) * w, :])
            acc = part if acc is None else acc + part
        x = x + mod[5:6] * acc
    sh = mod[3 * which:3 * which + 1]
    sc = mod[3 * which + 1:3 * which + 2]
    gt = mod[3 * which + 2:3 * which + 3]
    h = (_rms(x, g_ref[...]) * (1.0 + sc) + sh).astype(BF16)
    out = None
    for s, e in _ffn_chunks(w1_ref.shape[1]):
        a = _dot(h, w1_ref[:, s:e])
        b = _dot(h, w3_ref[:, s:e])
        act = (_silu(a) * b).astype(BF16)
        part = _dot(act, w2_ref[s:e, :])
        out = part if out is None else out + part
    xn = x + 0.5 * gt * out
    if final:
        xn = _rms(xn, gf_ref[...])
    o_ref[0] = xn


def _ffn(x, mod, g, w1, w3, w2, *, which, mix=None, w_out=None, final_g=None):
    bsz, seq, d = x.shape
    f = w1.shape[1]
    tm = min(512, seq)
    per_batch_mod = mod.shape[0] > 1
    mod_map = (lambda b, i: (b, 0, 0)) if per_batch_mod else (lambda b, i: (0, 0, 0))
    const2 = lambda b, i: (0, 0)
    single = pl.Buffered(1)
    args = [x, mod, g.reshape(1, d), w1, w3, w2]
    specs = [pl.BlockSpec((1, tm, d), lambda b, i: (b, i, 0)),
             pl.BlockSpec((1, N_MOD, d), mod_map),
             pl.BlockSpec((1, d), const2),
             pl.BlockSpec((d, f), const2, pipeline_mode=single),
             pl.BlockSpec((d, f), const2, pipeline_mode=single),
             pl.BlockSpec((f, d), const2, pipeline_mode=single)]
    n_mix = 0
    if mix is not None:
        o_a, o_b, (d_f, d_b, pc, ones_bd, delta_g), o_d = mix
        n_mix = 4
        tok = lambda b, i: (b, i, 0)
        gw = o_a.shape[-1]
        z_block = pc.shape[2] // gw - 1
        args += [o_a, o_b, d_f, d_b, pc, ones_bd,
                 jnp.tile(delta_g, LANES // HEAD_DIM).reshape(1, LANES), o_d, w_out]
        specs += [pl.BlockSpec((1, tm, gw), tok), pl.BlockSpec((1, tm, gw), tok),
                  pl.BlockSpec((1, tm, gw), tok), pl.BlockSpec((1, tm, gw), tok),
                  pl.BlockSpec((1, tm, gw), lambda b, i: (b, i, z_block)),
                  pl.BlockSpec((LANES, LANES), const2), pl.BlockSpec((1, LANES), const2),
                  pl.BlockSpec((1, tm, gw), tok),
                  pl.BlockSpec(w_out.shape, const2, pipeline_mode=single)]
    if final_g is not None:
        args.append(final_g.reshape(1, d))
        specs.append(pl.BlockSpec((1, d), const2))
    return pl.pallas_call(
        functools.partial(_ffn_kernel, which=which, n_mix=n_mix, final=final_g is not None),
        out_shape=jax.ShapeDtypeStruct((bsz, seq, d), F32),
        grid=(bsz, seq // tm),
        in_specs=specs,
        out_specs=pl.BlockSpec((1, tm, d), lambda b, i: (b, i, 0)),
        compiler_params=_cparams(("arbitrary", "arbitrary"), big=True),
        name="ffn",
    )(*args)


_PROJ_A = (0, 512)
_PROJ_B = (512, 1280)
_PROJ_C = (1280, 2304)
_PROJ_D = (2304, 2560)
_PROJ_S = (2560, 2688)


def _rope(x, cos, sin_signed):
    lane = lax.broadcasted_iota(jnp.int32, x.shape, 1)
    up = pltpu.roll(x, LANES - 16, axis=1)
    down = pltpu.roll(x, 16, axis=1)
    partner = jnp.where((lane % 32) < 16, up, down)
    return x * cos + partner * sin_signed


def _inproj_kernel(*refs, rope, want_cache):
    x_ref, mod_ref, g_ref, w_ref, ones_ref, gq_ref, gk_ref, cbd_ref, sbd_ref = refs[:9]
    pos = 9
    if rope:
        cos_ref, sin_ref = refs[pos:pos + 2]
        pos += 2
    q_ref, ka_ref, kb_ref, va_ref, vb_ref, pb_ref, pc_ref, ps_ref, wd_ref = refs[pos:pos + 9]
    pos += 9
    mod = mod_ref[0]
    h = (_rms(x_ref[0], g_ref[...]) * (1.0 + mod[4:5]) + mod[3:4]).astype(BF16)

    def proj(cols):
        return _dot(h, w_ref[:, cols[0]:cols[1]])

    pb_ref[0] = proj(_PROJ_B)
    pc_ref[0] = proj(_PROJ_C)
    ps_ref[0] = proj(_PROJ_S)

    xd = proj(_PROJ_D).astype(BF16)
    wd_ref[0] = _dot(xd, cbd_ref[...].astype(BF16)).astype(BF16)
    wd_ref[1] = _dot(xd, sbd_ref[...].astype(BF16)).astype(BF16)

    pa = proj(_PROJ_A)
    ones_bd = ones_ref[...]

    def normed(x, g):
        ss = _group_sumsq(x, ones_bd)
        return x * lax.rsqrt(ss * (1.0 / HEAD_DIM) + EPS) * g

    q0 = normed(pa[:, 0:128], gq_ref[...])
    q1 = normed(pa[:, 128:256], gq_ref[...])
    k = normed(pa[:, 256:384], gk_ref[...])
    v = pa[:, 384:512]
    if want_cache:
        refs[pos][0] = k
        refs[pos + 1][0] = v
    if rope:
        cos = cos_ref[...]
        sin = sin_ref[...]
        q0 = _rope(q0, cos, sin)
        q1 = _rope(q1, cos, sin)
        k = _rope(k, cos, sin)
    scale = HEAD_DIM ** -0.5
    q_ref[0, :, 0:128] = (q0 * scale).astype(BF16)
    q_ref[0, :, 128:256] = (q1 * scale).astype(BF16)
    ka_ref[0] = k.astype(BF16)
    kb_ref[0] = pltpu.roll(k, HEAD_DIM, axis=1).astype(BF16)
    ones = jnp.ones(v.shape, BF16)
    va_ref[0, :, 0:LANES] = v.astype(BF16)
    va_ref[0, :, LANES:2 * LANES] = ones
    vb_ref[0, :, 0:LANES] = pltpu.roll(v, HEAD_DIM, axis=1).astype(BF16)
    vb_ref[0, :, LANES:2 * LANES] = ones


def _inproj(x, mod, g, w_in_r, ones_bd, gq, gk, c_bd, s_bd, rope_tabs, want_cache):
    bsz, seq, d = x.shape
    tm = min(512, seq)
    per_batch_mod = mod.shape[0] > 1
    mod_map = (lambda b, i: (b, 0, 0)) if per_batch_mod else (lambda b, i: (0, 0, 0))
    const2 = lambda b, i: (0, 0)
    tok = lambda b, i: (b, i, 0)
    rope = rope_tabs is not None
    gw = GROUP_WIDTH
    args = [x, mod, g.reshape(1, d), w_in_r, ones_bd, gq, gk, c_bd, s_bd]
    specs = [pl.BlockSpec((1, tm, d), tok),
             pl.BlockSpec((1, N_MOD, d), mod_map),
             pl.BlockSpec((1, d), const2),
             pl.BlockSpec(w_in_r.shape, const2, pipeline_mode=pl.Buffered(1)),
             pl.BlockSpec((LANES, LANES), const2),
             pl.BlockSpec((1, LANES), const2),
             pl.BlockSpec((1, LANES), const2),
             pl.BlockSpec((gw, gw), const2),
             pl.BlockSpec((gw, gw), const2)]
    if rope:
        args += list(rope_tabs)
        specs += [pl.BlockSpec((tm, LANES), lambda b, i: (i, 0))] * 2

    def tok_out(width, dtype):
        return jax.ShapeDtypeStruct((bsz, seq, width), dtype), pl.BlockSpec((1, tm, width), tok)

    outs = [tok_out(256, BF16),
            tok_out(LANES, BF16), tok_out(LANES, BF16),
            tok_out(2 * LANES, BF16), tok_out(2 * LANES, BF16),
            tok_out(_PROJ_B[1] - _PROJ_B[0], F32),
            tok_out(_PROJ_C[1] - _PROJ_C[0], F32),
            tok_out(_PROJ_S[1] - _PROJ_S[0], F32),
            (jax.ShapeDtypeStruct((2, seq, bsz * gw), BF16),
             pl.BlockSpec((2, tm, gw), lambda b, i: (0, i, b)))]
    if want_cache:
        outs += [tok_out(LANES, F32), tok_out(LANES, F32)]
    return pl.pallas_call(
        functools.partial(_inproj_kernel, rope=rope, want_cache=want_cache),
        out_shape=[o[0] for o in outs],
        grid=(bsz, seq // tm),
        in_specs=specs,
        out_specs=[o[1] for o in outs],
        compiler_params=_cparams(("arbitrary", "arbitrary"), big=True),
        name="inproj",
    )(*args)


def _attn_kernel(q_ref, ka_ref, kb_ref, va_ref, vb_ref, o_ref):
    tq = q_ref.shape[1]
    lane = lax.broadcasted_iota(jnp.int32, (tq, LANES), 1)
    low = lane < HEAD_DIM
    arrangement = ((ka_ref, va_ref), (kb_ref, vb_ref), (kb_ref, vb_ref), (ka_ref, va_ref))
    def scores(h):
        slab = q_ref[0, :, (h // 2) * LANES:(h // 2 + 1) * LANES]
        keep = low if h % 2 == 0 else jnp.logical_not(low)
        qh = jnp.where(keep, slab, jnp.zeros_like(slab))
        return _dot_nt(qh, arrangement[h][0][0])

    def weighted_values(h, s):
        m = jnp.max(s, axis=-1, keepdims=True)
        p = jnp.exp(s - m)
        pv = _dot(p.astype(BF16), arrangement[h][1][0])
        return pv[:, 0:LANES] / pv[:, LANES:LANES + 1]

    outs = []
    s_next = scores(0)
    for h in range(N_HEADS_A):
        s_cur = s_next
        if h + 1 < N_HEADS_A:
            s_next = scores(h + 1)
        outs.append(weighted_values(h, s_cur))
    o_ref[0, :, 0:128] = jnp.where(low, outs[0], outs[1]).astype(BF16)
    o_ref[0, :, 128:256] = jnp.where(low, outs[2], outs[3]).astype(BF16)


def _attention(q, ka, kb, va, vb):
    bsz, seq, _ = q.shape
    lk = ka.shape[1]
    tq = min(512, seq)
    k_spec = pl.BlockSpec((1, lk, LANES), lambda b, i: (b, 0, 0))
    v_spec = pl.BlockSpec((1, lk, 2 * LANES), lambda b, i: (b, 0, 0))
    return pl.pallas_call(
        _attn_kernel,
        out_shape=jax.ShapeDtypeStruct((bsz, seq, 256), BF16),
        grid=(bsz, seq // tq),
        in_specs=[pl.BlockSpec((1, tq, 256), lambda b, i: (b, i, 0)),
                  k_spec, k_spec, v_spec, v_spec],
        out_specs=pl.BlockSpec((1, tq, 256), lambda b, i: (b, i, 0)),
        compiler_params=_cparams(("arbitrary", "arbitrary"), big=True),
        name="attention",
    )(q, ka, kb, va, vb)


def _conv3(x, prev_row, next_row, w):
    rows = x.shape[0]
    ridx = lax.broadcasted_iota(jnp.int32, x.shape, 0)
    xm = jnp.where(ridx == 0, prev_row, pltpu.roll(x, 1, axis=0))
    xp = jnp.where(ridx == rows - 1, next_row, pltpu.roll(x, rows - 1, axis=0))
    return xm * w[0:1] + x * w[1:2] + xp * w[2:3]


def _conv_prep_kernel(pb_ref, pbp_ref, pbn_ref, pc_ref, pcp_ref, pcn_ref, ps_ref,
                      wb_ref, wc_ref, ones_ref, alog_ref, dtb_ref, tril_ref, triu_ref,
                      ob_ref, q_ref, k_ref, v_ref, gates_ref, gatest_ref):
    i = pl.program_id(1)
    has_prev = i > 0
    has_next = i < pl.num_programs(1) - 1
    gw = GROUP_WIDTH

    pb = pb_ref[0]
    u = pb[:, gw:2 * gw] * pb[:, 2 * gw:3 * gw]
    pbp = pbp_ref[0, SUBLANES - 1:SUBLANES, :]
    pbn = pbn_ref[0, 0:1, :]
    u_prev = jnp.where(has_prev, pbp[:, gw:2 * gw] * pbp[:, 2 * gw:3 * gw], 0.0)
    u_next = jnp.where(has_next, pbn[:, gw:2 * gw] * pbn[:, 2 * gw:3 * gw], 0.0)
    ob_ref[0] = (pb[:, 0:gw] * _conv3(u, u_prev, u_next, wb_ref[...])).astype(BF16)

    pc = pc_ref[0]
    x3 = pc[:, 0:3 * gw]
    x_prev = jnp.where(has_prev, pcp_ref[0, SUBLANES - 1:SUBLANES, 0:3 * gw], 0.0)
    x_next = jnp.where(has_next, pcn_ref[0, 0:1, 0:3 * gw], 0.0)
    qkv = _silu(_conv3(x3, x_prev, x_next, wc_ref[...]))
    ones_bd = ones_ref[...]

    def l2n(x):
        return x * lax.rsqrt(_group_sumsq(x, ones_bd) + EPS)

    q_scale = HEAD_DIM ** -0.5
    for s in range(2):
        q_ref[0, :, s * LANES:(s + 1) * LANES] = l2n(qkv[:, s * LANES:(s + 1) * LANES]) * q_scale
        k_ref[0, :, s * LANES:(s + 1) * LANES] = l2n(qkv[:, gw + s * LANES:gw + (s + 1) * LANES])
    v_ref[0] = qkv[:, 2 * gw:3 * gw]

    raw = ps_ref[0]
    beta = jax.nn.sigmoid(raw)
    y = raw + dtb_ref[...]
    softplus = jnp.maximum(y, 0.0) + jnp.log(1.0 + jnp.exp(-jnp.abs(y)))
    g = -jnp.exp(alog_ref[...]) * softplus
    g_hi, g_mid, g_lo = _split3(g)
    tril = tril_ref[...]
    triu = triu_ref[...]
    gc_f = _dot(tril, g_hi) + _dot(tril, g_mid) + _dot(tril, g_lo)
    gc_b = _dot(triu, g_hi) + _dot(triu, g_mid) + _dot(triu, g_lo)
    lane = lax.broadcasted_iota(jnp.int32, raw.shape, 1)
    gates = jnp.where(lane < 8, beta, jnp.where(lane < 12, gc_f, gc_b))
    gates_ref[0] = gates
    gatest_ref[0] = jnp.transpose(gates)[0:2 * SUBLANES, :]


def _conv_prep(pb, pc, ps, conv_b_w, conv_c_w, ones_bd, alog_row, dtb_row):
    bsz, seq, _ = pb.shape
    tl = min(512, seq)
    nblk8 = seq // SUBLANES
    r8 = tl // SUBLANES
    main = lambda b, i: (b, i, 0)
    prev = lambda b, i: (b, jnp.maximum(i * r8 - 1, 0), 0)
    nxt = lambda b, i: (b, jnp.minimum((i + 1) * r8, nblk8 - 1), 0)
    const2 = lambda b, i: (0, 0)
    idx = np.arange(tl)
    same_chunk = (idx[:, None] // DELTA_CHUNK) == (idx[None, :] // DELTA_CHUNK)
    tril = jnp.asarray(same_chunk & (idx[:, None] >= idx[None, :]), BF16)
    triu = jnp.asarray(same_chunk & (idx[:, None] <= idx[None, :]), BF16)
    dense = jax.ShapeDtypeStruct((bsz, seq, GROUP_WIDTH), F32)
    dense_spec = pl.BlockSpec((1, tl, GROUP_WIDTH), main)
    return pl.pallas_call(
        _conv_prep_kernel,
        out_shape=[jax.ShapeDtypeStruct((bsz, seq, GROUP_WIDTH), BF16), dense, dense, dense,
                   jax.ShapeDtypeStruct((bsz, seq, LANES), F32),
                   jax.ShapeDtypeStruct((bsz, 2 * SUBLANES, seq), F32)],
        grid=(bsz, seq // tl),
        in_specs=[pl.BlockSpec((1, tl, 768), main),
                  pl.BlockSpec((1, SUBLANES, 768), prev),
                  pl.BlockSpec((1, SUBLANES, 768), nxt),
                  pl.BlockSpec((1, tl, 1024), main),
                  pl.BlockSpec((1, SUBLANES, 1024), prev),
                  pl.BlockSpec((1, SUBLANES, 1024), nxt),
                  pl.BlockSpec((1, tl, LANES), main),
                  pl.BlockSpec((3, GROUP_WIDTH), const2),
                  pl.BlockSpec((3, 3 * GROUP_WIDTH), const2),
                  pl.BlockSpec((LANES, LANES), const2),
                  pl.BlockSpec((1, LANES), const2),
                  pl.BlockSpec((1, LANES), const2),
                  pl.BlockSpec((tl, tl), const2),
                  pl.BlockSpec((tl, tl), const2)],
        out_specs=[pl.BlockSpec((1, tl, GROUP_WIDTH), main), dense_spec, dense_spec, dense_spec,
                   pl.BlockSpec((1, tl, LANES), main),
                   pl.BlockSpec((1, 2 * SUBLANES, tl), lambda b, i: (b, 0, i))],
        compiler_params=_cparams(("arbitrary", "arbitrary"), big=True),
        name="conv_prep",
    )(pb, pb, pb, pc, pc, pc, ps, conv_b_w, conv_c_w, ones_bd, alog_row, dtb_row, tril, triu)


_TRI_BASE_LOG2 = 3
_TRI_NEWTON_STEPS = 1
DELTA_TILE = 4 * DELTA_CHUNK


def _pair_blockdiag(y):
    low = lax.broadcasted_iota(jnp.int32, y.shape, 1) < HEAD_DIM
    zero = jnp.zeros_like(y)
    return jnp.concatenate([jnp.where(low, y, zero), jnp.where(low, zero, y)], axis=0)


def _pair_dot3_many(a_list, b_list):
    n = a_list[0].shape[0]
    lhs = [jnp.concatenate(_split2(a), axis=0) for a in a_list]
    rhs = [jnp.concatenate([_pair_blockdiag(part) for part in _split2(b)], axis=1) for b in b_list]
    res = [_dot(l, rr) for l, rr in zip(lhs, rhs)]
    return [(x[0:n, 0:LANES] + x[n:2 * n, LANES:2 * LANES])
            + (x[0:n, LANES:2 * LANES] + x[n:2 * n, 0:LANES]) for x in res]


def _pair_dot1_many(a_list, b_list):
    return [_dot(a.astype(BF16), _pair_blockdiag(b.astype(BF16))) for a, b in zip(a_list, b_list)]


def _pair_tri_inverse_many(lmats, r, c):
    def same_block(log2_size):
        return lax.shift_right_logical(r, log2_size) == lax.shift_right_logical(c, log2_size)

    base = same_block(_TRI_BASE_LOG2)
    eye = jnp.where(r == c, 1.0, 0.0)
    a_list = [jnp.where(base, m, 0.0) for m in lmats]
    p_list = [eye - a for a in a_list]
    for _ in range(_TRI_BASE_LOG2 - 1):
        a_list = _pair_dot1_many(a_list, a_list)
        p_list = [p + pa for p, pa in zip(p_list, _pair_dot1_many(p_list, a_list))]
    for log2_size in range(_TRI_BASE_LOG2, int(np.log2(DELTA_CHUNK))):
        couple = jnp.logical_and(same_block(log2_size + 1),
                                 jnp.logical_not(same_block(log2_size)))
        c_list = [jnp.where(couple, m, 0.0) for m in lmats]
        pc_list = _pair_dot1_many(p_list, c_list)
        p_list = [p - pcp for p, pcp in zip(p_list, _pair_dot1_many(pc_list, p_list))]
    for _ in range(_TRI_NEWTON_STEPS):
        lx_list = _pair_dot3_many(lmats, p_list)
        res_list = [(eye - p) - lx for p, lx in zip(p_list, lx_list)]
        p_list = [p + pr for p, pr in zip(p_list, _pair_dot1_many(p_list, res_list))]
    return p_list


def _delta_prep_kernel(q_ref, k_ref, v_ref, gates_ref, gt_ref, *out_refs):
    cs = DELTA_CHUNK
    nc = q_ref.shape[1] // cs
    n_slabs = q_ref.shape[2] // LANES
    r = lax.broadcasted_iota(jnp.int32, (cs, LANES), 0)
    lane = lax.broadcasted_iota(jnp.int32, (cs, LANES), 1)
    c = jnp.bitwise_and(lane, HEAD_DIM - 1)
    low = lane < HEAD_DIM
    er = lax.broadcasted_iota(jnp.int32, (LANES, LANES), 0)
    ec = lax.broadcasted_iota(jnp.int32, (LANES, LANES), 1)
    eye_bf = jnp.where(er == ec, 1.0, 0.0).astype(BF16)
    incl = (r >= c, r <= c)
    strict = (r > c, r < c)

    gates = gates_ref[0]
    gt = gt_ref[0]

    pairs = [(ci, s) for ci in range(nc) for s in range(n_slabs)]
    probs = [(ci, s, d) for d in range(2) for ci in range(nc) for s in range(n_slabs)]

    def tile(ref, ci, s):
        return ref[0, ci * cs:(ci + 1) * cs, s * LANES:(s + 1) * LANES]

    q = {p: tile(q_ref, *p) for p in pairs}
    k = {p: tile(k_ref, *p) for p in pairs}
    v = {p: tile(v_ref, *p) for p in pairs}
    k_bf = {p: k[p].astype(BF16) for p in pairs}
    kq = {p: _dot_nt(jnp.concatenate([k_bf[p], q[p].astype(BF16)], axis=0), _pair_blockdiag(k_bf[p]))
          for p in pairs}
    kk = {p: kq[p][0:cs] for p in pairs}
    qk = {p: kq[p][cs:2 * cs] for p in pairs}

    beta, gc, decay = {}, {}, {}
    for (ci, s, d) in probs:
        rows = slice(ci * cs, (ci + 1) * cs)
        col0 = 4 * d + 2 * s
        beta[ci, s, d] = jnp.where(low, gates[rows, col0:col0 + 1], gates[rows, col0 + 1:col0 + 2])
        gc_col = jnp.where(low, gates[rows, 8 + col0:9 + col0], gates[rows, 9 + col0:10 + col0])
        gc_row = jnp.concatenate([gt[8 + col0:9 + col0, rows], gt[9 + col0:10 + col0, rows]], axis=1)
        gc[ci, s, d] = gc_col
        decay[ci, s, d] = jnp.exp(jnp.where(incl[d], gc_col - gc_row, -jnp.inf))

    lmats = [jnp.where(strict[d], beta[ci, s, d] * kk[ci, s] * decay[ci, s, d], 0.0)
             for (ci, s, d) in probs]
    t_inv = [t.astype(BF16) for t in _pair_tri_inverse_many(lmats, r, c)]

    e = {p: jnp.exp(gc[p]) for p in probs}
    gc_last = {(ci, s, d): (gc[ci, s, d][0:1] if d == 1 else gc[ci, s, d][cs - 1:cs])
               for (ci, s, d) in probs}
    uw = {p: _dot(t, jnp.concatenate(
        [_pair_blockdiag((v[p[0], p[1]] * beta[p]).astype(BF16)),
         _pair_blockdiag((k[p[0], p[1]] * (beta[p] * e[p])).astype(BF16))], axis=1))
        for p, t in zip(probs, t_inv)}
    u = {p: uw[p][:, 0:LANES] for p in probs}
    w = {p: uw[p][:, LANES:2 * LANES] for p in probs}
    kdec_t = {p: _dot_nt(eye_bf, (k[p[0], p[1]] * jnp.exp(gc_last[p] - gc[p])).astype(BF16))
              for p in probs}

    for d in range(2):
        u_ref, w_ref, qg_ref, aqk_ref, kdt_ref, gl_ref = out_refs[6 * d:6 * d + 6]
        for (ci, s) in pairs:
            rows = slice(ci * cs, (ci + 1) * cs)
            lanes = slice(s * LANES, (s + 1) * LANES)
            u_ref[0, rows, lanes] = u[ci, s, d]
            w_ref[0, rows, lanes] = w[ci, s, d].astype(BF16)
            qg_ref[0, rows, lanes] = (q[ci, s] * e[ci, s, d]).astype(BF16)
            aqk_ref[0, rows, lanes] = (qk[ci, s] * decay[ci, s, d]).astype(BF16)
            gl_ref[0, 0, ci:ci + 1, lanes] = jnp.exp(gc_last[ci, s, d])
        for s in range(n_slabs):
            kdt_ref[0, 0, s * LANES:(s + 1) * LANES, :] = jnp.concatenate(
                [kdec_t[ci, s, d] for ci in range(nc)], axis=1).astype(BF16)
        gl_ref[0, 0, nc:SUBLANES, :] = jnp.zeros((SUBLANES - nc, gl_ref.shape[3]), F32)


def _delta_prep(q, k, v, gates, gates_t):
    bsz, seq, width = q.shape
    ta = min(DELTA_TILE, seq)
    nt = seq // ta
    main = lambda b, i: (b, i, 0)
    per_dir_shapes = [jax.ShapeDtypeStruct((bsz, seq, width), F32),
                      jax.ShapeDtypeStruct((bsz, seq, width), BF16),
                      jax.ShapeDtypeStruct((bsz, seq, width), BF16),
                      jax.ShapeDtypeStruct((bsz, seq, width), BF16),
                      jax.ShapeDtypeStruct((bsz, nt, width, ta), BF16),
                      jax.ShapeDtypeStruct((bsz, nt, SUBLANES, width), F32)]
    per_dir_specs = [pl.BlockSpec((1, ta, width), main)] * 4 + [
        pl.BlockSpec((1, 1, width, ta), lambda b, i: (b, i, 0, 0)),
        pl.BlockSpec((1, 1, SUBLANES, width), lambda b, i: (b, i, 0, 0))]
    return pl.pallas_call(
        _delta_prep_kernel,
        out_shape=per_dir_shapes * 2,
        grid=(bsz, nt),
        in_specs=[pl.BlockSpec((1, ta, width), main)] * 3 + [
            pl.BlockSpec((1, ta, LANES), main),
            pl.BlockSpec((1, 2 * SUBLANES, ta), lambda b, i: (b, 0, i))],
        out_specs=per_dir_specs * 2,
        compiler_params=_cparams(("arbitrary", "arbitrary"), big=True),
        name="delta_prep",
    )(q, k, v, gates, gates_t)


DELTA_SCAN_BATCH = 4


def _delta_scan_kernel(*refs):
    ins = (refs[0:6], refs[6:12])
    s0_ref = refs[12]
    outs = refs[13:15]
    send_ref, s_scr = refs[15:17]
    t = pl.program_id(1)
    cs = DELTA_CHUNK
    hd = HEAD_DIM
    nb, ta, width = ins[0][0].shape
    nc = ta // cs
    heads = width // hd
    rr = lax.broadcasted_iota(jnp.int32, (width, width), 0)
    cc = lax.broadcasted_iota(jnp.int32, (width, width), 1)
    same_head = lax.shift_right_logical(rr, 6) == lax.shift_right_logical(cc, 6)
    lane_head = lax.shift_right_logical(lax.broadcasted_iota(jnp.int32, (cs, width), 1), 6)
    zeros = jnp.zeros((cs, width), BF16)

    @pl.when(t == 0)
    def _():
        s_scr[...] = s0_ref[...]

    chains = [(bi, d) for bi in range(nb) for d in range(2)]
    for step in range(nc):
        chunk = (step, nc - 1 - step)
        rows = [slice(chunk[d] * cs, (chunk[d] + 1) * cs) for d in range(2)]
        s_prev = {(bi, d): s_scr[bi, d] for (bi, d) in chains}
        lhs = {(bi, d): jnp.concatenate([ins[d][1][bi, rows[d], :], ins[d][2][bi, rows[d], :]], axis=0)
               for (bi, d) in chains}
        res = {ch: _dot(lhs[ch], s_prev[ch].astype(BF16)) for ch in chains}
        v_new = {(bi, d): (ins[d][0][bi, rows[d], :] - res[bi, d][0:cs]).astype(BF16)
                 for (bi, d) in chains}
        v_bd = {ch: jnp.concatenate([jnp.where(lane_head == h, v_new[ch], zeros)
                                     for h in range(heads)], axis=0) for ch in chains}
        v_tile = {(bi, d): jnp.concatenate(
            [v_new[bi, d] if ci == chunk[d] else zeros for ci in range(nc)], axis=0)
            for (bi, d) in chains}
        o = {(bi, d): res[bi, d][cs:2 * cs] + _dot(ins[d][3][bi, rows[d], :], v_bd[bi, d])
             for (bi, d) in chains}
        upd = {(bi, d): _dot(ins[d][4][bi, 0], v_tile[bi, d]) for (bi, d) in chains}
        for (bi, d) in chains:
            gl = ins[d][5][bi, 0, chunk[d]:chunk[d] + 1, :]
            s_scr[bi, d] = s_prev[bi, d] * gl + jnp.where(same_head, upd[bi, d], 0.0)
            outs[d][bi, rows[d], :] = o[bi, d]

    @pl.when(t == pl.num_programs(1) - 1)
    def _():
        send_ref[...] = s_scr[...]


def _delta_scan(prep, s0_bd):
    u_f = prep[0]
    bsz, seq, width = u_f.shape
    ta = min(DELTA_TILE, seq)
    nt = seq // ta
    nb = DELTA_SCAN_BATCH if bsz % DELTA_SCAN_BATCH == 0 else 1

    def specs(index3, index4):
        return [pl.BlockSpec((nb, ta, width), index3)] * 4 + [
            pl.BlockSpec((nb, 1, width, ta), index4),
            pl.BlockSpec((nb, 1, SUBLANES, width), index4)]

    fwd3 = lambda b, t: (b, t, 0)
    fwd4 = lambda b, t: (b, t, 0, 0)
    bwd3 = lambda b, t: (b, nt - 1 - t, 0)
    bwd4 = lambda b, t: (b, nt - 1 - t, 0, 0)
    state_spec = pl.BlockSpec((nb, 2, width, width), lambda b, t: (b, 0, 0, 0))
    return pl.pallas_call(
        _delta_scan_kernel,
        out_shape=[jax.ShapeDtypeStruct((bsz, seq, width), F32),
                   jax.ShapeDtypeStruct((bsz, seq, width), F32),
                   jax.ShapeDtypeStruct((bsz, 2, width, width), F32)],
        grid=(bsz // nb, nt),
        in_specs=specs(fwd3, fwd4) + specs(bwd3, bwd4) + [state_spec],
        out_specs=[pl.BlockSpec((nb, ta, width), fwd3), pl.BlockSpec((nb, ta, width), bwd3),
                   state_spec],
        scratch_shapes=[pltpu.VMEM((nb, 2, width, width), F32)],
        compiler_params=_cparams(("arbitrary", "arbitrary")),
        name="delta_scan",
    )(*prep, s0_bd)


def _deltanet(q, k, v, gates, gates_t, s0):
    bsz = q.shape[0]
    heads, hd = N_HEADS_C, HEAD_DIM
    eye_h = jnp.eye(heads, dtype=F32)
    s0_bd = jnp.einsum('bdhij,hg->bdhigj', s0.astype(F32), eye_h).reshape(
        bsz, 2, heads * hd, heads * hd)
    prep = _delta_prep(q, k, v, gates, gates_t)
    o_f, o_b, s_bd = _delta_scan(prep, s0_bd)
    s_blocks = s_bd.reshape(bsz, 2, heads, hd, heads, hd)
    s_end = jnp.stack([s_blocks[:, :, h, :, h, :] for h in range(heads)], axis=2)
    return o_f, o_b, s_end


def _fourier_pos_kernel(w_ref, a1_ref, b1_ref, a0_ref, b0_ref, o_ref, c_scr, s_scr, *, scale):
    i = pl.program_id(1)
    tm = c_scr.shape[0]
    sub = a0_ref.shape[0]
    a0 = a0_ref[...]
    b0 = b0_ref[...]
    for rblk in range(tm // sub):
        j1 = i * (tm // sub) + rblk
        a1 = a1_ref[pl.ds(j1, 1), :]
        b1 = b1_ref[pl.ds(j1, 1), :]
        c_scr[rblk * sub:(rblk + 1) * sub, :] = (a1 * a0 - b1 * b0).astype(BF16)
        s_scr[rblk * sub:(rblk + 1) * sub, :] = (b1 * a0 + a1 * b0).astype(BF16)
    res = (_dot(c_scr[...], w_ref[0]) - _dot(s_scr[...], w_ref[1])) * scale
    gw = o_ref.shape[2]
    for b in range(o_ref.shape[0]):
        o_ref[b] = res[:, b * gw:(b + 1) * gw].astype(BF16)


def _fourier_positions(w, tabs, bsz, gw):
    _, seq, ncol = w.shape
    a1, b1, a0, b0 = tabs
    tm = min(512, seq)
    nsplit = 2 if (bsz % 2 == 0 and seq > 1024) else 1
    bpart = bsz // nsplit
    const2 = lambda p, i: (0, 0)
    single = pl.Buffered(1)
    scale = 1.0 / np.sqrt(float(seq) * HEAD_DIM)
    return pl.pallas_call(
        functools.partial(_fourier_pos_kernel, scale=scale),
        out_shape=jax.ShapeDtypeStruct((bsz, seq, gw), BF16),
        grid=(nsplit, seq // tm),
        in_specs=[pl.BlockSpec((2, seq, bpart * gw), lambda p, i: (0, 0, p), pipeline_mode=single),
                  pl.BlockSpec(a1.shape, const2, pipeline_mode=single),
                  pl.BlockSpec(b1.shape, const2, pipeline_mode=single),
                  pl.BlockSpec(a0.shape, const2, pipeline_mode=single),
                  pl.BlockSpec(b0.shape, const2, pipeline_mode=single)],
        out_specs=pl.BlockSpec((bpart, tm, gw), lambda p, i: (p, i, 0)),
        scratch_shapes=[pltpu.VMEM((tm, seq), BF16), pltpu.VMEM((tm, seq), BF16)],
        compiler_params=_cparams(("arbitrary", "arbitrary"), big=True),
        name="fourier_positions",
    )(w, a1, b1, a0, b0)


def _rope_tables(seq):
    rows = seq // GRID_W
    r, cl = jnp.meshgrid(jnp.arange(rows), jnp.arange(GRID_W), indexing='ij')
    pos_r = r.reshape(-1).astype(F32)
    pos_c = cl.reshape(-1).astype(F32)
    quarter = HEAD_DIM // 4
    inv_freq = ROPE_THETA ** (-jnp.arange(quarter, dtype=F32) / quarter)
    cos_parts, sin_parts = [], []
    for pos in (pos_r, pos_c):
        ang = pos[:, None] * inv_freq[None, :]
        cos_parts += [jnp.cos(ang), jnp.cos(ang)]
        sin_parts += [-jnp.sin(ang), jnp.sin(ang)]
    cos = jnp.concatenate(cos_parts, axis=-1)
    sin = jnp.concatenate(sin_parts, axis=-1)
    return jnp.tile(cos, (1, 2)), jnp.tile(sin, (1, 2))


def _trig_table(num_rows, row_mult, seq):
    j = np.arange(num_rows, dtype=np.int64)[:, None] * row_mult
    k = np.arange(seq, dtype=np.int64)[None, :]
    ang = 2.0 * np.pi * ((j * k) % seq).astype(np.float64) / seq
    return jnp.asarray(np.cos(ang), F32), jnp.asarray(np.sin(ang), F32)


def _fourier_tables(seq):
    sub = 64
    a1, b1 = _trig_table(seq // sub, sub, seq)
    a0, b0 = _trig_table(sub, 1, seq)
    idx = np.arange(GROUP_WIDTH)
    same = (idx[:, None] // HEAD_DIM) == (idx[None, :] // HEAD_DIM)
    ang = 2.0 * np.pi * ((idx[:, None] % HEAD_DIM) * (idx[None, :] % HEAD_DIM) % HEAD_DIM) / HEAD_DIM
    c_bd = jnp.asarray(np.where(same, np.cos(ang), 0.0), F32)
    s_bd = jnp.asarray(np.where(same, np.sin(ang), 0.0), F32)
    return (a1, b1, a0, b0), c_bd, s_bd


def _to_lane_dense(t):
    b, hh, l, d = t.shape
    return jnp.swapaxes(t, 1, 2).reshape(b, l, hh * d)


def _to_head_major(t, heads):
    b, l, w = t.shape
    return jnp.swapaxes(t.reshape(b, l, heads, w // heads), 1, 2)


def _swap_halves(t):
    return jnp.concatenate([t[..., HEAD_DIM:], t[..., :HEAD_DIM]], axis=-1)


def _trunk_layer(x, mod, wts, consts, ctx, final_g):
    bsz, seq, _ = x.shape
    x1 = _ffn(x, mod, wts['norm_ffn1'], wts['ffn1_w1'], wts['ffn1_w3'], wts['ffn1_w2'], which=0)
    rope_tabs = consts['rope'] if ctx is not None else None
    proj = _inproj(x1, mod, wts['norm_mix'], wts['w_in'], consts['ones_bd'], wts['gq'], wts['gk'],
                   consts['c_bd'], consts['s_bd'], rope_tabs, ctx is None)
    q, ka, kb, va, vb, pb, pc, ps, wd = proj[:9]

    k_cache = v_cache = None
    if ctx is None:
        k_cache = _to_head_major(proj[9], KV_HEADS_A)
        v_cache = _to_head_major(proj[10], KV_HEADS_A)
        s0 = jnp.zeros((bsz, 2, N_HEADS_C, HEAD_DIM, HEAD_DIM), F32)
    else:
        k_ctx, v_ctx, s0 = ctx
        k_past = _to_lane_dense(k_ctx).astype(BF16)
        v_past = _to_lane_dense(v_ctx).astype(BF16)
        ka = jnp.concatenate([k_past, ka], axis=1)
        kb = jnp.concatenate([_swap_halves(k_past), kb], axis=1)
        ones = jnp.ones_like(v_past)
        va = jnp.concatenate([jnp.concatenate([v_past, ones], axis=-1), va], axis=1)
        vb = jnp.concatenate([jnp.concatenate([_swap_halves(v_past), ones], axis=-1), vb], axis=1)
    o_a = _attention(q, ka, kb, va, vb)

    o_b, qc, kc, vc, gates, gates_t = _conv_prep(pb, pc, ps, wts['conv_b_w'], wts['conv_c_w'],
                                                 consts['ones_bd'], wts['alog_row'], wts['dtb_row'])
    d_f, d_b, s_end = _deltanet(qc, kc, vc, gates, gates_t, s0)

    o_d = _fourier_positions(wd, consts['fourier'], bsz, GROUP_WIDTH)

    x3 = _ffn(x1, mod, wts['norm_ffn2'], wts['ffn2_w1'], wts['ffn2_w3'], wts['ffn2_w2'], which=2,
              mix=(o_a, o_b, (d_f, d_b, pc, consts['ones_bd'], wts['delta_norm']), o_d),
              w_out=wts['w_out'], final_g=final_g)
    return x3, k_cache, v_cache, s_end


def _lane_row(values, start):
    row = jnp.zeros((1, LANES), F32)
    return row.at[0, start:start + values.shape[0]].set(values.astype(F32))


def kernel(x_prompt, x_sample, c, cache_k, cache_v, state_delta, c_ctx, mod_w, mod_b, norm_ffn1, norm_mix, norm_ffn2, ffn1_w1, ffn1_w3, ffn1_w2, ffn2_w1, ffn2_w3, ffn2_w2, w_in, w_out, q_norm, k_norm, conv_b_w, conv_c_w, delta_a_log, delta_dt_bias, delta_norm, final_norm):
    depth = mod_w.shape[0]
    d_model = x_prompt.shape[-1]
    dec_b = x_sample.shape[0]
    assert w_in.shape[-1] == PROJ_MAIN + PROJ_SMALL + GROUP_WIDTH

    n_cond = 1 + dec_b
    rows = -(-n_cond // SUBLANES) * SUBLANES
    cond = jnp.zeros((rows, d_model), F32).at[0].set(c_ctx).at[1:n_cond].set(c)

    idx = np.arange(LANES)
    ones_bd = jnp.asarray((idx[:, None] // HEAD_DIM) == (idx[None, :] // HEAD_DIM), BF16)
    consts_p = {'ones_bd': ones_bd}
    consts_s = {'ones_bd': ones_bd, 'rope': _rope_tables(x_sample.shape[1])}
    consts_p['fourier'], consts_p['c_bd'], consts_p['s_bd'] = _fourier_tables(x_prompt.shape[1])
    consts_s['fourier'], consts_s['c_bd'], consts_s['s_bd'] = _fourier_tables(x_sample.shape[1])

    yp, ys = x_prompt, x_sample
    k_list, v_list, s_list = [], [], []
    for l in range(depth):
        w_in_l = w_in[l]
        small = w_in_l[:, PROJ_MAIN:PROJ_MAIN + PROJ_SMALL]
        w_in_r = jnp.concatenate(
            [w_in_l[:, :PROJ_MAIN], w_in_l[:, PROJ_MAIN + PROJ_SMALL:], small,
             jnp.zeros((d_model, LANES - PROJ_SMALL), F32)], axis=1).astype(BF16)
        wts = {
            'norm_ffn1': norm_ffn1[l], 'norm_mix': norm_mix[l], 'norm_ffn2': norm_ffn2[l],
            'ffn1_w1': ffn1_w1[l].astype(BF16), 'ffn1_w3': ffn1_w3[l].astype(BF16),
            'ffn1_w2': ffn1_w2[l].astype(BF16),
            'ffn2_w1': ffn2_w1[l].astype(BF16), 'ffn2_w3': ffn2_w3[l].astype(BF16),
            'ffn2_w2': ffn2_w2[l].astype(BF16),
            'w_in': w_in_r, 'w_out': w_out[l].astype(BF16),
            'gq': jnp.tile(q_norm[l], 2).reshape(1, LANES),
            'gk': jnp.tile(k_norm[l], 2).reshape(1, LANES),
            'conv_b_w': conv_b_w[l], 'conv_c_w': conv_c_w[l],
            'alog_row': _lane_row(delta_a_log[l].reshape(-1), 8),
            'dtb_row': _lane_row(delta_dt_bias[l].reshape(-1), 8),
            'delta_norm': delta_norm[l],
        }
        mod = _modulation(cond, mod_w[l], mod_b[l]).reshape(rows, N_MOD, d_model)
        fin = final_norm if l == depth - 1 else None
        yp, k_l, v_l, s_l = _trunk_layer(yp, mod[0:1], wts, consts_p, None, fin)
        k_list.append(k_l)
        v_list.append(v_l)
        s_list.append(s_l)
        ys, _, _, _ = _trunk_layer(ys, mod[1:n_cond], wts, consts_s,
                                   (cache_k[:, l], cache_v[:, l], state_delta[:, l]), fin)
    return (yp, ys, jnp.stack(k_list, axis=1), jnp.stack(v_list, axis=1),
            jnp.stack(s_list, axis=1))
```

```python
import functools

import jax
import jax.numpy as jnp
import numpy as np
from jax import lax
from jax.experimental import pallas as pl
from jax.experimental.pallas import tpu as pltpu

F32 = jnp.float32
BF16 = jnp.bfloat16

HEAD_DIM = 64
N_HEADS_A = 4
KV_HEADS_A = 2
N_HEADS_C = 4
GROUP_WIDTH = 256
GRID_W = 64
DELTA_CHUNK = 64
ROPE_THETA = 10000.0
N_MOD = 9
EPS = 1e-6
PROJ_MAIN = 2304
PROJ_SMALL = 16
LANES = 128
SUBLANES = 8
VMEM_LIMIT_BYTES = 56 * 1024 * 1024


def _cparams(sem, big=False):
    return pltpu.CompilerParams(
        dimension_semantics=sem,
        vmem_limit_bytes=VMEM_LIMIT_BYTES if big else None)


def _dot(a, b):
    return jnp.dot(a, b, preferred_element_type=F32)


def _dot_nt(a, b):
    return lax.dot_general(a, b, (((1,), (1,)), ((), ())), preferred_element_type=F32)


def _split2(x):
    hi = x.astype(BF16)
    lo = (x - hi.astype(F32)).astype(BF16)
    return hi, lo


def _split3(x):
    hi = x.astype(BF16)
    r = x - hi.astype(F32)
    mid = r.astype(BF16)
    lo = (r - mid.astype(F32)).astype(BF16)
    return hi, mid, lo


def _rms(x, g):
    return x * lax.rsqrt(jnp.mean(x * x, axis=-1, keepdims=True) + EPS) * g


def _silu(x):
    return x * jax.nn.sigmoid(x)


def _group_sumsq(x, ones_bd):
    hi, lo = _split2(x * x)
    return _dot(hi, ones_bd) + _dot(lo, ones_bd)


def _mod_kernel(c_ref, w_ref, b_ref, o_ref):
    s = _silu(c_ref[...]).astype(BF16)
    o_ref[...] = _dot(s, w_ref[0].astype(BF16)) + b_ref[0]


def _modulation(cond, mod_w, mod_b, layer):
    rows, d = cond.shape
    depth, _, n = mod_w.shape
    tn = d
    return pl.pallas_call(
        _mod_kernel,
        out_shape=jax.ShapeDtypeStruct((rows, n), F32),
        grid=(n // tn,),
        in_specs=[pl.BlockSpec((rows, d), lambda j: (0, 0)),
                  pl.BlockSpec((1, d, tn), lambda j: (layer, 0, j)),
                  pl.BlockSpec((1, 1, tn), lambda j: (layer, 0, j))],
        out_specs=pl.BlockSpec((rows, tn), lambda j: (0, j)),
        compiler_params=_cparams(("arbitrary",)),
        name="modulation",
    )(cond, mod_w, mod_b.reshape(depth, 1, n))


def _ffn_chunks(f):
    step = 1024
    return [(s, min(s + step, f)) for s in range(0, f, step)]


def _gated_delta_out(of_ref, ob_ref, z_ref, ones_ref, g_ref):
    ones_bd = ones_ref[...]
    parts = []
    for s in range(of_ref.shape[2] // LANES):
        lanes = slice(s * LANES, (s + 1) * LANES)
        o = of_ref[0, :, lanes] + ob_ref[0, :, lanes]
        ms = _group_sumsq(o, ones_bd) * (1.0 / HEAD_DIM)
        y = o * lax.rsqrt(ms + EPS) * g_ref[...]
        parts.append((y * _silu(z_ref[0, :, lanes])).astype(BF16))
    return jnp.concatenate(parts, axis=-1)


def _ffn_kernel(*refs, which, n_mix, final):
    x_ref, mod_ref, g_ref, w1_ref, w3_ref, w2_ref = refs[:6]
    pos = 6
    mix_vals = []
    wout_ref = None
    if n_mix:
        oa_ref, ob_ref, dof_ref, dob_ref, z_ref, ones_ref, dg_ref, od_ref, wout_ref = refs[pos:pos + 9]
        pos += 9
        mix_vals = [oa_ref[0], ob_ref[0],
                    _gated_delta_out(dof_ref, dob_ref, z_ref, ones_ref, dg_ref), od_ref[0]]
    gf_ref = None
    if final:
        gf_ref = refs[pos]
        pos += 1
    o_ref = refs[pos]

    x = x_ref[0]
    mod = mod_ref[0]
    if n_mix:
        acc = None
        for i, m in enumerate(mix_vals):
            w = m.shape[-1]
            part = _dot(m, wout_ref[i * w:(i + 1) * w, :])
            acc = part if acc is None else acc + part
        x = x + mod[5:6] * acc
    sh = mod[3 * which:3 * which + 1]
    sc = mod[3 * which + 1:3 * which + 2]
    gt = mod[3 * which + 2:3 * which + 3]
    h = (_rms(x, g_ref[...]) * (1.0 + sc) + sh).astype(BF16)
    out = None
    for s, e in _ffn_chunks(w1_ref.shape[1]):
        a = _dot(h, w1_ref[:, s:e])
        b = _dot(h, w3_ref[:, s:e])
        act = (_silu(a) * b).astype(BF16)
        part = _dot(act, w2_ref[s:e, :])
        out = part if out is None else out + part
    xn = x + 0.5 * gt * out
    if final:
        xn = _rms(xn, gf_ref[...])
    o_ref[0] = xn


def _ffn(x, mod, g, w1, w3, w2, *, which, mix=None, w_out=None, final_g=None):
    bsz, seq, d = x.shape
    f = w1.shape[1]
    tm = min(512, seq)
    per_batch_mod = mod.shape[0] > 1
    mod_map = (lambda b, i: (b, 0, 0)) if per_batch_mod else (lambda b, i: (0, 0, 0))
    const2 = lambda b, i: (0, 0)
    single = pl.Buffered(1)
    args = [x, mod, g.reshape(1, d), w1, w3, w2]
    specs = [pl.BlockSpec((1, tm, d), lambda b, i: (b, i, 0)),
             pl.BlockSpec((1, N_MOD, d), mod_map),
             pl.BlockSpec((1, d), const2),
             pl.BlockSpec((d, f), const2, pipeline_mode=single),
             pl.BlockSpec((d, f), const2, pipeline_mode=single),
             pl.BlockSpec((f, d), const2, pipeline_mode=single)]
    n_mix = 0
    if mix is not None:
        o_a, o_b, (d_f, d_b, pc, ones_bd, delta_g), o_d = mix
        n_mix = 4
        tok = lambda b, i: (b, i, 0)
        gw = o_a.shape[-1]
        z_block = pc.shape[2] // gw - 1
        args += [o_a, o_b, d_f, d_b, pc, ones_bd,
                 jnp.tile(delta_g, LANES // HEAD_DIM).reshape(1, LANES), o_d, w_out]
        specs += [pl.BlockSpec((1, tm, gw), tok), pl.BlockSpec((1, tm, gw), tok),
                  pl.BlockSpec((1, tm, gw), tok), pl.BlockSpec((1, tm, gw), tok),
                  pl.BlockSpec((1, tm, gw), lambda b, i: (b, i, z_block)),
                  pl.BlockSpec((LANES, LANES), const2), pl.BlockSpec((1, LANES), const2),
                  pl.BlockSpec((1, tm, gw), tok),
                  pl.BlockSpec(w_out.shape, const2, pipeline_mode=single)]
    if final_g is not None:
        args.append(final_g.reshape(1, d))
        specs.append(pl.BlockSpec((1, d), const2))
    return pl.pallas_call(
        functools.partial(_ffn_kernel, which=which, n_mix=n_mix, final=final_g is not None),
        out_shape=jax.ShapeDtypeStruct((bsz, seq, d), F32),
        grid=(bsz, seq // tm),
        in_specs=specs,
        out_specs=pl.BlockSpec((1, tm, d), lambda b, i: (b, i, 0)),
        compiler_params=_cparams(("arbitrary", "arbitrary"), big=True),
        name="ffn",
    )(*args)


_PROJ_A = (0, 512)
_PROJ_B = (512, 1280)
_PROJ_C = (1280, 2304)
_PROJ_D = (2304, 2560)
_PROJ_S = (2560, 2688)


def _rope(x, cos, sin_signed):
    lane = lax.broadcasted_iota(jnp.int32, x.shape, 1)
    up = pltpu.roll(x, LANES - 16, axis=1)
    down = pltpu.roll(x, 16, axis=1)
    partner = jnp.where((lane % 32) < 16, up, down)
    return x * cos + partner * sin_signed


def _inproj_kernel(*refs, rope, want_cache):
    x_ref, mod_ref, g_ref, w_ref, ones_ref, gq_ref, gk_ref, cbd_ref, sbd_ref = refs[:9]
    pos = 9
    if rope:
        cos_ref, sin_ref = refs[pos:pos + 2]
        pos += 2
    q_ref, ka_ref, kb_ref, va_ref, vb_ref, pb_ref, pc_ref, ps_ref, wd_ref = refs[pos:pos + 9]
    pos += 9
    mod = mod_ref[0]
    h = (_rms(x_ref[0], g_ref[...]) * (1.0 + mod[4:5]) + mod[3:4]).astype(BF16)

    def proj(cols):
        return _dot(h, w_ref[:, cols[0]:cols[1]])

    pb_ref[0] = proj(_PROJ_B)
    pc_ref[0] = proj(_PROJ_C)
    ps_ref[0] = proj(_PROJ_S)

    xd = proj(_PROJ_D).astype(BF16)
    wd_ref[0] = _dot(xd, cbd_ref[...].astype(BF16)).astype(BF16)
    wd_ref[1] = _dot(xd, sbd_ref[...].astype(BF16)).astype(BF16)

    pa = proj(_PROJ_A)
    ones_bd = ones_ref[...]

    def normed(x, g):
        ss = _group_sumsq(x, ones_bd)
        return x * lax.rsqrt(ss * (1.0 / HEAD_DIM) + EPS) * g

    q0 = normed(pa[:, 0:128], gq_ref[...])
    q1 = normed(pa[:, 128:256], gq_ref[...])
    k = normed(pa[:, 256:384], gk_ref[...])
    v = pa[:, 384:512]
    if want_cache:
        refs[pos][0] = k
        refs[pos + 1][0] = v
    if rope:
        cos = cos_ref[...]
        sin = sin_ref[...]
        q0 = _rope(q0, cos, sin)
        q1 = _rope(q1, cos, sin)
        k = _rope(k, cos, sin)
    scale = HEAD_DIM ** -0.5
    q_ref[0, :, 0:128] = (q0 * scale).astype(BF16)
    q_ref[0, :, 128:256] = (q1 * scale).astype(BF16)
    ka_ref[0] = k.astype(BF16)
    kb_ref[0] = pltpu.roll(k, HEAD_DIM, axis=1).astype(BF16)
    ones = jnp.ones(v.shape, BF16)
    va_ref[0, :, 0:LANES] = v.astype(BF16)
    va_ref[0, :, LANES:2 * LANES] = ones
    vb_ref[0, :, 0:LANES] = pltpu.roll(v, HEAD_DIM, axis=1).astype(BF16)
    vb_ref[0, :, LANES:2 * LANES] = ones


def _inproj(x, mod, g, w_in_r, ones_bd, gq, gk, c_bd, s_bd, rope_tabs, want_cache):
    bsz, seq, d = x.shape
    tm = min(512, seq)
    per_batch_mod = mod.shape[0] > 1
    mod_map = (lambda b, i: (b, 0, 0)) if per_batch_mod else (lambda b, i: (0, 0, 0))
    const2 = lambda b, i: (0, 0)
    tok = lambda b, i: (b, i, 0)
    rope = rope_tabs is not None
    gw = GROUP_WIDTH
    args = [x, mod, g.reshape(1, d), w_in_r, ones_bd, gq, gk, c_bd, s_bd]
    specs = [pl.BlockSpec((1, tm, d), tok),
             pl.BlockSpec((1, N_MOD, d), mod_map),
             pl.BlockSpec((1, d), const2),
             pl.BlockSpec(w_in_r.shape, const2, pipeline_mode=pl.Buffered(1)),
             pl.BlockSpec((LANES, LANES), const2),
             pl.BlockSpec((1, LANES), const2),
             pl.BlockSpec((1, LANES), const2),
             pl.BlockSpec((gw, gw), const2),
             pl.BlockSpec((gw, gw), const2)]
    if rope:
        args += list(rope_tabs)
        specs += [pl.BlockSpec((tm, LANES), lambda b, i: (i, 0))] * 2

    def tok_out(width, dtype):
        return jax.ShapeDtypeStruct((bsz, seq, width), dtype), pl.BlockSpec((1, tm, width), tok)

    outs = [tok_out(256, BF16),
            tok_out(LANES, BF16), tok_out(LANES, BF16),
            tok_out(2 * LANES, BF16), tok_out(2 * LANES, BF16),
            tok_out(_PROJ_B[1] - _PROJ_B[0], F32),
            tok_out(_PROJ_C[1] - _PROJ_C[0], F32),
            tok_out(_PROJ_S[1] - _PROJ_S[0], F32),
            (jax.ShapeDtypeStruct((2, seq, bsz * gw), BF16),
             pl.BlockSpec((2, tm, gw), lambda b, i: (0, i, b)))]
    if want_cache:
        outs += [tok_out(LANES, F32), tok_out(LANES, F32)]
    return pl.pallas_call(
        functools.partial(_inproj_kernel, rope=rope, want_cache=want_cache),
        out_shape=[o[0] for o in outs],
        grid=(bsz, seq // tm),
        in_specs=specs,
        out_specs=[o[1] for o in outs],
        compiler_params=_cparams(("arbitrary", "arbitrary"), big=True),
        name="inproj",
    )(*args)


def _attn_kernel(q_ref, ka_ref, kb_ref, va_ref, vb_ref, o_ref):
    tq = q_ref.shape[1]
    lane = lax.broadcasted_iota(jnp.int32, (tq, LANES), 1)
    low = lane < HEAD_DIM
    arrangement = ((ka_ref, va_ref), (kb_ref, vb_ref), (kb_ref, vb_ref), (ka_ref, va_ref))
    def scores(h):
        slab = q_ref[0, :, (h // 2) * LANES:(h // 2 + 1) * LANES]
        keep = low if h % 2 == 0 else jnp.logical_not(low)
        qh = jnp.where(keep, slab, jnp.zeros_like(slab))
        return _dot_nt(qh, arrangement[h][0][0])

    def weighted_values(h, s):
        m = jnp.max(s, axis=-1, keepdims=True)
        p = jnp.exp(s - m)
        pv = _dot(p.astype(BF16), arrangement[h][1][0])
        return pv[:, 0:LANES] / pv[:, LANES:LANES + 1]

    outs = []
    s_next = scores(0)
    for h in range(N_HEADS_A):
        s_cur = s_next
        if h + 1 < N_HEADS_A:
            s_next = scores(h + 1)
        outs.append(weighted_values(h, s_cur))
    o_ref[0, :, 0:128] = jnp.where(low, outs[0], outs[1]).astype(BF16)
    o_ref[0, :, 128:256] = jnp.where(low, outs[2], outs[3]).astype(BF16)


def _attention(q, ka, kb, va, vb):
    bsz, seq, _ = q.shape
    lk = ka.shape[1]
    tq = min(512, seq)
    k_spec = pl.BlockSpec((1, lk, LANES), lambda b, i: (b, 0, 0))
    v_spec = pl.BlockSpec((1, lk, 2 * LANES), lambda b, i: (b, 0, 0))
    return pl.pallas_call(
        _attn_kernel,
        out_shape=jax.ShapeDtypeStruct((bsz, seq, 256), BF16),
        grid=(bsz, seq // tq),
        in_specs=[pl.BlockSpec((1, tq, 256), lambda b, i: (b, i, 0)),
                  k_spec, k_spec, v_spec, v_spec],
        out_specs=pl.BlockSpec((1, tq, 256), lambda b, i: (b, i, 0)),
        compiler_params=_cparams(("arbitrary", "arbitrary"), big=True),
        name="attention",
    )(q, ka, kb, va, vb)


def _conv3(x, prev_row, next_row, w):
    rows = x.shape[0]
    ridx = lax.broadcasted_iota(jnp.int32, x.shape, 0)
    xm = jnp.where(ridx == 0, prev_row, pltpu.roll(x, 1, axis=0))
    xp = jnp.where(ridx == rows - 1, next_row, pltpu.roll(x, rows - 1, axis=0))
    return xm * w[0:1] + x * w[1:2] + xp * w[2:3]


def _conv_prep_kernel(pb_ref, pbp_ref, pbn_ref, pc_ref, pcp_ref, pcn_ref, ps_ref,
                      wb_ref, wc_ref, ones_ref, alog_ref, dtb_ref, tril_ref, triu_ref,
                      ob_ref, q_ref, k_ref, v_ref, gates_ref, gatest_ref):
    i = pl.program_id(1)
    has_prev = i > 0
    has_next = i < pl.num_programs(1) - 1
    gw = GROUP_WIDTH

    pb = pb_ref[0]
    u = pb[:, gw:2 * gw] * pb[:, 2 * gw:3 * gw]
    pbp = pbp_ref[0, SUBLANES - 1:SUBLANES, :]
    pbn = pbn_ref[0, 0:1, :]
    u_prev = jnp.where(has_prev, pbp[:, gw:2 * gw] * pbp[:, 2 * gw:3 * gw], 0.0)
    u_next = jnp.where(has_next, pbn[:, gw:2 * gw] * pbn[:, 2 * gw:3 * gw], 0.0)
    ob_ref[0] = (pb[:, 0:gw] * _conv3(u, u_prev, u_next, wb_ref[...])).astype(BF16)

    pc = pc_ref[0]
    x3 = pc[:, 0:3 * gw]
    x_prev = jnp.where(has_prev, pcp_ref[0, SUBLANES - 1:SUBLANES, 0:3 * gw], 0.0)
    x_next = jnp.where(has_next, pcn_ref[0, 0:1, 0:3 * gw], 0.0)
    qkv = _silu(_conv3(x3, x_prev, x_next, wc_ref[...]))
    ones_bd = ones_ref[...]

    def l2n(x):
        return x * lax.rsqrt(_group_sumsq(x, ones_bd) + EPS)

    q_scale = HEAD_DIM ** -0.5
    for s in range(2):
        q_ref[0, :, s * LANES:(s + 1) * LANES] = l2n(qkv[:, s * LANES:(s + 1) * LANES]) * q_scale
        k_ref[0, :, s * LANES:(s + 1) * LANES] = l2n(qkv[:, gw + s * LANES:gw + (s + 1) * LANES])
    v_ref[0] = qkv[:, 2 * gw:3 * gw]

    raw = ps_ref[0]
    beta = jax.nn.sigmoid(raw)
    y = raw + dtb_ref[...]
    softplus = jnp.maximum(y, 0.0) + jnp.log(1.0 + jnp.exp(-jnp.abs(y)))
    g = -jnp.exp(alog_ref[...]) * softplus
    g_hi, g_mid, g_lo = _split3(g)
    tril = tril_ref[...]
    triu = triu_ref[...]
    gc_f = _dot(tril, g_hi) + _dot(tril, g_mid) + _dot(tril, g_lo)
    gc_b = _dot(triu, g_hi) + _dot(triu, g_mid) + _dot(triu, g_lo)
    lane = lax.broadcasted_iota(jnp.int32, raw.shape, 1)
    gates = jnp.where(lane < 8, beta, jnp.where(lane < 12, gc_f, gc_b))
    gates_ref[0] = gates
    gatest_ref[0] = jnp.transpose(gates)[0:2 * SUBLANES, :]


def _conv_prep(pb, pc, ps, conv_b_w, conv_c_w, ones_bd, alog_row, dtb_row):
    bsz, seq, _ = pb.shape
    tl = min(512, seq)
    nblk8 = seq // SUBLANES
    r8 = tl // SUBLANES
    main = lambda b, i: (b, i, 0)
    prev = lambda b, i: (b, jnp.maximum(i * r8 - 1, 0), 0)
    nxt = lambda b, i: (b, jnp.minimum((i + 1) * r8, nblk8 - 1), 0)
    const2 = lambda b, i: (0, 0)
    idx = np.arange(tl)
    same_chunk = (idx[:, None] // DELTA_CHUNK) == (idx[None, :] // DELTA_CHUNK)
    tril = jnp.asarray(same_chunk & (idx[:, None] >= idx[None, :]), BF16)
    triu = jnp.asarray(same_chunk & (idx[:, None] <= idx[None, :]), BF16)
    dense = jax.ShapeDtypeStruct((bsz, seq, GROUP_WIDTH), F32)
    dense_spec = pl.BlockSpec((1, tl, GROUP_WIDTH), main)
    return pl.pallas_call(
        _conv_prep_kernel,
        out_shape=[jax.ShapeDtypeStruct((bsz, seq, GROUP_WIDTH), BF16), dense, dense, dense,
                   jax.ShapeDtypeStruct((bsz, seq, LANES), F32),
                   jax.ShapeDtypeStruct((bsz, 2 * SUBLANES, seq), F32)],
        grid=(bsz, seq // tl),
        in_specs=[pl.BlockSpec((1, tl, 768), main),
                  pl.BlockSpec((1, SUBLANES, 768), prev),
                  pl.BlockSpec((1, SUBLANES, 768), nxt),
                  pl.BlockSpec((1, tl, 1024), main),
                  pl.BlockSpec((1, SUBLANES, 1024), prev),
                  pl.BlockSpec((1, SUBLANES, 1024), nxt),
                  pl.BlockSpec((1, tl, LANES), main),
                  pl.BlockSpec((3, GROUP_WIDTH), const2),
                  pl.BlockSpec((3, 3 * GROUP_WIDTH), const2),
                  pl.BlockSpec((LANES, LANES), const2),
                  pl.BlockSpec((1, LANES), const2),
                  pl.BlockSpec((1, LANES), const2),
                  pl.BlockSpec((tl, tl), const2),
                  pl.BlockSpec((tl, tl), const2)],
        out_specs=[pl.BlockSpec((1, tl, GROUP_WIDTH), main), dense_spec, dense_spec, dense_spec,
                   pl.BlockSpec((1, tl, LANES), main),
                   pl.BlockSpec((1, 2 * SUBLANES, tl), lambda b, i: (b, 0, i))],
        compiler_params=_cparams(("arbitrary", "arbitrary"), big=True),
        name="conv_prep",
    )(pb, pb, pb, pc, pc, pc, ps, conv_b_w, conv_c_w, ones_bd, alog_row, dtb_row, tril, triu)


_TRI_BASE_LOG2 = 3
_TRI_NEWTON_STEPS = 1
DELTA_TILE = 4 * DELTA_CHUNK


def _pair_blockdiag(y):
    low = lax.broadcasted_iota(jnp.int32, y.shape, 1) < HEAD_DIM
    zero = jnp.zeros_like(y)
    return jnp.concatenate([jnp.where(low, y, zero), jnp.where(low, zero, y)], axis=0)


def _pair_dot3_many(a_list, b_list):
    n = a_list[0].shape[0]
    lhs = [jnp.concatenate(_split2(a), axis=0) for a in a_list]
    rhs = [jnp.concatenate([_pair_blockdiag(part) for part in _split2(b)], axis=1) for b in b_list]
    res = [_dot(l, rr) for l, rr in zip(lhs, rhs)]
    return [(x[0:n, 0:LANES] + x[n:2 * n, LANES:2 * LANES])
            + (x[0:n, LANES:2 * LANES] + x[n:2 * n, 0:LANES]) for x in res]


def _pair_dot1_many(a_list, b_list):
    return [_dot(a.astype(BF16), _pair_blockdiag(b.astype(BF16))) for a, b in zip(a_list, b_list)]


def _pair_tri_inverse_many(lmats, r, c):
    def same_block(log2_size):
        return lax.shift_right_logical(r, log2_size) == lax.shift_right_logical(c, log2_size)

    base = same_block(_TRI_BASE_LOG2)
    eye = jnp.where(r == c, 1.0, 0.0)
    a_list = [jnp.where(base, m, 0.0) for m in lmats]
    p_list = [eye - a for a in a_list]
    for _ in range(_TRI_BASE_LOG2 - 1):
        a_list = _pair_dot1_many(a_list, a_list)
        p_list = [p + pa for p, pa in zip(p_list, _pair_dot1_many(p_list, a_list))]
    for log2_size in range(_TRI_BASE_LOG2, int(np.log2(DELTA_CHUNK))):
        couple = jnp.logical_and(same_block(log2_size + 1),
                                 jnp.logical_not(same_block(log2_size)))
        c_list = [jnp.where(couple, m, 0.0) for m in lmats]
        pc_list = _pair_dot1_many(p_list, c_list)
        p_list = [p - pcp for p, pcp in zip(p_list, _pair_dot1_many(pc_list, p_list))]
    for _ in range(_TRI_NEWTON_STEPS):
        lx_list = _pair_dot3_many(lmats, p_list)
        res_list = [(eye - p) - lx for p, lx in zip(p_list, lx_list)]
        p_list = [p + pr for p, pr in zip(p_list, _pair_dot1_many(p_list, res_list))]
    return p_list


def _delta_prep_kernel(q_ref, k_ref, v_ref, gates_ref, gt_ref, *out_refs):
    cs = DELTA_CHUNK
    nc = q_ref.shape[1] // cs
    n_slabs = q_ref.shape[2] // LANES
    r = lax.broadcasted_iota(jnp.int32, (cs, LANES), 0)
    lane = lax.broadcasted_iota(jnp.int32, (cs, LANES), 1)
    c = jnp.bitwise_and(lane, HEAD_DIM - 1)
    low = lane < HEAD_DIM
    er = lax.broadcasted_iota(jnp.int32, (LANES, LANES), 0)
    ec = lax.broadcasted_iota(jnp.int32, (LANES, LANES), 1)
    eye_bf = jnp.where(er == ec, 1.0, 0.0).astype(BF16)
    incl = (r >= c, r <= c)
    strict = (r > c, r < c)

    gates = gates_ref[0]
    gt = gt_ref[0]

    pairs = [(ci, s) for ci in range(nc) for s in range(n_slabs)]
    probs = [(ci, s, d) for d in range(2) for ci in range(nc) for s in range(n_slabs)]

    def tile(ref, ci, s):
        return ref[0, ci * cs:(ci + 1) * cs, s * LANES:(s + 1) * LANES]

    q = {p: tile(q_ref, *p) for p in pairs}
    k = {p: tile(k_ref, *p) for p in pairs}
    v = {p: tile(v_ref, *p) for p in pairs}
    k_bf = {p: k[p].astype(BF16) for p in pairs}
    kq = {p: _dot_nt(jnp.concatenate([k_bf[p], q[p].astype(BF16)], axis=0), _pair_blockdiag(k_bf[p]))
          for p in pairs}
    kk = {p: kq[p][0:cs] for p in pairs}
    qk = {p: kq[p][cs:2 * cs] for p in pairs}

    beta, gc, decay = {}, {}, {}
    for (ci, s, d) in probs:
        rows = slice(ci * cs, (ci + 1) * cs)
        col0 = 4 * d + 2 * s
        beta[ci, s, d] = jnp.where(low, gates[rows, col0:col0 + 1], gates[rows, col0 + 1:col0 + 2])
        gc_col = jnp.where(low, gates[rows, 8 + col0:9 + col0], gates[rows, 9 + col0:10 + col0])
        gc_row = jnp.concatenate([gt[8 + col0:9 + col0, rows], gt[9 + col0:10 + col0, rows]], axis=1)
        gc[ci, s, d] = gc_col
        decay[ci, s, d] = jnp.exp(jnp.where(incl[d], gc_col - gc_row, -jnp.inf))

    lmats = [jnp.where(strict[d], beta[ci, s, d] * kk[ci, s] * decay[ci, s, d], 0.0)
             for (ci, s, d) in probs]
    t_inv = [t.astype(BF16) for t in _pair_tri_inverse_many(lmats, r, c)]

    e = {p: jnp.exp(gc[p]) for p in probs}
    gc_last = {(ci, s, d): (gc[ci, s, d][0:1] if d == 1 else gc[ci, s, d][cs - 1:cs])
               for (ci, s, d) in probs}
    uw = {p: _dot(t, jnp.concatenate(
        [_pair_blockdiag((v[p[0], p[1]] * beta[p]).astype(BF16)),
         _pair_blockdiag((k[p[0], p[1]] * (beta[p] * e[p])).astype(BF16))], axis=1))
        for p, t in zip(probs, t_inv)}
    u = {p: uw[p][:, 0:LANES] for p in probs}
    w = {p: uw[p][:, LANES:2 * LANES] for p in probs}
    kdec_t = {p: _dot_nt(eye_bf, (k[p[0], p[1]] * jnp.exp(gc_last[p] - gc[p])).astype(BF16))
              for p in probs}

    for d in range(2):
        u_ref, w_ref, qg_ref, aqk_ref, kdt_ref, gl_ref = out_refs[6 * d:6 * d + 6]
        for (ci, s) in pairs:
            rows = slice(ci * cs, (ci + 1) * cs)
            lanes = slice(s * LANES, (s + 1) * LANES)
            u_ref[0, rows, lanes] = u[ci, s, d]
            w_ref[0, rows, lanes] = w[ci, s, d].astype(BF16)
            qg_ref[0, rows, lanes] = (q[ci, s] * e[ci, s, d]).astype(BF16)
            aqk_ref[0, rows, lanes] = (qk[ci, s] * decay[ci, s, d]).astype(BF16)
            gl_ref[0, 0, ci:ci + 1, lanes] = jnp.exp(gc_last[ci, s, d])
        for s in range(n_slabs):
            kdt_ref[0, 0, s * LANES:(s + 1) * LANES, :] = jnp.concatenate(
                [kdec_t[ci, s, d] for ci in range(nc)], axis=1).astype(BF16)
        gl_ref[0, 0, nc:SUBLANES, :] = jnp.zeros((SUBLANES - nc, gl_ref.shape[3]), F32)


def _delta_prep(q, k, v, gates, gates_t):
    bsz, seq, width = q.shape
    ta = min(DELTA_TILE, seq)
    nt = seq // ta
    main = lambda b, i: (b, i, 0)
    per_dir_shapes = [jax.ShapeDtypeStruct((bsz, seq, width), F32),
                      jax.ShapeDtypeStruct((bsz, seq, width), BF16),
                      jax.ShapeDtypeStruct((bsz, seq, width), BF16),
                      jax.ShapeDtypeStruct((bsz, seq, width), BF16),
                      jax.ShapeDtypeStruct((bsz, nt, width, ta), BF16),
                      jax.ShapeDtypeStruct((bsz, nt, SUBLANES, width), F32)]
    per_dir_specs = [pl.BlockSpec((1, ta, width), main)] * 4 + [
        pl.BlockSpec((1, 1, width, ta), lambda b, i: (b, i, 0, 0)),
        pl.BlockSpec((1, 1, SUBLANES, width), lambda b, i: (b, i, 0, 0))]
    return pl.pallas_call(
        _delta_prep_kernel,
        out_shape=per_dir_shapes * 2,
        grid=(bsz, nt),
        in_specs=[pl.BlockSpec((1, ta, width), main)] * 3 + [
            pl.BlockSpec((1, ta, LANES), main),
            pl.BlockSpec((1, 2 * SUBLANES, ta), lambda b, i: (b, 0, i))],
        out_specs=per_dir_specs * 2,
        compiler_params=_cparams(("arbitrary", "arbitrary"), big=True),
        name="delta_prep",
    )(q, k, v, gates, gates_t)


DELTA_SCAN_BATCH = 4


def _delta_scan_kernel(*refs):
    ins = (refs[0:6], refs[6:12])
    s0_ref = refs[12]
    outs = refs[13:15]
    send_ref, s_scr = refs[15:17]
    t = pl.program_id(1)
    cs = DELTA_CHUNK
    hd = HEAD_DIM
    nb, ta, width = ins[0][0].shape
    nc = ta // cs
    heads = width // hd
    rr = lax.broadcasted_iota(jnp.int32, (width, width), 0)
    cc = lax.broadcasted_iota(jnp.int32, (width, width), 1)
    same_head = lax.shift_right_logical(rr, 6) == lax.shift_right_logical(cc, 6)
    lane_head = lax.shift_right_logical(lax.broadcasted_iota(jnp.int32, (cs, width), 1), 6)
    zeros = jnp.zeros((cs, width), BF16)

    chains = [(bi, d) for bi in range(nb) for d in range(2)]

    @pl.when(t == 0)
    def _():
        blank = jnp.zeros((hd, hd), F32)
        for (bi, d) in chains:
            s_scr[bi, d] = jnp.concatenate(
                [jnp.concatenate([s0_ref[bi, d, h] if g == h else blank for g in range(heads)], axis=1)
                 for h in range(heads)], axis=0)

    for step in range(nc):
        chunk = (step, nc - 1 - step)
        rows = [slice(chunk[d] * cs, (chunk[d] + 1) * cs) for d in range(2)]
        s_prev = {(bi, d): s_scr[bi, d] for (bi, d) in chains}
        lhs = {(bi, d): jnp.concatenate([ins[d][1][bi, rows[d], :], ins[d][2][bi, rows[d], :]], axis=0)
               for (bi, d) in chains}
        res = {ch: _dot(lhs[ch], s_prev[ch].astype(BF16)) for ch in chains}
        v_new = {(bi, d): (ins[d][0][bi, rows[d], :] - res[bi, d][0:cs]).astype(BF16)
                 for (bi, d) in chains}
        v_bd = {ch: jnp.concatenate([jnp.where(lane_head == h, v_new[ch], zeros)
                                     for h in range(heads)], axis=0) for ch in chains}
        v_tile = {(bi, d): jnp.concatenate(
            [v_new[bi, d] if ci == chunk[d] else zeros for ci in range(nc)], axis=0)
            for (bi, d) in chains}
        o = {(bi, d): res[bi, d][cs:2 * cs] + _dot(ins[d][3][bi, rows[d], :], v_bd[bi, d])
             for (bi, d) in chains}
        upd = {(bi, d): _dot(ins[d][4][bi, 0], v_tile[bi, d]) for (bi, d) in chains}
        for (bi, d) in chains:
            gl = ins[d][5][bi, 0, chunk[d]:chunk[d] + 1, :]
            s_scr[bi, d] = s_prev[bi, d] * gl + jnp.where(same_head, upd[bi, d], 0.0)
            outs[d][bi, rows[d], :] = o[bi, d]

    @pl.when(t == pl.num_programs(1) - 1)
    def _():
        for (bi, d) in chains:
            for h in range(heads):
                send_ref[bi, d, h] = s_scr[bi, d, h * hd:(h + 1) * hd, h * hd:(h + 1) * hd]


def _delta_scan(prep, s0):
    u_f = prep[0]
    bsz, seq, width = u_f.shape
    ta = min(DELTA_TILE, seq)
    nt = seq // ta
    nb = DELTA_SCAN_BATCH if bsz % DELTA_SCAN_BATCH == 0 else 1

    def specs(index3, index4):
        return [pl.BlockSpec((nb, ta, width), index3)] * 4 + [
            pl.BlockSpec((nb, 1, width, ta), index4),
            pl.BlockSpec((nb, 1, SUBLANES, width), index4)]

    fwd3 = lambda b, t: (b, t, 0)
    fwd4 = lambda b, t: (b, t, 0, 0)
    bwd3 = lambda b, t: (b, nt - 1 - t, 0)
    bwd4 = lambda b, t: (b, nt - 1 - t, 0, 0)
    heads = width // HEAD_DIM
    state_spec = pl.BlockSpec((nb, 2, heads, HEAD_DIM, HEAD_DIM), lambda b, t: (b, 0, 0, 0, 0))
    return pl.pallas_call(
        _delta_scan_kernel,
        out_shape=[jax.ShapeDtypeStruct((bsz, seq, width), F32),
                   jax.ShapeDtypeStruct((bsz, seq, width), F32),
                   jax.ShapeDtypeStruct((bsz, 2, heads, HEAD_DIM, HEAD_DIM), F32)],
        grid=(bsz // nb, nt),
        in_specs=specs(fwd3, fwd4) + specs(bwd3, bwd4) + [state_spec],
        out_specs=[pl.BlockSpec((nb, ta, width), fwd3), pl.BlockSpec((nb, ta, width), bwd3),
                   state_spec],
        scratch_shapes=[pltpu.VMEM((nb, 2, width, width), F32)],
        compiler_params=_cparams(("arbitrary", "arbitrary")),
        name="delta_scan",
    )(*prep, s0.astype(F32))


def _deltanet(q, k, v, gates, gates_t, s0):
    prep = _delta_prep(q, k, v, gates, gates_t)
    return _delta_scan(prep, s0)


def _fourier_pos_kernel(w_ref, a1_ref, b1_ref, a0_ref, b0_ref, o_ref, c_scr, s_scr, *, scale):
    i = pl.program_id(1)
    tm = c_scr.shape[0]
    sub = a0_ref.shape[0]
    a0 = a0_ref[...]
    b0 = b0_ref[...]
    for rblk in range(tm // sub):
        j1 = i * (tm // sub) + rblk
        a1 = a1_ref[pl.ds(j1, 1), :]
        b1 = b1_ref[pl.ds(j1, 1), :]
        c_scr[rblk * sub:(rblk + 1) * sub, :] = (a1 * a0 - b1 * b0).astype(BF16)
        s_scr[rblk * sub:(rblk + 1) * sub, :] = (b1 * a0 + a1 * b0).astype(BF16)
    res = (_dot(c_scr[...], w_ref[0]) - _dot(s_scr[...], w_ref[1])) * scale
    gw = o_ref.shape[2]
    for b in range(o_ref.shape[0]):
        o_ref[b] = res[:, b * gw:(b + 1) * gw].astype(BF16)


def _fourier_positions(w, tabs, bsz, gw):
    _, seq, ncol = w.shape
    a1, b1, a0, b0 = tabs
    tm = min(512, seq)
    nsplit = 2 if (bsz % 2 == 0 and seq > 1024) else 1
    bpart = bsz // nsplit
    const2 = lambda p, i: (0, 0)
    single = pl.Buffered(1)
    scale = 1.0 / np.sqrt(float(seq) * HEAD_DIM)
    return pl.pallas_call(
        functools.partial(_fourier_pos_kernel, scale=scale),
        out_shape=jax.ShapeDtypeStruct((bsz, seq, gw), BF16),
        grid=(nsplit, seq // tm),
        in_specs=[pl.BlockSpec((2, seq, bpart * gw), lambda p, i: (0, 0, p), pipeline_mode=single),
                  pl.BlockSpec(a1.shape, const2, pipeline_mode=single),
                  pl.BlockSpec(b1.shape, const2, pipeline_mode=single),
                  pl.BlockSpec(a0.shape, const2, pipeline_mode=single),
                  pl.BlockSpec(b0.shape, const2, pipeline_mode=single)],
        out_specs=pl.BlockSpec((bpart, tm, gw), lambda p, i: (p, i, 0)),
        scratch_shapes=[pltpu.VMEM((tm, seq), BF16), pltpu.VMEM((tm, seq), BF16)],
        compiler_params=_cparams(("arbitrary", "arbitrary"), big=True),
        name="fourier_positions",
    )(w, a1, b1, a0, b0)


def _rope_tables(seq):
    rows = seq // GRID_W
    r, cl = jnp.meshgrid(jnp.arange(rows), jnp.arange(GRID_W), indexing='ij')
    pos_r = r.reshape(-1).astype(F32)
    pos_c = cl.reshape(-1).astype(F32)
    quarter = HEAD_DIM // 4
    inv_freq = ROPE_THETA ** (-jnp.arange(quarter, dtype=F32) / quarter)
    cos_parts, sin_parts = [], []
    for pos in (pos_r, pos_c):
        ang = pos[:, None] * inv_freq[None, :]
        cos_parts += [jnp.cos(ang), jnp.cos(ang)]
        sin_parts += [-jnp.sin(ang), jnp.sin(ang)]
    cos = jnp.concatenate(cos_parts, axis=-1)
    sin = jnp.concatenate(sin_parts, axis=-1)
    return jnp.tile(cos, (1, 2)), jnp.tile(sin, (1, 2))


def _trig_table(num_rows, row_mult, seq):
    j = np.arange(num_rows, dtype=np.int64)[:, None] * row_mult
    k = np.arange(seq, dtype=np.int64)[None, :]
    ang = 2.0 * np.pi * ((j * k) % seq).astype(np.float64) / seq
    return jnp.asarray(np.cos(ang), F32), jnp.asarray(np.sin(ang), F32)


def _fourier_tables(seq):
    sub = 64
    a1, b1 = _trig_table(seq // sub, sub, seq)
    a0, b0 = _trig_table(sub, 1, seq)
    idx = np.arange(GROUP_WIDTH)
    same = (idx[:, None] // HEAD_DIM) == (idx[None, :] // HEAD_DIM)
    ang = 2.0 * np.pi * ((idx[:, None] % HEAD_DIM) * (idx[None, :] % HEAD_DIM) % HEAD_DIM) / HEAD_DIM
    c_bd = jnp.asarray(np.where(same, np.cos(ang), 0.0), F32)
    s_bd = jnp.asarray(np.where(same, np.sin(ang), 0.0), F32)
    return (a1, b1, a0, b0), c_bd, s_bd


def _to_lane_dense(t):
    b, hh, l, d = t.shape
    return jnp.swapaxes(t, 1, 2).reshape(b, l, hh * d)


def _to_head_major(t, heads):
    b, l, w = t.shape
    return jnp.swapaxes(t.reshape(b, l, heads, w // heads), 1, 2)


def _swap_halves(t):
    return jnp.concatenate([t[..., HEAD_DIM:], t[..., :HEAD_DIM]], axis=-1)


def _trunk_layer(x, mod, wts, consts, ctx, final_g):
    bsz, seq, _ = x.shape
    x1 = _ffn(x, mod, wts['norm_ffn1'], wts['ffn1_w1'], wts['ffn1_w3'], wts['ffn1_w2'], which=0)
    rope_tabs = consts['rope'] if ctx is not None else None
    proj = _inproj(x1, mod, wts['norm_mix'], wts['w_in'], consts['ones_bd'], wts['gq'], wts['gk'],
                   consts['c_bd'], consts['s_bd'], rope_tabs, ctx is None)
    q, ka, kb, va, vb, pb, pc, ps, wd = proj[:9]

    k_cache = v_cache = None
    if ctx is None:
        k_cache = _to_head_major(proj[9], KV_HEADS_A)
        v_cache = _to_head_major(proj[10], KV_HEADS_A)
        s0 = jnp.zeros((bsz, 2, N_HEADS_C, HEAD_DIM, HEAD_DIM), F32)
    else:
        k_ctx, v_ctx, s0 = ctx
        k_past = _to_lane_dense(k_ctx).astype(BF16)
        v_past = _to_lane_dense(v_ctx).astype(BF16)
        ka = jnp.concatenate([k_past, ka], axis=1)
        kb = jnp.concatenate([_swap_halves(k_past), kb], axis=1)
        ones = jnp.ones_like(v_past)
        va = jnp.concatenate([jnp.concatenate([v_past, ones], axis=-1), va], axis=1)
        vb = jnp.concatenate([jnp.concatenate([_swap_halves(v_past), ones], axis=-1), vb], axis=1)
    o_a = _attention(q, ka, kb, va, vb)

    o_b, qc, kc, vc, gates, gates_t = _conv_prep(pb, pc, ps, wts['conv_b_w'], wts['conv_c_w'],
                                                 consts['ones_bd'], wts['alog_row'], wts['dtb_row'])
    d_f, d_b, s_end = _deltanet(qc, kc, vc, gates, gates_t, s0)

    o_d = _fourier_positions(wd, consts['fourier'], bsz, GROUP_WIDTH)

    x3 = _ffn(x1, mod, wts['norm_ffn2'], wts['ffn2_w1'], wts['ffn2_w3'], wts['ffn2_w2'], which=2,
              mix=(o_a, o_b, (d_f, d_b, pc, consts['ones_bd'], wts['delta_norm']), o_d),
              w_out=wts['w_out'], final_g=final_g)
    return x3, k_cache, v_cache, s_end


def _lane_row(values, start):
    row = jnp.zeros((1, LANES), F32)
    return row.at[0, start:start + values.shape[0]].set(values.astype(F32))


def kernel(x_prompt, x_sample, c, cache_k, cache_v, state_delta, c_ctx, mod_w, mod_b, norm_ffn1, norm_mix, norm_ffn2, ffn1_w1, ffn1_w3, ffn1_w2, ffn2_w1, ffn2_w3, ffn2_w2, w_in, w_out, q_norm, k_norm, conv_b_w, conv_c_w, delta_a_log, delta_dt_bias, delta_norm, final_norm):
    depth = mod_w.shape[0]
    d_model = x_prompt.shape[-1]
    dec_b = x_sample.shape[0]
    assert w_in.shape[-1] == PROJ_MAIN + PROJ_SMALL + GROUP_WIDTH

    n_cond = 1 + dec_b
    rows = -(-n_cond // SUBLANES) * SUBLANES
    cond = jnp.zeros((rows, d_model), F32).at[0].set(c_ctx).at[1:n_cond].set(c)

    idx = np.arange(LANES)
    ones_bd = jnp.asarray((idx[:, None] // HEAD_DIM) == (idx[None, :] // HEAD_DIM), BF16)
    consts_p = {'ones_bd': ones_bd}
    consts_s = {'ones_bd': ones_bd, 'rope': _rope_tables(x_sample.shape[1])}
    consts_p['fourier'], consts_p['c_bd'], consts_p['s_bd'] = _fourier_tables(x_prompt.shape[1])
    consts_s['fourier'], consts_s['c_bd'], consts_s['s_bd'] = _fourier_tables(x_sample.shape[1])

    yp, ys = x_prompt, x_sample
    k_list, v_list, s_list = [], [], []
    for l in range(depth):
        w_in_l = w_in[l]
        small = w_in_l[:, PROJ_MAIN:PROJ_MAIN + PROJ_SMALL]
        w_in_r = jnp.concatenate(
            [w_in_l[:, :PROJ_MAIN], w_in_l[:, PROJ_MAIN + PROJ_SMALL:], small,
             jnp.zeros((d_model, LANES - PROJ_SMALL), F32)], axis=1).astype(BF16)
        wts = {
            'norm_ffn1': norm_ffn1[l], 'norm_mix': norm_mix[l], 'norm_ffn2': norm_ffn2[l],
            'ffn1_w1': ffn1_w1[l].astype(BF16), 'ffn1_w3': ffn1_w3[l].astype(BF16),
            'ffn1_w2': ffn1_w2[l].astype(BF16),
            'ffn2_w1': ffn2_w1[l].astype(BF16), 'ffn2_w3': ffn2_w3[l].astype(BF16),
            'ffn2_w2': ffn2_w2[l].astype(BF16),
            'w_in': w_in_r, 'w_out': w_out[l].astype(BF16),
            'gq': jnp.tile(q_norm[l], 2).reshape(1, LANES),
            'gk': jnp.tile(k_norm[l], 2).reshape(1, LANES),
            'conv_b_w': conv_b_w[l], 'conv_c_w': conv_c_w[l],
            'alog_row': _lane_row(delta_a_log[l].reshape(-1), 8),
            'dtb_row': _lane_row(delta_dt_bias[l].reshape(-1), 8),
            'delta_norm': delta_norm[l],
        }
        mod = _modulation(cond, mod_w, mod_b, l).reshape(rows, N_MOD, d_model)
        fin = final_norm if l == depth - 1 else None
        yp, k_l, v_l, s_l = _trunk_layer(yp, mod[0:1], wts, consts_p, None, fin)
        k_list.append(k_l)
        v_list.append(v_l)
        s_list.append(s_l)
        ys, _, _, _ = _trunk_layer(ys, mod[1:n_cond], wts, consts_s,
                                   (cache_k[:, l], cache_v[:, l], state_delta[:, l]), fin)
    return (yp, ys, jnp.stack(k_list, axis=1), jnp.stack(v_list, axis=1),
            jnp.stack(s_list, axis=1))
```

```python
import functools

import jax
import jax.numpy as jnp
import numpy as np
from jax import lax
from jax.experimental import pallas as pl
from jax.experimental.pallas import tpu as pltpu

F32 = jnp.float32
BF16 = jnp.bfloat16

HEAD_DIM = 64
N_HEADS_A = 4
KV_HEADS_A = 2
N_HEADS_C = 4
GROUP_WIDTH = 256
GRID_W = 64
DELTA_CHUNK = 64
ROPE_THETA = 10000.0
N_MOD = 9
EPS = 1e-6
PROJ_MAIN = 2304
PROJ_SMALL = 16
LANES = 128
SUBLANES = 8
VMEM_LIMIT_BYTES = 56 * 1024 * 1024


def _cparams(sem, big=False):
    return pltpu.CompilerParams(
        dimension_semantics=sem,
        vmem_limit_bytes=VMEM_LIMIT_BYTES if big else None)


def _dot(a, b):
    return jnp.dot(a, b, preferred_element_type=F32)


def _dot_nt(a, b):
    return lax.dot_general(a, b, (((1,), (1,)), ((), ())), preferred_element_type=F32)


def _split2(x):
    hi = x.astype(BF16)
    lo = (x - hi.astype(F32)).astype(BF16)
    return hi, lo


def _split3(x):
    hi = x.astype(BF16)
    r = x - hi.astype(F32)
    mid = r.astype(BF16)
    lo = (r - mid.astype(F32)).astype(BF16)
    return hi, mid, lo


def _rms(x, g):
    return x * lax.rsqrt(jnp.mean(x * x, axis=-1, keepdims=True) + EPS) * g


def _silu(x):
    return x * jax.nn.sigmoid(x)


def _group_sumsq(x, ones_bd):
    hi, lo = _split2(x * x)
    return _dot(hi, ones_bd) + _dot(lo, ones_bd)


def _mod_kernel(c_ref, w_ref, b_ref, o_ref):
    s = _silu(c_ref[...]).astype(BF16)
    o_ref[...] = _dot(s, w_ref[0].astype(BF16)) + b_ref[0]


def _modulation(cond, mod_w, mod_b, layer):
    rows, d = cond.shape
    depth, _, n = mod_w.shape
    tn = d
    return pl.pallas_call(
        _mod_kernel,
        out_shape=jax.ShapeDtypeStruct((rows, n), F32),
        grid=(n // tn,),
        in_specs=[pl.BlockSpec((rows, d), lambda j: (0, 0)),
                  pl.BlockSpec((1, d, tn), lambda j: (layer, 0, j)),
                  pl.BlockSpec((1, 1, tn), lambda j: (layer, 0, j))],
        out_specs=pl.BlockSpec((rows, tn), lambda j: (0, j)),
        compiler_params=_cparams(("arbitrary",)),
        name="modulation",
    )(cond, mod_w, mod_b.reshape(depth, 1, n))


def _ffn_chunks(f):
    step = 1024
    return [(s, min(s + step, f)) for s in range(0, f, step)]


def _gated_delta_out(of_ref, ob_ref, z_ref, ones_ref, g_ref):
    ones_bd = ones_ref[...]
    parts = []
    for s in range(of_ref.shape[2] // LANES):
        lanes = slice(s * LANES, (s + 1) * LANES)
        o = of_ref[0, :, lanes] + ob_ref[0, :, lanes]
        ms = _group_sumsq(o, ones_bd) * (1.0 / HEAD_DIM)
        y = o * lax.rsqrt(ms + EPS) * g_ref[...]
        parts.append((y * _silu(z_ref[0, :, lanes])).astype(BF16))
    return jnp.concatenate(parts, axis=-1)


def _ffn_kernel(*refs, which, n_mix, final):
    x_ref, mod_ref, g_ref, w1_ref, w3_ref, w2_ref = refs[:6]
    pos = 6
    mix_vals = []
    wout_ref = None
    if n_mix:
        oa_ref, ob_ref, dof_ref, dob_ref, z_ref, ones_ref, dg_ref, od_ref, wout_ref = refs[pos:pos + 9]
        pos += 9
        mix_vals = [oa_ref[0], ob_ref[0],
                    _gated_delta_out(dof_ref, dob_ref, z_ref, ones_ref, dg_ref), od_ref[0]]
    gf_ref = None
    if final:
        gf_ref = refs[pos]
        pos += 1
    o_ref = refs[pos]

    x = x_ref[0]
    mod = mod_ref[0]
    if n_mix:
        acc = None
        for i, m in enumerate(mix_vals):
            w = m.shape[-1]
            part = _dot(m, wout_ref[i * w:(i + 1) * w, :])
            acc = part if acc is None else acc + part
        x = x + mod[5:6] * acc
    sh = mod[3 * which:3 * which + 1]
    sc = mod[3 * which + 1:3 * which + 2]
    gt = mod[3 * which + 2:3 * which + 3]
    h = (_rms(x, g_ref[...]) * (1.0 + sc) + sh).astype(BF16)
    out = None
    for s, e in _ffn_chunks(w1_ref.shape[1]):
        a = _dot(h, w1_ref[:, s:e])
        b = _dot(h, w3_ref[:, s:e])
        act = (_silu(a) * b).astype(BF16)
        part = _dot(act, w2_ref[s:e, :])
        out = part if out is None else out + part
    xn = x + 0.5 * gt * out
    if final:
        xn = _rms(xn, gf_ref[...])
    o_ref[0] = xn


def _ffn(x, mod, g, w1, w3, w2, *, which, mix=None, w_out=None, final_g=None):
    bsz, seq, d = x.shape
    f = w1.shape[1]
    tm = min(512, seq)
    per_batch_mod = mod.shape[0] > 1
    mod_map = (lambda b, i: (b, 0, 0)) if per_batch_mod else (lambda b, i: (0, 0, 0))
    const2 = lambda b, i: (0, 0)
    single = pl.Buffered(1)
    args = [x, mod, g.reshape(1, d), w1, w3, w2]
    specs = [pl.BlockSpec((1, tm, d), lambda b, i: (b, i, 0)),
             pl.BlockSpec((1, N_MOD, d), mod_map),
             pl.BlockSpec((1, d), const2),
             pl.BlockSpec((d, f), const2, pipeline_mode=single),
             pl.BlockSpec((d, f), const2, pipeline_mode=single),
             pl.BlockSpec((f, d), const2, pipeline_mode=single)]
    n_mix = 0
    if mix is not None:
        o_a, o_b, (d_f, d_b, pc, ones_bd, delta_g), o_d = mix
        n_mix = 4
        tok = lambda b, i: (b, i, 0)
        gw = o_a.shape[-1]
        z_block = pc.shape[2] // gw - 1
        args += [o_a, o_b, d_f, d_b, pc, ones_bd,
                 jnp.tile(delta_g, LANES // HEAD_DIM).reshape(1, LANES), o_d, w_out]
        specs += [pl.BlockSpec((1, tm, gw), tok), pl.BlockSpec((1, tm, gw), tok),
                  pl.BlockSpec((1, tm, gw), tok), pl.BlockSpec((1, tm, gw), tok),
                  pl.BlockSpec((1, tm, gw), lambda b, i: (b, i, z_block)),
                  pl.BlockSpec((LANES, LANES), const2), pl.BlockSpec((1, LANES), const2),
                  pl.BlockSpec((1, tm, gw), tok),
                  pl.BlockSpec(w_out.shape, const2, pipeline_mode=single)]
    if final_g is not None:
        args.append(final_g.reshape(1, d))
        specs.append(pl.BlockSpec((1, d), const2))
    return pl.pallas_call(
        functools.partial(_ffn_kernel, which=which, n_mix=n_mix, final=final_g is not None),
        out_shape=jax.ShapeDtypeStruct((bsz, seq, d), F32),
        grid=(bsz, seq // tm),
        in_specs=specs,
        out_specs=pl.BlockSpec((1, tm, d), lambda b, i: (b, i, 0)),
        compiler_params=_cparams(("arbitrary", "arbitrary"), big=True),
        name="ffn",
    )(*args)


_PROJ_A = (0, 512)
_PROJ_B = (512, 1280)
_PROJ_C = (1280, 2304)
_PROJ_D = (2304, 2560)
_PROJ_S = (2560, 2688)


def _rope(x, cos, sin_signed):
    lane = lax.broadcasted_iota(jnp.int32, x.shape, 1)
    up = pltpu.roll(x, LANES - 16, axis=1)
    down = pltpu.roll(x, 16, axis=1)
    partner = jnp.where((lane % 32) < 16, up, down)
    return x * cos + partner * sin_signed


def _inproj_kernel(*refs, rope, want_cache):
    x_ref, mod_ref, g_ref, w_ref, ones_ref, gq_ref, gk_ref, cbd_ref, sbd_ref = refs[:9]
    pos = 9
    if rope:
        cos_ref, sin_ref = refs[pos:pos + 2]
        pos += 2
    q_ref, ka_ref, kb_ref, va_ref, vb_ref, pb_ref, pc_ref, ps_ref, wd_ref = refs[pos:pos + 9]
    pos += 9
    mod = mod_ref[0]
    h = (_rms(x_ref[0], g_ref[...]) * (1.0 + mod[4:5]) + mod[3:4]).astype(BF16)

    def proj(cols):
        return _dot(h, w_ref[:, cols[0]:cols[1]])

    pa = proj(_PROJ_A)
    ones_bd = ones_ref[...]
    sumsq = [_group_sumsq(pa[:, s * LANES:(s + 1) * LANES], ones_bd) for s in range(3)]

    pb_ref[0] = proj(_PROJ_B)
    pc_ref[0] = proj(_PROJ_C)
    ps_ref[0] = proj(_PROJ_S)

    xd = proj(_PROJ_D).astype(BF16)
    wd_ref[0] = _dot(xd, cbd_ref[...].astype(BF16)).astype(BF16)
    wd_ref[1] = _dot(xd, sbd_ref[...].astype(BF16)).astype(BF16)

    def normed(s, g):
        x = pa[:, s * LANES:(s + 1) * LANES]
        return x * lax.rsqrt(sumsq[s] * (1.0 / HEAD_DIM) + EPS) * g

    q0 = normed(0, gq_ref[...])
    q1 = normed(1, gq_ref[...])
    k = normed(2, gk_ref[...])
    v = pa[:, 384:512]
    if want_cache:
        refs[pos][0] = k
        refs[pos + 1][0] = v
    if rope:
        cos = cos_ref[...]
        sin = sin_ref[...]
        q0 = _rope(q0, cos, sin)
        q1 = _rope(q1, cos, sin)
        k = _rope(k, cos, sin)
    scale = HEAD_DIM ** -0.5
    q_ref[0, :, 0:128] = (q0 * scale).astype(BF16)
    q_ref[0, :, 128:256] = (q1 * scale).astype(BF16)
    ka_ref[0] = k.astype(BF16)
    kb_ref[0] = pltpu.roll(k, HEAD_DIM, axis=1).astype(BF16)
    ones = jnp.ones(v.shape, BF16)
    va_ref[0, :, 0:LANES] = v.astype(BF16)
    va_ref[0, :, LANES:2 * LANES] = ones
    vb_ref[0, :, 0:LANES] = pltpu.roll(v, HEAD_DIM, axis=1).astype(BF16)
    vb_ref[0, :, LANES:2 * LANES] = ones


def _inproj(x, mod, g, w_in_r, ones_bd, gq, gk, c_bd, s_bd, rope_tabs, want_cache):
    bsz, seq, d = x.shape
    tm = min(512, seq)
    per_batch_mod = mod.shape[0] > 1
    mod_map = (lambda b, i: (b, 0, 0)) if per_batch_mod else (lambda b, i: (0, 0, 0))
    const2 = lambda b, i: (0, 0)
    tok = lambda b, i: (b, i, 0)
    rope = rope_tabs is not None
    gw = GROUP_WIDTH
    args = [x, mod, g.reshape(1, d), w_in_r, ones_bd, gq, gk, c_bd, s_bd]
    specs = [pl.BlockSpec((1, tm, d), tok),
             pl.BlockSpec((1, N_MOD, d), mod_map),
             pl.BlockSpec((1, d), const2),
             pl.BlockSpec(w_in_r.shape, const2, pipeline_mode=pl.Buffered(1)),
             pl.BlockSpec((LANES, LANES), const2),
             pl.BlockSpec((1, LANES), const2),
             pl.BlockSpec((1, LANES), const2),
             pl.BlockSpec((gw, gw), const2),
             pl.BlockSpec((gw, gw), const2)]
    if rope:
        args += list(rope_tabs)
        specs += [pl.BlockSpec((tm, LANES), lambda b, i: (i, 0))] * 2

    def tok_out(width, dtype):
        return jax.ShapeDtypeStruct((bsz, seq, width), dtype), pl.BlockSpec((1, tm, width), tok)

    outs = [tok_out(256, BF16),
            tok_out(LANES, BF16), tok_out(LANES, BF16),
            tok_out(2 * LANES, BF16), tok_out(2 * LANES, BF16),
            tok_out(_PROJ_B[1] - _PROJ_B[0], F32),
            tok_out(_PROJ_C[1] - _PROJ_C[0], F32),
            tok_out(_PROJ_S[1] - _PROJ_S[0], F32),
            (jax.ShapeDtypeStruct((2, seq, bsz * gw), BF16),
             pl.BlockSpec((2, tm, gw), lambda b, i: (0, i, b)))]
    if want_cache:
        outs += [tok_out(LANES, F32), tok_out(LANES, F32)]
    return pl.pallas_call(
        functools.partial(_inproj_kernel, rope=rope, want_cache=want_cache),
        out_shape=[o[0] for o in outs],
        grid=(bsz, seq // tm),
        in_specs=specs,
        out_specs=[o[1] for o in outs],
        compiler_params=_cparams(("arbitrary", "arbitrary"), big=True),
        name="inproj",
    )(*args)


def _attn_kernel(q_ref, ka_ref, kb_ref, va_ref, vb_ref, o_ref):
    tq = q_ref.shape[1]
    lane = lax.broadcasted_iota(jnp.int32, (tq, LANES), 1)
    low = lane < HEAD_DIM
    arrangement = ((ka_ref, va_ref), (kb_ref, vb_ref), (kb_ref, vb_ref), (ka_ref, va_ref))
    def scores(h):
        slab = q_ref[0, :, (h // 2) * LANES:(h // 2 + 1) * LANES]
        keep = low if h % 2 == 0 else jnp.logical_not(low)
        qh = jnp.where(keep, slab, jnp.zeros_like(slab))
        return _dot_nt(qh, arrangement[h][0][0])

    def weighted_values(h, s):
        m = jnp.max(s, axis=-1, keepdims=True)
        p = jnp.exp(s - m)
        pv = _dot(p.astype(BF16), arrangement[h][1][0])
        return pv[:, 0:LANES] / pv[:, LANES:LANES + 1]

    outs = []
    s_next = scores(0)
    for h in range(N_HEADS_A):
        s_cur = s_next
        if h + 1 < N_HEADS_A:
            s_next = scores(h + 1)
        outs.append(weighted_values(h, s_cur))
    o_ref[0, :, 0:128] = jnp.where(low, outs[0], outs[1]).astype(BF16)
    o_ref[0, :, 128:256] = jnp.where(low, outs[2], outs[3]).astype(BF16)


def _attention(q, ka, kb, va, vb):
    bsz, seq, _ = q.shape
    lk = ka.shape[1]
    tq = min(512, seq)
    k_spec = pl.BlockSpec((1, lk, LANES), lambda b, i: (b, 0, 0))
    v_spec = pl.BlockSpec((1, lk, 2 * LANES), lambda b, i: (b, 0, 0))
    return pl.pallas_call(
        _attn_kernel,
        out_shape=jax.ShapeDtypeStruct((bsz, seq, 256), BF16),
        grid=(bsz, seq // tq),
        in_specs=[pl.BlockSpec((1, tq, 256), lambda b, i: (b, i, 0)),
                  k_spec, k_spec, v_spec, v_spec],
        out_specs=pl.BlockSpec((1, tq, 256), lambda b, i: (b, i, 0)),
        compiler_params=_cparams(("arbitrary", "arbitrary"), big=True),
        name="attention",
    )(q, ka, kb, va, vb)


def _conv3(x, prev_row, next_row, w):
    rows = x.shape[0]
    ridx = lax.broadcasted_iota(jnp.int32, x.shape, 0)
    xm = jnp.where(ridx == 0, prev_row, pltpu.roll(x, 1, axis=0))
    xp = jnp.where(ridx == rows - 1, next_row, pltpu.roll(x, rows - 1, axis=0))
    return xm * w[0:1] + x * w[1:2] + xp * w[2:3]


def _conv_prep_kernel(pb_ref, pbp_ref, pbn_ref, pc_ref, pcp_ref, pcn_ref, ps_ref,
                      wb_ref, wc_ref, ones_ref, alog_ref, dtb_ref, tril_ref, triu_ref,
                      ob_ref, q_ref, k_ref, v_ref, gates_ref, gatest_ref):
    i = pl.program_id(1)
    has_prev = i > 0
    has_next = i < pl.num_programs(1) - 1
    gw = GROUP_WIDTH

    pb = pb_ref[0]
    u = pb[:, gw:2 * gw] * pb[:, 2 * gw:3 * gw]
    pbp = pbp_ref[0, SUBLANES - 1:SUBLANES, :]
    pbn = pbn_ref[0, 0:1, :]
    u_prev = jnp.where(has_prev, pbp[:, gw:2 * gw] * pbp[:, 2 * gw:3 * gw], 0.0)
    u_next = jnp.where(has_next, pbn[:, gw:2 * gw] * pbn[:, 2 * gw:3 * gw], 0.0)
    ob_ref[0] = (pb[:, 0:gw] * _conv3(u, u_prev, u_next, wb_ref[...])).astype(BF16)

    pc = pc_ref[0]
    x3 = pc[:, 0:3 * gw]
    x_prev = jnp.where(has_prev, pcp_ref[0, SUBLANES - 1:SUBLANES, 0:3 * gw], 0.0)
    x_next = jnp.where(has_next, pcn_ref[0, 0:1, 0:3 * gw], 0.0)
    qkv = _silu(_conv3(x3, x_prev, x_next, wc_ref[...]))
    ones_bd = ones_ref[...]

    def l2n(x):
        return x * lax.rsqrt(_group_sumsq(x, ones_bd) + EPS)

    q_scale = HEAD_DIM ** -0.5
    for s in range(2):
        q_ref[0, :, s * LANES:(s + 1) * LANES] = l2n(qkv[:, s * LANES:(s + 1) * LANES]) * q_scale
        k_ref[0, :, s * LANES:(s + 1) * LANES] = l2n(qkv[:, gw + s * LANES:gw + (s + 1) * LANES])
    v_ref[0] = qkv[:, 2 * gw:3 * gw]

    raw = ps_ref[0]
    beta = jax.nn.sigmoid(raw)
    y = raw + dtb_ref[...]
    softplus = jnp.maximum(y, 0.0) + jnp.log(1.0 + jnp.exp(-jnp.abs(y)))
    g = -jnp.exp(alog_ref[...]) * softplus
    g_hi, g_mid, g_lo = _split3(g)
    tril = tril_ref[...]
    triu = triu_ref[...]
    gc_f = _dot(tril, g_hi) + _dot(tril, g_mid) + _dot(tril, g_lo)
    gc_b = _dot(triu, g_hi) + _dot(triu, g_mid) + _dot(triu, g_lo)
    lane = lax.broadcasted_iota(jnp.int32, raw.shape, 1)
    gates = jnp.where(lane < 8, beta, jnp.where(lane < 12, gc_f, gc_b))
    gates_ref[0] = gates
    gatest_ref[0] = jnp.transpose(gates)[0:2 * SUBLANES, :]


def _conv_prep(pb, pc, ps, conv_b_w, conv_c_w, ones_bd, alog_row, dtb_row):
    bsz, seq, _ = pb.shape
    tl = min(512, seq)
    nblk8 = seq // SUBLANES
    r8 = tl // SUBLANES
    main = lambda b, i: (b, i, 0)
    prev = lambda b, i: (b, jnp.maximum(i * r8 - 1, 0), 0)
    nxt = lambda b, i: (b, jnp.minimum((i + 1) * r8, nblk8 - 1), 0)
    const2 = lambda b, i: (0, 0)
    idx = np.arange(tl)
    same_chunk = (idx[:, None] // DELTA_CHUNK) == (idx[None, :] // DELTA_CHUNK)
    tril = jnp.asarray(same_chunk & (idx[:, None] >= idx[None, :]), BF16)
    triu = jnp.asarray(same_chunk & (idx[:, None] <= idx[None, :]), BF16)
    dense = jax.ShapeDtypeStruct((bsz, seq, GROUP_WIDTH), F32)
    dense_spec = pl.BlockSpec((1, tl, GROUP_WIDTH), main)
    return pl.pallas_call(
        _conv_prep_kernel,
        out_shape=[jax.ShapeDtypeStruct((bsz, seq, GROUP_WIDTH), BF16), dense, dense, dense,
                   jax.ShapeDtypeStruct((bsz, seq, LANES), F32),
                   jax.ShapeDtypeStruct((bsz, 2 * SUBLANES, seq), F32)],
        grid=(bsz, seq // tl),
        in_specs=[pl.BlockSpec((1, tl, 768), main),
                  pl.BlockSpec((1, SUBLANES, 768), prev),
                  pl.BlockSpec((1, SUBLANES, 768), nxt),
                  pl.BlockSpec((1, tl, 1024), main),
                  pl.BlockSpec((1, SUBLANES, 1024), prev),
                  pl.BlockSpec((1, SUBLANES, 1024), nxt),
                  pl.BlockSpec((1, tl, LANES), main),
                  pl.BlockSpec((3, GROUP_WIDTH), const2),
                  pl.BlockSpec((3, 3 * GROUP_WIDTH), const2),
                  pl.BlockSpec((LANES, LANES), const2),
                  pl.BlockSpec((1, LANES), const2),
                  pl.BlockSpec((1, LANES), const2),
                  pl.BlockSpec((tl, tl), const2),
                  pl.BlockSpec((tl, tl), const2)],
        out_specs=[pl.BlockSpec((1, tl, GROUP_WIDTH), main), dense_spec, dense_spec, dense_spec,
                   pl.BlockSpec((1, tl, LANES), main),
                   pl.BlockSpec((1, 2 * SUBLANES, tl), lambda b, i: (b, 0, i))],
        compiler_params=_cparams(("arbitrary", "arbitrary"), big=True),
        name="conv_prep",
    )(pb, pb, pb, pc, pc, pc, ps, conv_b_w, conv_c_w, ones_bd, alog_row, dtb_row, tril, triu)


_TRI_BASE_LOG2 = 3
_TRI_NEWTON_STEPS = 1
DELTA_TILE = 4 * DELTA_CHUNK


def _pair_blockdiag(y):
    low = lax.broadcasted_iota(jnp.int32, y.shape, 1) < HEAD_DIM
    zero = jnp.zeros_like(y)
    return jnp.concatenate([jnp.where(low, y, zero), jnp.where(low, zero, y)], axis=0)


def _pair_dot3_many(a_list, b_list):
    n = a_list[0].shape[0]
    lhs = [jnp.concatenate(_split2(a), axis=0) for a in a_list]
    rhs = [jnp.concatenate([_pair_blockdiag(part) for part in _split2(b)], axis=1) for b in b_list]
    res = [_dot(l, rr) for l, rr in zip(lhs, rhs)]
    return [(x[0:n, 0:LANES] + x[n:2 * n, LANES:2 * LANES])
            + (x[0:n, LANES:2 * LANES] + x[n:2 * n, 0:LANES]) for x in res]


def _pair_dot1_many(a_list, b_list):
    return [_dot(a.astype(BF16), _pair_blockdiag(b.astype(BF16))) for a, b in zip(a_list, b_list)]


def _pair_tri_inverse_many(lmats, r, c):
    def same_block(log2_size):
        return lax.shift_right_logical(r, log2_size) == lax.shift_right_logical(c, log2_size)

    base = same_block(_TRI_BASE_LOG2)
    eye = jnp.where(r == c, 1.0, 0.0)
    a_list = [jnp.where(base, m, 0.0) for m in lmats]
    p_list = [eye - a for a in a_list]
    for _ in range(_TRI_BASE_LOG2 - 1):
        a_list = _pair_dot1_many(a_list, a_list)
        p_list = [p + pa for p, pa in zip(p_list, _pair_dot1_many(p_list, a_list))]
    for log2_size in range(_TRI_BASE_LOG2, int(np.log2(DELTA_CHUNK))):
        couple = jnp.logical_and(same_block(log2_size + 1),
                                 jnp.logical_not(same_block(log2_size)))
        c_list = [jnp.where(couple, m, 0.0) for m in lmats]
        pc_list = _pair_dot1_many(p_list, c_list)
        p_list = [p - pcp for p, pcp in zip(p_list, _pair_dot1_many(pc_list, p_list))]
    for _ in range(_TRI_NEWTON_STEPS):
        lx_list = _pair_dot3_many(lmats, p_list)
        res_list = [(eye - p) - lx for p, lx in zip(p_list, lx_list)]
        p_list = [p + pr for p, pr in zip(p_list, _pair_dot1_many(p_list, res_list))]
    return p_list


def _delta_prep_kernel(q_ref, k_ref, v_ref, gates_ref, gt_ref, *out_refs):
    cs = DELTA_CHUNK
    nc = q_ref.shape[1] // cs
    n_slabs = q_ref.shape[2] // LANES
    r = lax.broadcasted_iota(jnp.int32, (cs, LANES), 0)
    lane = lax.broadcasted_iota(jnp.int32, (cs, LANES), 1)
    c = jnp.bitwise_and(lane, HEAD_DIM - 1)
    low = lane < HEAD_DIM
    er = lax.broadcasted_iota(jnp.int32, (LANES, LANES), 0)
    ec = lax.broadcasted_iota(jnp.int32, (LANES, LANES), 1)
    eye_bf = jnp.where(er == ec, 1.0, 0.0).astype(BF16)
    incl = (r >= c, r <= c)
    strict = (r > c, r < c)

    gates = gates_ref[0]
    gt = gt_ref[0]

    pairs = [(ci, s) for ci in range(nc) for s in range(n_slabs)]
    probs = [(ci, s, d) for d in range(2) for ci in range(nc) for s in range(n_slabs)]

    def tile(ref, ci, s):
        return ref[0, ci * cs:(ci + 1) * cs, s * LANES:(s + 1) * LANES]

    q = {p: tile(q_ref, *p) for p in pairs}
    k = {p: tile(k_ref, *p) for p in pairs}
    v = {p: tile(v_ref, *p) for p in pairs}
    k_bf = {p: k[p].astype(BF16) for p in pairs}
    kq = {p: _dot_nt(jnp.concatenate([k_bf[p], q[p].astype(BF16)], axis=0), _pair_blockdiag(k_bf[p]))
          for p in pairs}
    kk = {p: kq[p][0:cs] for p in pairs}
    qk = {p: kq[p][cs:2 * cs] for p in pairs}

    beta, gc, decay = {}, {}, {}
    for (ci, s, d) in probs:
        rows = slice(ci * cs, (ci + 1) * cs)
        col0 = 4 * d + 2 * s
        beta[ci, s, d] = jnp.where(low, gates[rows, col0:col0 + 1], gates[rows, col0 + 1:col0 + 2])
        gc_col = jnp.where(low, gates[rows, 8 + col0:9 + col0], gates[rows, 9 + col0:10 + col0])
        gc_row = jnp.concatenate([gt[8 + col0:9 + col0, rows], gt[9 + col0:10 + col0, rows]], axis=1)
        gc[ci, s, d] = gc_col
        decay[ci, s, d] = jnp.exp(jnp.where(incl[d], gc_col - gc_row, -jnp.inf))

    lmats = [jnp.where(strict[d], beta[ci, s, d] * kk[ci, s] * decay[ci, s, d], 0.0)
             for (ci, s, d) in probs]
    t_inv = [t.astype(BF16) for t in _pair_tri_inverse_many(lmats, r, c)]

    e = {p: jnp.exp(gc[p]) for p in probs}
    gc_last = {(ci, s, d): (gc[ci, s, d][0:1] if d == 1 else gc[ci, s, d][cs - 1:cs])
               for (ci, s, d) in probs}
    uw = {p: _dot(t, jnp.concatenate(
        [_pair_blockdiag((v[p[0], p[1]] * beta[p]).astype(BF16)),
         _pair_blockdiag((k[p[0], p[1]] * (beta[p] * e[p])).astype(BF16))], axis=1))
        for p, t in zip(probs, t_inv)}
    u = {p: uw[p][:, 0:LANES] for p in probs}
    w = {p: uw[p][:, LANES:2 * LANES] for p in probs}
    kdec_t = {p: _dot_nt(eye_bf, (k[p[0], p[1]] * jnp.exp(gc_last[p] - gc[p])).astype(BF16))
              for p in probs}

    for d in range(2):
        u_ref, w_ref, qg_ref, aqk_ref, kdt_ref, gl_ref = out_refs[6 * d:6 * d + 6]
        for (ci, s) in pairs:
            rows = slice(ci * cs, (ci + 1) * cs)
            lanes = slice(s * LANES, (s + 1) * LANES)
            u_ref[0, rows, lanes] = u[ci, s, d]
            w_ref[0, rows, lanes] = w[ci, s, d].astype(BF16)
            qg_ref[0, rows, lanes] = (q[ci, s] * e[ci, s, d]).astype(BF16)
            aqk_ref[0, rows, lanes] = (qk[ci, s] * decay[ci, s, d]).astype(BF16)
            gl_ref[0, 0, ci:ci + 1, lanes] = jnp.exp(gc_last[ci, s, d])
        for s in range(n_slabs):
            kdt_ref[0, 0, s * LANES:(s + 1) * LANES, :] = jnp.concatenate(
                [kdec_t[ci, s, d] for ci in range(nc)], axis=1).astype(BF16)
        gl_ref[0, 0, nc:SUBLANES, :] = jnp.zeros((SUBLANES - nc, gl_ref.shape[3]), F32)


def _delta_prep(q, k, v, gates, gates_t):
    bsz, seq, width = q.shape
    ta = min(DELTA_TILE, seq)
    nt = seq // ta
    main = lambda b, i: (b, i, 0)
    per_dir_shapes = [jax.ShapeDtypeStruct((bsz, seq, width), F32),
                      jax.ShapeDtypeStruct((bsz, seq, width), BF16),
                      jax.ShapeDtypeStruct((bsz, seq, width), BF16),
                      jax.ShapeDtypeStruct((bsz, seq, width), BF16),
                      jax.ShapeDtypeStruct((bsz, nt, width, ta), BF16),
                      jax.ShapeDtypeStruct((bsz, nt, SUBLANES, width), F32)]
    per_dir_specs = [pl.BlockSpec((1, ta, width), main)] * 4 + [
        pl.BlockSpec((1, 1, width, ta), lambda b, i: (b, i, 0, 0)),
        pl.BlockSpec((1, 1, SUBLANES, width), lambda b, i: (b, i, 0, 0))]
    return pl.pallas_call(
        _delta_prep_kernel,
        out_shape=per_dir_shapes * 2,
        grid=(bsz, nt),
        in_specs=[pl.BlockSpec((1, ta, width), main)] * 3 + [
            pl.BlockSpec((1, ta, LANES), main),
            pl.BlockSpec((1, 2 * SUBLANES, ta), lambda b, i: (b, 0, i))],
        out_specs=per_dir_specs * 2,
        compiler_params=_cparams(("arbitrary", "arbitrary"), big=True),
        name="delta_prep",
    )(q, k, v, gates, gates_t)


DELTA_SCAN_BATCH = 4


def _delta_scan_kernel(*refs):
    ins = (refs[0:6], refs[6:12])
    s0_ref = refs[12]
    outs = refs[13:15]
    send_ref, s_scr = refs[15:17]
    t = pl.program_id(1)
    cs = DELTA_CHUNK
    hd = HEAD_DIM
    nb, ta, width = ins[0][0].shape
    nc = ta // cs
    heads = width // hd
    rr = lax.broadcasted_iota(jnp.int32, (width, width), 0)
    cc = lax.broadcasted_iota(jnp.int32, (width, width), 1)
    same_head = lax.shift_right_logical(rr, 6) == lax.shift_right_logical(cc, 6)
    lane_head = lax.shift_right_logical(lax.broadcasted_iota(jnp.int32, (cs, width), 1), 6)
    zeros = jnp.zeros((cs, width), BF16)

    chains = [(bi, d) for bi in range(nb) for d in range(2)]

    @pl.when(t == 0)
    def _():
        blank = jnp.zeros((hd, hd), F32)
        for (bi, d) in chains:
            s_scr[bi, d] = jnp.concatenate(
                [jnp.concatenate([s0_ref[bi, d, h] if g == h else blank for g in range(heads)], axis=1)
                 for h in range(heads)], axis=0)

    for step in range(nc):
        chunk = (step, nc - 1 - step)
        rows = [slice(chunk[d] * cs, (chunk[d] + 1) * cs) for d in range(2)]
        s_prev = {(bi, d): s_scr[bi, d] for (bi, d) in chains}
        lhs = {(bi, d): jnp.concatenate([ins[d][1][bi, rows[d], :], ins[d][2][bi, rows[d], :]], axis=0)
               for (bi, d) in chains}
        res = {ch: _dot(lhs[ch], s_prev[ch].astype(BF16)) for ch in chains}
        v_new = {(bi, d): (ins[d][0][bi, rows[d], :] - res[bi, d][0:cs]).astype(BF16)
                 for (bi, d) in chains}
        v_bd = {ch: jnp.concatenate([jnp.where(lane_head == h, v_new[ch], zeros)
                                     for h in range(heads)], axis=0) for ch in chains}
        v_tile = {(bi, d): jnp.concatenate(
            [v_new[bi, d] if ci == chunk[d] else zeros for ci in range(nc)], axis=0)
            for (bi, d) in chains}
        o = {(bi, d): res[bi, d][cs:2 * cs] + _dot(ins[d][3][bi, rows[d], :], v_bd[bi, d])
             for (bi, d) in chains}
        upd = {(bi, d): _dot(ins[d][4][bi, 0], v_tile[bi, d]) for (bi, d) in chains}
        for (bi, d) in chains:
            gl = ins[d][5][bi, 0, chunk[d]:chunk[d] + 1, :]
            s_scr[bi, d] = s_prev[bi, d] * gl + jnp.where(same_head, upd[bi, d], 0.0)
            outs[d][bi, rows[d], :] = o[bi, d]

    @pl.when(t == pl.num_programs(1) - 1)
    def _():
        for (bi, d) in chains:
            for h in range(heads):
                send_ref[bi, d, h] = s_scr[bi, d, h * hd:(h + 1) * hd, h * hd:(h + 1) * hd]


def _delta_scan(prep, s0):
    u_f = prep[0]
    bsz, seq, width = u_f.shape
    ta = min(DELTA_TILE, seq)
    nt = seq // ta
    nb = DELTA_SCAN_BATCH if bsz % DELTA_SCAN_BATCH == 0 else 1

    def specs(index3, index4):
        return [pl.BlockSpec((nb, ta, width), index3)] * 4 + [
            pl.BlockSpec((nb, 1, width, ta), index4),
            pl.BlockSpec((nb, 1, SUBLANES, width), index4)]

    fwd3 = lambda b, t: (b, t, 0)
    fwd4 = lambda b, t: (b, t, 0, 0)
    bwd3 = lambda b, t: (b, nt - 1 - t, 0)
    bwd4 = lambda b, t: (b, nt - 1 - t, 0, 0)
    heads = width // HEAD_DIM
    state_spec = pl.BlockSpec((nb, 2, heads, HEAD_DIM, HEAD_DIM), lambda b, t: (b, 0, 0, 0, 0))
    return pl.pallas_call(
        _delta_scan_kernel,
        out_shape=[jax.ShapeDtypeStruct((bsz, seq, width), F32),
                   jax.ShapeDtypeStruct((bsz, seq, width), F32),
                   jax.ShapeDtypeStruct((bsz, 2, heads, HEAD_DIM, HEAD_DIM), F32)],
        grid=(bsz // nb, nt),
        in_specs=specs(fwd3, fwd4) + specs(bwd3, bwd4) + [state_spec],
        out_specs=[pl.BlockSpec((nb, ta, width), fwd3), pl.BlockSpec((nb, ta, width), bwd3),
                   state_spec],
        scratch_shapes=[pltpu.VMEM((nb, 2, width, width), F32)],
        compiler_params=_cparams(("arbitrary", "arbitrary")),
        name="delta_scan",
    )(*prep, s0.astype(F32))


def _deltanet(q, k, v, gates, gates_t, s0):
    prep = _delta_prep(q, k, v, gates, gates_t)
    return _delta_scan(prep, s0)


def _fourier_pos_kernel(w_ref, a1_ref, b1_ref, a0_ref, b0_ref, o_ref, c_scr, s_scr, *, scale):
    i = pl.program_id(1)
    tm = c_scr.shape[0]
    sub = a0_ref.shape[0]
    a0 = a0_ref[...]
    b0 = b0_ref[...]
    for rblk in range(tm // sub):
        j1 = i * (tm // sub) + rblk
        a1 = a1_ref[pl.ds(j1, 1), :]
        b1 = b1_ref[pl.ds(j1, 1), :]
        c_scr[rblk * sub:(rblk + 1) * sub, :] = (a1 * a0 - b1 * b0).astype(BF16)
        s_scr[rblk * sub:(rblk + 1) * sub, :] = (b1 * a0 + a1 * b0).astype(BF16)
    res = (_dot(c_scr[...], w_ref[0]) - _dot(s_scr[...], w_ref[1])) * scale
    gw = o_ref.shape[2]
    for b in range(o_ref.shape[0]):
        o_ref[b] = res[:, b * gw:(b + 1) * gw].astype(BF16)


def _fourier_positions(w, tabs, bsz, gw):
    _, seq, ncol = w.shape
    a1, b1, a0, b0 = tabs
    tm = min(512, seq)
    nsplit = 2 if (bsz % 2 == 0 and seq > 1024) else 1
    bpart = bsz // nsplit
    const2 = lambda p, i: (0, 0)
    single = pl.Buffered(1)
    scale = 1.0 / np.sqrt(float(seq) * HEAD_DIM)
    return pl.pallas_call(
        functools.partial(_fourier_pos_kernel, scale=scale),
        out_shape=jax.ShapeDtypeStruct((bsz, seq, gw), BF16),
        grid=(nsplit, seq // tm),
        in_specs=[pl.BlockSpec((2, seq, bpart * gw), lambda p, i: (0, 0, p), pipeline_mode=single),
                  pl.BlockSpec(a1.shape, const2, pipeline_mode=single),
                  pl.BlockSpec(b1.shape, const2, pipeline_mode=single),
                  pl.BlockSpec(a0.shape, const2, pipeline_mode=single),
                  pl.BlockSpec(b0.shape, const2, pipeline_mode=single)],
        out_specs=pl.BlockSpec((bpart, tm, gw), lambda p, i: (p, i, 0)),
        scratch_shapes=[pltpu.VMEM((tm, seq), BF16), pltpu.VMEM((tm, seq), BF16)],
        compiler_params=_cparams(("arbitrary", "arbitrary"), big=True),
        name="fourier_positions",
    )(w, a1, b1, a0, b0)


def _rope_tables(seq):
    rows = seq // GRID_W
    r, cl = jnp.meshgrid(jnp.arange(rows), jnp.arange(GRID_W), indexing='ij')
    pos_r = r.reshape(-1).astype(F32)
    pos_c = cl.reshape(-1).astype(F32)
    quarter = HEAD_DIM // 4
    inv_freq = ROPE_THETA ** (-jnp.arange(quarter, dtype=F32) / quarter)
    cos_parts, sin_parts = [], []
    for pos in (pos_r, pos_c):
        ang = pos[:, None] * inv_freq[None, :]
        cos_parts += [jnp.cos(ang), jnp.cos(ang)]
        sin_parts += [-jnp.sin(ang), jnp.sin(ang)]
    cos = jnp.concatenate(cos_parts, axis=-1)
    sin = jnp.concatenate(sin_parts, axis=-1)
    return jnp.tile(cos, (1, 2)), jnp.tile(sin, (1, 2))


def _trig_table(num_rows, row_mult, seq):
    j = np.arange(num_rows, dtype=np.int64)[:, None] * row_mult
    k = np.arange(seq, dtype=np.int64)[None, :]
    ang = 2.0 * np.pi * ((j * k) % seq).astype(np.float64) / seq
    return jnp.asarray(np.cos(ang), F32), jnp.asarray(np.sin(ang), F32)


def _fourier_tables(seq):
    sub = 64
    a1, b1 = _trig_table(seq // sub, sub, seq)
    a0, b0 = _trig_table(sub, 1, seq)
    idx = np.arange(GROUP_WIDTH)
    same = (idx[:, None] // HEAD_DIM) == (idx[None, :] // HEAD_DIM)
    ang = 2.0 * np.pi * ((idx[:, None] % HEAD_DIM) * (idx[None, :] % HEAD_DIM) % HEAD_DIM) / HEAD_DIM
    c_bd = jnp.asarray(np.where(same, np.cos(ang), 0.0), F32)
    s_bd = jnp.asarray(np.where(same, np.sin(ang), 0.0), F32)
    return (a1, b1, a0, b0), c_bd, s_bd


def _to_lane_dense(t):
    b, hh, l, d = t.shape
    return jnp.swapaxes(t, 1, 2).reshape(b, l, hh * d)


def _to_head_major(t, heads):
    b, l, w = t.shape
    return jnp.swapaxes(t.reshape(b, l, heads, w // heads), 1, 2)


def _swap_halves(t):
    return jnp.concatenate([t[..., HEAD_DIM:], t[..., :HEAD_DIM]], axis=-1)


def _trunk_layer(x, mod, wts, consts, ctx, final_g):
    bsz, seq, _ = x.shape
    x1 = _ffn(x, mod, wts['norm_ffn1'], wts['ffn1_w1'], wts['ffn1_w3'], wts['ffn1_w2'], which=0)
    rope_tabs = consts['rope'] if ctx is not None else None
    proj = _inproj(x1, mod, wts['norm_mix'], wts['w_in'], consts['ones_bd'], wts['gq'], wts['gk'],
                   consts['c_bd'], consts['s_bd'], rope_tabs, ctx is None)
    q, ka, kb, va, vb, pb, pc, ps, wd = proj[:9]

    k_cache = v_cache = None
    if ctx is None:
        k_cache = _to_head_major(proj[9], KV_HEADS_A)
        v_cache = _to_head_major(proj[10], KV_HEADS_A)
        s0 = jnp.zeros((bsz, 2, N_HEADS_C, HEAD_DIM, HEAD_DIM), F32)
    else:
        k_ctx, v_ctx, s0 = ctx
        k_past = _to_lane_dense(k_ctx).astype(BF16)
        v_past = _to_lane_dense(v_ctx).astype(BF16)
        ka = jnp.concatenate([k_past, ka], axis=1)
        kb = jnp.concatenate([_swap_halves(k_past), kb], axis=1)
        ones = jnp.ones_like(v_past)
        va = jnp.concatenate([jnp.concatenate([v_past, ones], axis=-1), va], axis=1)
        vb = jnp.concatenate([jnp.concatenate([_swap_halves(v_past), ones], axis=-1), vb], axis=1)
    o_a = _attention(q, ka, kb, va, vb)

    o_b, qc, kc, vc, gates, gates_t = _conv_prep(pb, pc, ps, wts['conv_b_w'], wts['conv_c_w'],
                                                 consts['ones_bd'], wts['alog_row'], wts['dtb_row'])
    d_f, d_b, s_end = _deltanet(qc, kc, vc, gates, gates_t, s0)

    o_d = _fourier_positions(wd, consts['fourier'], bsz, GROUP_WIDTH)

    x3 = _ffn(x1, mod, wts['norm_ffn2'], wts['ffn2_w1'], wts['ffn2_w3'], wts['ffn2_w2'], which=2,
              mix=(o_a, o_b, (d_f, d_b, pc, consts['ones_bd'], wts['delta_norm']), o_d),
              w_out=wts['w_out'], final_g=final_g)
    return x3, k_cache, v_cache, s_end


def _lane_row(values, start):
    row = jnp.zeros((1, LANES), F32)
    return row.at[0, start:start + values.shape[0]].set(values.astype(F32))


def kernel(x_prompt, x_sample, c, cache_k, cache_v, state_delta, c_ctx, mod_w, mod_b, norm_ffn1, norm_mix, norm_ffn2, ffn1_w1, ffn1_w3, ffn1_w2, ffn2_w1, ffn2_w3, ffn2_w2, w_in, w_out, q_norm, k_norm, conv_b_w, conv_c_w, delta_a_log, delta_dt_bias, delta_norm, final_norm):
    depth = mod_w.shape[0]
    d_model = x_prompt.shape[-1]
    dec_b = x_sample.shape[0]
    assert w_in.shape[-1] == PROJ_MAIN + PROJ_SMALL + GROUP_WIDTH

    n_cond = 1 + dec_b
    rows = -(-n_cond // SUBLANES) * SUBLANES
    cond = jnp.zeros((rows, d_model), F32).at[0].set(c_ctx).at[1:n_cond].set(c)

    idx = np.arange(LANES)
    ones_bd = jnp.asarray((idx[:, None] // HEAD_DIM) == (idx[None, :] // HEAD_DIM), BF16)
    consts_p = {'ones_bd': ones_bd}
    consts_s = {'ones_bd': ones_bd, 'rope': _rope_tables(x_sample.shape[1])}
    consts_p['fourier'], consts_p['c_bd'], consts_p['s_bd'] = _fourier_tables(x_prompt.shape[1])
    consts_s['fourier'], consts_s['c_bd'], consts_s['s_bd'] = _fourier_tables(x_sample.shape[1])

    yp, ys = x_prompt, x_sample
    k_list, v_list, s_list = [], [], []
    for l in range(depth):
        w_in_l = w_in[l]
        small = w_in_l[:, PROJ_MAIN:PROJ_MAIN + PROJ_SMALL]
        w_in_r = jnp.concatenate(
            [w_in_l[:, :PROJ_MAIN], w_in_l[:, PROJ_MAIN + PROJ_SMALL:], small,
             jnp.zeros((d_model, LANES - PROJ_SMALL), F32)], axis=1).astype(BF16)
        wts = {
            'norm_ffn1': norm_ffn1[l], 'norm_mix': norm_mix[l], 'norm_ffn2': norm_ffn2[l],
            'ffn1_w1': ffn1_w1[l].astype(BF16), 'ffn1_w3': ffn1_w3[l].astype(BF16),
            'ffn1_w2': ffn1_w2[l].astype(BF16),
            'ffn2_w1': ffn2_w1[l].astype(BF16), 'ffn2_w3': ffn2_w3[l].astype(BF16),
            'ffn2_w2': ffn2_w2[l].astype(BF16),
            'w_in': w_in_r, 'w_out': w_out[l].astype(BF16),
            'gq': jnp.tile(q_norm[l], 2).reshape(1, LANES),
            'gk': jnp.tile(k_norm[l], 2).reshape(1, LANES),
            'conv_b_w': conv_b_w[l], 'conv_c_w': conv_c_w[l],
            'alog_row': _lane_row(delta_a_log[l].reshape(-1), 8),
            'dtb_row': _lane_row(delta_dt_bias[l].reshape(-1), 8),
            'delta_norm': delta_norm[l],
        }
        mod = _modulation(cond, mod_w, mod_b, l).reshape(rows, N_MOD, d_model)
        fin = final_norm if l == depth - 1 else None
        yp, k_l, v_l, s_l = _trunk_layer(yp, mod[0:1], wts, consts_p, None, fin)
        k_list.append(k_l)
        v_list.append(v_l)
        s_list.append(s_l)
        ys, _, _, _ = _trunk_layer(ys, mod[1:n_cond], wts, consts_s,
                                   (cache_k[:, l], cache_v[:, l], state_delta[:, l]), fin)
    return (yp, ys, jnp.stack(k_list, axis=1), jnp.stack(v_list, axis=1),
            jnp.stack(s_list, axis=1))
```

```python
import functools

import jax
import jax.numpy as jnp
import numpy as np
from jax import lax
from jax.experimental import pallas as pl
from jax.experimental.pallas import tpu as pltpu

F32 = jnp.float32
BF16 = jnp.bfloat16

HEAD_DIM = 64
N_HEADS_A = 4
KV_HEADS_A = 2
N_HEADS_C = 4
GROUP_WIDTH = 256
GRID_W = 64
DELTA_CHUNK = 64
ROPE_THETA = 10000.0
N_MOD = 9
EPS = 1e-6
PROJ_MAIN = 2304
PROJ_SMALL = 16
LANES = 128
SUBLANES = 8
VMEM_LIMIT_BYTES = 56 * 1024 * 1024


def _cparams(sem, big=False):
    return pltpu.CompilerParams(
        dimension_semantics=sem,
        vmem_limit_bytes=VMEM_LIMIT_BYTES if big else None)


def _dot(a, b):
    return jnp.dot(a, b, preferred_element_type=F32)


def _dot_nt(a, b):
    return lax.dot_general(a, b, (((1,), (1,)), ((), ())), preferred_element_type=F32)


def _split2(x):
    hi = x.astype(BF16)
    lo = (x - hi.astype(F32)).astype(BF16)
    return hi, lo


def _split3(x):
    hi = x.astype(BF16)
    r = x - hi.astype(F32)
    mid = r.astype(BF16)
    lo = (r - mid.astype(F32)).astype(BF16)
    return hi, mid, lo


def _rms(x, g):
    return x * lax.rsqrt(jnp.mean(x * x, axis=-1, keepdims=True) + EPS) * g


def _silu(x):
    return x * jax.nn.sigmoid(x)


def _group_sumsq(x, ones_bd):
    hi, lo = _split2(x * x)
    return _dot(hi, ones_bd) + _dot(lo, ones_bd)


def _mod_kernel(c_ref, w_ref, b_ref, o_ref):
    s = _silu(c_ref[...]).astype(BF16)
    o_ref[...] = _dot(s, w_ref[0].astype(BF16)) + b_ref[0]


def _modulation(cond, mod_w, mod_b, layer):
    rows, d = cond.shape
    depth, _, n = mod_w.shape
    tn = d
    return pl.pallas_call(
        _mod_kernel,
        out_shape=jax.ShapeDtypeStruct((rows, n), F32),
        grid=(n // tn,),
        in_specs=[pl.BlockSpec((rows, d), lambda j: (0, 0)),
                  pl.BlockSpec((1, d, tn), lambda j: (layer, 0, j)),
                  pl.BlockSpec((1, 1, tn), lambda j: (layer, 0, j))],
        out_specs=pl.BlockSpec((rows, tn), lambda j: (0, j)),
        compiler_params=_cparams(("arbitrary",)),
        name="modulation",
    )(cond, mod_w, mod_b.reshape(depth, 1, n))


def _ffn_chunks(f):
    step = 1024
    return [(s, min(s + step, f)) for s in range(0, f, step)]


def _gated_delta_out(of_ref, ob_ref, z_ref, ones_ref, g_ref):
    ones_bd = ones_ref[...]
    parts = []
    for s in range(of_ref.shape[2] // LANES):
        lanes = slice(s * LANES, (s + 1) * LANES)
        o = of_ref[0, :, lanes] + ob_ref[0, :, lanes]
        ms = _group_sumsq(o, ones_bd) * (1.0 / HEAD_DIM)
        y = o * lax.rsqrt(ms + EPS) * g_ref[...]
        parts.append((y * _silu(z_ref[0, :, lanes])).astype(BF16))
    return jnp.concatenate(parts, axis=-1)


def _ffn_kernel(*refs, which, n_mix, final):
    x_ref, mod_ref, g_ref, w1_ref, w3_ref, w2_ref = refs[:6]
    pos = 6
    mix_vals = []
    wout_ref = None
    if n_mix:
        oa_ref, ob_ref, dof_ref, dob_ref, z_ref, ones_ref, dg_ref, od_ref, wout_ref = refs[pos:pos + 9]
        pos += 9
        mix_vals = [oa_ref[0], ob_ref[0],
                    _gated_delta_out(dof_ref, dob_ref, z_ref, ones_ref, dg_ref), od_ref[0]]
    gf_ref = None
    if final:
        gf_ref = refs[pos]
        pos += 1
    o_ref = refs[pos]

    x = x_ref[0]
    mod = mod_ref[0]
    if n_mix:
        acc = None
        for i, m in enumerate(mix_vals):
            w = m.shape[-1]
            part = _dot(m, wout_ref[i * w:(i + 1) * w, :])
            acc = part if acc is None else acc + part
        x = x + mod[5:6] * acc
    sh = mod[3 * which:3 * which + 1]
    sc = mod[3 * which + 1:3 * which + 2]
    gt = mod[3 * which + 2:3 * which + 3]
    h = (_rms(x, g_ref[...]) * (1.0 + sc) + sh).astype(BF16)
    out = None
    for s, e in _ffn_chunks(w1_ref.shape[1]):
        a = _dot(h, w1_ref[:, s:e])
        b = _dot(h, w3_ref[:, s:e])
        act = (_silu(a) * b).astype(BF16)
        part = _dot(act, w2_ref[s:e, :])
        out = part if out is None else out + part
    xn = x + 0.5 * gt * out
    if final:
        xn = _rms(xn, gf_ref[...])
    o_ref[0] = xn


def _ffn(x, mod, g, w1, w3, w2, *, which, mix=None, w_out=None, final_g=None):
    bsz, seq, d = x.shape
    f = w1.shape[1]
    tm = min(512, seq)
    per_batch_mod = mod.shape[0] > 1
    mod_map = (lambda b, i: (b, 0, 0)) if per_batch_mod else (lambda b, i: (0, 0, 0))
    const2 = lambda b, i: (0, 0)
    single = pl.Buffered(1)
    args = [x, mod, g.reshape(1, d), w1, w3, w2]
    specs = [pl.BlockSpec((1, tm, d), lambda b, i: (b, i, 0)),
             pl.BlockSpec((1, N_MOD, d), mod_map),
             pl.BlockSpec((1, d), const2),
             pl.BlockSpec((d, f), const2, pipeline_mode=single),
             pl.BlockSpec((d, f), const2, pipeline_mode=single),
             pl.BlockSpec((f, d), const2, pipeline_mode=single)]
    n_mix = 0
    if mix is not None:
        o_a, o_b, (d_f, d_b, pc, ones_bd, delta_g), o_d = mix
        n_mix = 4
        tok = lambda b, i: (b, i, 0)
        gw = o_a.shape[-1]
        z_block = pc.shape[2] // gw - 1
        args += [o_a, o_b, d_f, d_b, pc, ones_bd,
                 jnp.tile(delta_g, LANES // HEAD_DIM).reshape(1, LANES), o_d, w_out]
        specs += [pl.BlockSpec((1, tm, gw), tok), pl.BlockSpec((1, tm, gw), tok),
                  pl.BlockSpec((1, tm, gw), tok), pl.BlockSpec((1, tm, gw), tok),
                  pl.BlockSpec((1, tm, gw), lambda b, i: (b, i, z_block)),
                  pl.BlockSpec((LANES, LANES), const2), pl.BlockSpec((1, LANES), const2),
                  pl.BlockSpec((1, tm, gw), tok),
                  pl.BlockSpec(w_out.shape, const2, pipeline_mode=single)]
    if final_g is not None:
        args.append(final_g.reshape(1, d))
        specs.append(pl.BlockSpec((1, d), const2))
    return pl.pallas_call(
        functools.partial(_ffn_kernel, which=which, n_mix=n_mix, final=final_g is not None),
        out_shape=jax.ShapeDtypeStruct((bsz, seq, d), F32),
        grid=(bsz, seq // tm),
        in_specs=specs,
        out_specs=pl.BlockSpec((1, tm, d), lambda b, i: (b, i, 0)),
        compiler_params=_cparams(("arbitrary", "arbitrary"), big=True),
        name="ffn",
    )(*args)


_PROJ_A = (0, 512)
_PROJ_B = (512, 1280)
_PROJ_C = (1280, 2304)
_PROJ_D = (2304, 2560)
_PROJ_S = (2560, 2688)


def _rope(x, cos, sin_signed):
    lane = lax.broadcasted_iota(jnp.int32, x.shape, 1)
    up = pltpu.roll(x, LANES - 16, axis=1)
    down = pltpu.roll(x, 16, axis=1)
    partner = jnp.where((lane % 32) < 16, up, down)
    return x * cos + partner * sin_signed


def _inproj_kernel(*refs, rope, want_cache):
    x_ref, mod_ref, g_ref, w_ref, ones_ref, gq_ref, gk_ref, cbd_ref, sbd_ref = refs[:9]
    pos = 9
    if rope:
        cos_ref, sin_ref = refs[pos:pos + 2]
        pos += 2
    q_ref, ka_ref, kb_ref, va_ref, vb_ref, pb_ref, pc_ref, ps_ref, wd_ref = refs[pos:pos + 9]
    pos += 9
    mod = mod_ref[0]
    h = (_rms(x_ref[0], g_ref[...]) * (1.0 + mod[4:5]) + mod[3:4]).astype(BF16)

    def proj(cols):
        return _dot(h, w_ref[:, cols[0]:cols[1]])

    pa = proj(_PROJ_A)
    ones_bd = ones_ref[...]
    sumsq = [_group_sumsq(pa[:, s * LANES:(s + 1) * LANES], ones_bd) for s in range(3)]

    pb_ref[0] = proj(_PROJ_B)
    pc_ref[0] = proj(_PROJ_C)
    ps_ref[0] = proj(_PROJ_S)

    xd = proj(_PROJ_D).astype(BF16)
    wd_ref[0] = _dot(xd, cbd_ref[...].astype(BF16)).astype(BF16)
    wd_ref[1] = _dot(xd, sbd_ref[...].astype(BF16)).astype(BF16)

    def normed(s, g):
        x = pa[:, s * LANES:(s + 1) * LANES]
        return x * lax.rsqrt(sumsq[s] * (1.0 / HEAD_DIM) + EPS) * g

    q0 = normed(0, gq_ref[...])
    q1 = normed(1, gq_ref[...])
    k = normed(2, gk_ref[...])
    v = pa[:, 384:512]
    if want_cache:
        refs[pos][0] = k
        refs[pos + 1][0] = v
    if rope:
        cos = cos_ref[...]
        sin = sin_ref[...]
        q0 = _rope(q0, cos, sin)
        q1 = _rope(q1, cos, sin)
        k = _rope(k, cos, sin)
    scale = HEAD_DIM ** -0.5
    q_ref[0, :, 0:128] = (q0 * scale).astype(BF16)
    q_ref[0, :, 128:256] = (q1 * scale).astype(BF16)
    ka_ref[0] = k.astype(BF16)
    kb_ref[0] = pltpu.roll(k, HEAD_DIM, axis=1).astype(BF16)
    ones = jnp.ones(v.shape, BF16)
    va_ref[0, :, 0:LANES] = v.astype(BF16)
    va_ref[0, :, LANES:2 * LANES] = ones
    vb_ref[0, :, 0:LANES] = pltpu.roll(v, HEAD_DIM, axis=1).astype(BF16)
    vb_ref[0, :, LANES:2 * LANES] = ones


def _inproj(x, mod, g, w_in_r, ones_bd, gq, gk, c_bd, s_bd, rope_tabs, want_cache):
    bsz, seq, d = x.shape
    tm = min(512, seq)
    per_batch_mod = mod.shape[0] > 1
    mod_map = (lambda b, i: (b, 0, 0)) if per_batch_mod else (lambda b, i: (0, 0, 0))
    const2 = lambda b, i: (0, 0)
    tok = lambda b, i: (b, i, 0)
    rope = rope_tabs is not None
    gw = GROUP_WIDTH
    args = [x, mod, g.reshape(1, d), w_in_r, ones_bd, gq, gk, c_bd, s_bd]
    specs = [pl.BlockSpec((1, tm, d), tok),
             pl.BlockSpec((1, N_MOD, d), mod_map),
             pl.BlockSpec((1, d), const2),
             pl.BlockSpec(w_in_r.shape, const2, pipeline_mode=pl.Buffered(1)),
             pl.BlockSpec((LANES, LANES), const2),
             pl.BlockSpec((1, LANES), const2),
             pl.BlockSpec((1, LANES), const2),
             pl.BlockSpec((gw, gw), const2),
             pl.BlockSpec((gw, gw), const2)]
    if rope:
        args += list(rope_tabs)
        specs += [pl.BlockSpec((tm, LANES), lambda b, i: (i, 0))] * 2

    def tok_out(width, dtype):
        return jax.ShapeDtypeStruct((bsz, seq, width), dtype), pl.BlockSpec((1, tm, width), tok)

    outs = [tok_out(256, BF16),
            tok_out(LANES, BF16), tok_out(LANES, BF16),
            tok_out(2 * LANES, BF16), tok_out(2 * LANES, BF16),
            tok_out(_PROJ_B[1] - _PROJ_B[0], F32),
            tok_out(_PROJ_C[1] - _PROJ_C[0], F32),
            tok_out(_PROJ_S[1] - _PROJ_S[0], F32),
            (jax.ShapeDtypeStruct((2, seq, bsz * gw), BF16),
             pl.BlockSpec((2, tm, gw), lambda b, i: (0, i, b)))]
    if want_cache:
        outs += [tok_out(LANES, F32), tok_out(LANES, F32)]
    return pl.pallas_call(
        functools.partial(_inproj_kernel, rope=rope, want_cache=want_cache),
        out_shape=[o[0] for o in outs],
        grid=(bsz, seq // tm),
        in_specs=specs,
        out_specs=[o[1] for o in outs],
        compiler_params=_cparams(("arbitrary", "arbitrary"), big=True),
        name="inproj",
    )(*args)


def _attn_kernel(q_ref, ka_ref, kb_ref, va_ref, vb_ref, o_ref):
    tq = q_ref.shape[1]
    lane = lax.broadcasted_iota(jnp.int32, (tq, LANES), 1)
    low = lane < HEAD_DIM
    arrangement = ((ka_ref, va_ref), (kb_ref, vb_ref), (kb_ref, vb_ref), (ka_ref, va_ref))
    def scores(h):
        slab = q_ref[0, :, (h // 2) * LANES:(h // 2 + 1) * LANES]
        keep = low if h % 2 == 0 else jnp.logical_not(low)
        qh = jnp.where(keep, slab, jnp.zeros_like(slab))
        return _dot_nt(qh, arrangement[h][0][0])

    def weighted_values(h, s):
        m = jnp.max(s, axis=-1, keepdims=True)
        p = jnp.exp(s - m)
        pv = _dot(p.astype(BF16), arrangement[h][1][0])
        return pv[:, 0:LANES] / pv[:, LANES:LANES + 1]

    outs = []
    s_next = scores(0)
    for h in range(N_HEADS_A):
        s_cur = s_next
        if h + 1 < N_HEADS_A:
            s_next = scores(h + 1)
        outs.append(weighted_values(h, s_cur))
    o_ref[0, :, 0:128] = jnp.where(low, outs[0], outs[1]).astype(BF16)
    o_ref[0, :, 128:256] = jnp.where(low, outs[2], outs[3]).astype(BF16)


def _attention(q, ka, kb, va, vb):
    bsz, seq, _ = q.shape
    lk = ka.shape[1]
    tq = min(512, seq)
    k_spec = pl.BlockSpec((1, lk, LANES), lambda b, i: (b, 0, 0))
    v_spec = pl.BlockSpec((1, lk, 2 * LANES), lambda b, i: (b, 0, 0))
    return pl.pallas_call(
        _attn_kernel,
        out_shape=jax.ShapeDtypeStruct((bsz, seq, 256), BF16),
        grid=(bsz, seq // tq),
        in_specs=[pl.BlockSpec((1, tq, 256), lambda b, i: (b, i, 0)),
                  k_spec, k_spec, v_spec, v_spec],
        out_specs=pl.BlockSpec((1, tq, 256), lambda b, i: (b, i, 0)),
        compiler_params=_cparams(("arbitrary", "arbitrary"), big=True),
        name="attention",
    )(q, ka, kb, va, vb)


def _conv3(x, prev_row, next_row, w):
    rows = x.shape[0]
    ridx = lax.broadcasted_iota(jnp.int32, x.shape, 0)
    xm = jnp.where(ridx == 0, prev_row, pltpu.roll(x, 1, axis=0))
    xp = jnp.where(ridx == rows - 1, next_row, pltpu.roll(x, rows - 1, axis=0))
    return xm * w[0:1] + x * w[1:2] + xp * w[2:3]


def _conv_prep_kernel(pb_ref, pbp_ref, pbn_ref, pc_ref, pcp_ref, pcn_ref, ps_ref,
                      wb_ref, wc_ref, ones_ref, alog_ref, dtb_ref, tril_ref, triu_ref,
                      ob_ref, q_ref, k_ref, v_ref, gates_ref, gatest_ref):
    i = pl.program_id(1)
    has_prev = i > 0
    has_next = i < pl.num_programs(1) - 1
    gw = GROUP_WIDTH

    pb = pb_ref[0]
    u = pb[:, gw:2 * gw] * pb[:, 2 * gw:3 * gw]
    pbp = pbp_ref[0, SUBLANES - 1:SUBLANES, :]
    pbn = pbn_ref[0, 0:1, :]
    u_prev = jnp.where(has_prev, pbp[:, gw:2 * gw] * pbp[:, 2 * gw:3 * gw], 0.0)
    u_next = jnp.where(has_next, pbn[:, gw:2 * gw] * pbn[:, 2 * gw:3 * gw], 0.0)
    ob_ref[0] = (pb[:, 0:gw] * _conv3(u, u_prev, u_next, wb_ref[...])).astype(BF16)

    pc = pc_ref[0]
    x3 = pc[:, 0:3 * gw]
    x_prev = jnp.where(has_prev, pcp_ref[0, SUBLANES - 1:SUBLANES, 0:3 * gw], 0.0)
    x_next = jnp.where(has_next, pcn_ref[0, 0:1, 0:3 * gw], 0.0)
    qkv = _silu(_conv3(x3, x_prev, x_next, wc_ref[...]))
    ones_bd = ones_ref[...]

    def l2n(x):
        return x * lax.rsqrt(_group_sumsq(x, ones_bd) + EPS)

    q_scale = HEAD_DIM ** -0.5
    for s in range(2):
        q_ref[0, :, s * LANES:(s + 1) * LANES] = l2n(qkv[:, s * LANES:(s + 1) * LANES]) * q_scale
        k_ref[0, :, s * LANES:(s + 1) * LANES] = l2n(qkv[:, gw + s * LANES:gw + (s + 1) * LANES])
    v_ref[0] = qkv[:, 2 * gw:3 * gw]

    raw = ps_ref[0]
    beta = jax.nn.sigmoid(raw)
    y = raw + dtb_ref[...]
    softplus = jnp.maximum(y, 0.0) + jnp.log(1.0 + jnp.exp(-jnp.abs(y)))
    g = -jnp.exp(alog_ref[...]) * softplus
    g_hi, g_mid, g_lo = _split3(g)
    lane = lax.broadcasted_iota(jnp.int32, raw.shape, 1)
    packed = jnp.where(lane < 16, g_hi.astype(F32),
                       jnp.where(lane < 32, pltpu.roll(g_mid.astype(F32), 16, axis=1),
                                 pltpu.roll(g_lo.astype(F32), 32, axis=1))).astype(BF16)

    def cumulative(tri):
        parts = _dot(tri, packed)
        return parts + pltpu.roll(parts, LANES - 16, axis=1) + pltpu.roll(parts, LANES - 32, axis=1)

    gc_f = cumulative(tril_ref[...])
    gc_b = cumulative(triu_ref[...])
    gates = jnp.where(lane < 8, beta, jnp.where(lane < 12, gc_f, gc_b))
    gates_ref[0] = gates
    gatest_ref[0] = jnp.transpose(gates)[0:2 * SUBLANES, :]


def _conv_prep(pb, pc, ps, conv_b_w, conv_c_w, ones_bd, alog_row, dtb_row):
    bsz, seq, _ = pb.shape
    tl = min(512, seq)
    nblk8 = seq // SUBLANES
    r8 = tl // SUBLANES
    main = lambda b, i: (b, i, 0)
    prev = lambda b, i: (b, jnp.maximum(i * r8 - 1, 0), 0)
    nxt = lambda b, i: (b, jnp.minimum((i + 1) * r8, nblk8 - 1), 0)
    const2 = lambda b, i: (0, 0)
    idx = np.arange(tl)
    same_chunk = (idx[:, None] // DELTA_CHUNK) == (idx[None, :] // DELTA_CHUNK)
    tril = jnp.asarray(same_chunk & (idx[:, None] >= idx[None, :]), BF16)
    triu = jnp.asarray(same_chunk & (idx[:, None] <= idx[None, :]), BF16)
    dense = jax.ShapeDtypeStruct((bsz, seq, GROUP_WIDTH), F32)
    dense_spec = pl.BlockSpec((1, tl, GROUP_WIDTH), main)
    return pl.pallas_call(
        _conv_prep_kernel,
        out_shape=[jax.ShapeDtypeStruct((bsz, seq, GROUP_WIDTH), BF16), dense, dense, dense,
                   jax.ShapeDtypeStruct((bsz, seq, LANES), F32),
                   jax.ShapeDtypeStruct((bsz, 2 * SUBLANES, seq), F32)],
        grid=(bsz, seq // tl),
        in_specs=[pl.BlockSpec((1, tl, 768), main),
                  pl.BlockSpec((1, SUBLANES, 768), prev),
                  pl.BlockSpec((1, SUBLANES, 768), nxt),
                  pl.BlockSpec((1, tl, 1024), main),
                  pl.BlockSpec((1, SUBLANES, 1024), prev),
                  pl.BlockSpec((1, SUBLANES, 1024), nxt),
                  pl.BlockSpec((1, tl, LANES), main),
                  pl.BlockSpec((3, GROUP_WIDTH), const2),
                  pl.BlockSpec((3, 3 * GROUP_WIDTH), const2),
                  pl.BlockSpec((LANES, LANES), const2),
                  pl.BlockSpec((1, LANES), const2),
                  pl.BlockSpec((1, LANES), const2),
                  pl.BlockSpec((tl, tl), const2),
                  pl.BlockSpec((tl, tl), const2)],
        out_specs=[pl.BlockSpec((1, tl, GROUP_WIDTH), main), dense_spec, dense_spec, dense_spec,
                   pl.BlockSpec((1, tl, LANES), main),
                   pl.BlockSpec((1, 2 * SUBLANES, tl), lambda b, i: (b, 0, i))],
        compiler_params=_cparams(("arbitrary", "arbitrary"), big=True),
        name="conv_prep",
    )(pb, pb, pb, pc, pc, pc, ps, conv_b_w, conv_c_w, ones_bd, alog_row, dtb_row, tril, triu)


_TRI_BASE_LOG2 = 3
_TRI_NEWTON_STEPS = 1
DELTA_TILE = 4 * DELTA_CHUNK


def _pair_blockdiag(y):
    low = lax.broadcasted_iota(jnp.int32, y.shape, 1) < HEAD_DIM
    zero = jnp.zeros_like(y)
    return jnp.concatenate([jnp.where(low, y, zero), jnp.where(low, zero, y)], axis=0)


def _pair_dot3_many(a_list, b_list):
    n = a_list[0].shape[0]
    lhs = [jnp.concatenate(_split2(a), axis=0) for a in a_list]
    rhs = [jnp.concatenate([_pair_blockdiag(part) for part in _split2(b)], axis=1) for b in b_list]
    res = [_dot(l, rr) for l, rr in zip(lhs, rhs)]
    return [(x[0:n, 0:LANES] + x[n:2 * n, LANES:2 * LANES])
            + (x[0:n, LANES:2 * LANES] + x[n:2 * n, 0:LANES]) for x in res]


def _pair_dot1_many(a_list, b_list):
    return [_dot(a.astype(BF16), _pair_blockdiag(b.astype(BF16))) for a, b in zip(a_list, b_list)]


def _pair_tri_inverse_many(lmats, r, c):
    def same_block(log2_size):
        return lax.shift_right_logical(r, log2_size) == lax.shift_right_logical(c, log2_size)

    base = same_block(_TRI_BASE_LOG2)
    eye = jnp.where(r == c, 1.0, 0.0)
    a_list = [jnp.where(base, m, 0.0) for m in lmats]
    p_list = [eye - a for a in a_list]
    for _ in range(_TRI_BASE_LOG2 - 1):
        a_list = _pair_dot1_many(a_list, a_list)
        p_list = [p + pa for p, pa in zip(p_list, _pair_dot1_many(p_list, a_list))]
    for log2_size in range(_TRI_BASE_LOG2, int(np.log2(DELTA_CHUNK))):
        couple = jnp.logical_and(same_block(log2_size + 1),
                                 jnp.logical_not(same_block(log2_size)))
        c_list = [jnp.where(couple, m, 0.0) for m in lmats]
        pc_list = _pair_dot1_many(p_list, c_list)
        p_list = [p - pcp for p, pcp in zip(p_list, _pair_dot1_many(pc_list, p_list))]
    for _ in range(_TRI_NEWTON_STEPS):
        lx_list = _pair_dot3_many(lmats, p_list)
        res_list = [(eye - p) - lx for p, lx in zip(p_list, lx_list)]
        p_list = [p + pr for p, pr in zip(p_list, _pair_dot1_many(p_list, res_list))]
    return p_list


def _delta_prep_kernel(q_ref, k_ref, v_ref, gates_ref, gt_ref, *out_refs):
    cs = DELTA_CHUNK
    nc = q_ref.shape[1] // cs
    n_slabs = q_ref.shape[2] // LANES
    r = lax.broadcasted_iota(jnp.int32, (cs, LANES), 0)
    lane = lax.broadcasted_iota(jnp.int32, (cs, LANES), 1)
    c = jnp.bitwise_and(lane, HEAD_DIM - 1)
    low = lane < HEAD_DIM
    er = lax.broadcasted_iota(jnp.int32, (LANES, LANES), 0)
    ec = lax.broadcasted_iota(jnp.int32, (LANES, LANES), 1)
    eye_bf = jnp.where(er == ec, 1.0, 0.0).astype(BF16)
    incl = (r >= c, r <= c)
    strict = (r > c, r < c)

    gates = gates_ref[0]
    gt = gt_ref[0]

    pairs = [(ci, s) for ci in range(nc) for s in range(n_slabs)]
    probs = [(ci, s, d) for d in range(2) for ci in range(nc) for s in range(n_slabs)]

    def tile(ref, ci, s):
        return ref[0, ci * cs:(ci + 1) * cs, s * LANES:(s + 1) * LANES]

    q = {p: tile(q_ref, *p) for p in pairs}
    k = {p: tile(k_ref, *p) for p in pairs}
    v = {p: tile(v_ref, *p) for p in pairs}
    k_bf = {p: k[p].astype(BF16) for p in pairs}
    kq = {p: _dot_nt(jnp.concatenate([k_bf[p], q[p].astype(BF16)], axis=0), _pair_blockdiag(k_bf[p]))
          for p in pairs}
    kk = {p: kq[p][0:cs] for p in pairs}
    qk = {p: kq[p][cs:2 * cs] for p in pairs}

    beta, gc, decay = {}, {}, {}
    for (ci, s, d) in probs:
        rows = slice(ci * cs, (ci + 1) * cs)
        col0 = 4 * d + 2 * s
        beta[ci, s, d] = jnp.where(low, gates[rows, col0:col0 + 1], gates[rows, col0 + 1:col0 + 2])
        gc_col = jnp.where(low, gates[rows, 8 + col0:9 + col0], gates[rows, 9 + col0:10 + col0])
        gc_row = jnp.concatenate([gt[8 + col0:9 + col0, rows], gt[9 + col0:10 + col0, rows]], axis=1)
        gc[ci, s, d] = gc_col
        decay[ci, s, d] = jnp.exp(jnp.where(incl[d], gc_col - gc_row, -jnp.inf))

    lmats = [jnp.where(strict[d], beta[ci, s, d] * kk[ci, s] * decay[ci, s, d], 0.0)
             for (ci, s, d) in probs]
    t_inv = [t.astype(BF16) for t in _pair_tri_inverse_many(lmats, r, c)]

    e = {p: jnp.exp(gc[p]) for p in probs}
    gc_last = {(ci, s, d): (gc[ci, s, d][0:1] if d == 1 else gc[ci, s, d][cs - 1:cs])
               for (ci, s, d) in probs}
    uw = {p: _dot(t, jnp.concatenate(
        [_pair_blockdiag((v[p[0], p[1]] * beta[p]).astype(BF16)),
         _pair_blockdiag((k[p[0], p[1]] * (beta[p] * e[p])).astype(BF16))], axis=1))
        for p, t in zip(probs, t_inv)}
    u = {p: uw[p][:, 0:LANES] for p in probs}
    w = {p: uw[p][:, LANES:2 * LANES] for p in probs}
    kdec_t = {p: _dot_nt(eye_bf, (k[p[0], p[1]] * jnp.exp(gc_last[p] - gc[p])).astype(BF16))
              for p in probs}

    for d in range(2):
        u_ref, w_ref, qg_ref, aqk_ref, kdt_ref, gl_ref = out_refs[6 * d:6 * d + 6]
        for (ci, s) in pairs:
            rows = slice(ci * cs, (ci + 1) * cs)
            lanes = slice(s * LANES, (s + 1) * LANES)
            u_ref[0, rows, lanes] = u[ci, s, d]
            w_ref[0, rows, lanes] = w[ci, s, d].astype(BF16)
            qg_ref[0, rows, lanes] = (q[ci, s] * e[ci, s, d]).astype(BF16)
            aqk_ref[0, rows, lanes] = (qk[ci, s] * decay[ci, s, d]).astype(BF16)
            gl_ref[0, 0, ci:ci + 1, lanes] = jnp.exp(gc_last[ci, s, d])
        for s in range(n_slabs):
            kdt_ref[0, 0, s * LANES:(s + 1) * LANES, :] = jnp.concatenate(
                [kdec_t[ci, s, d] for ci in range(nc)], axis=1).astype(BF16)
        gl_ref[0, 0, nc:SUBLANES, :] = jnp.zeros((SUBLANES - nc, gl_ref.shape[3]), F32)


def _delta_prep(q, k, v, gates, gates_t):
    bsz, seq, width = q.shape
    ta = min(DELTA_TILE, seq)
    nt = seq // ta
    main = lambda b, i: (b, i, 0)
    per_dir_shapes = [jax.ShapeDtypeStruct((bsz, seq, width), F32),
                      jax.ShapeDtypeStruct((bsz, seq, width), BF16),
                      jax.ShapeDtypeStruct((bsz, seq, width), BF16),
                      jax.ShapeDtypeStruct((bsz, seq, width), BF16),
                      jax.ShapeDtypeStruct((bsz, nt, width, ta), BF16),
                      jax.ShapeDtypeStruct((bsz, nt, SUBLANES, width), F32)]
    per_dir_specs = [pl.BlockSpec((1, ta, width), main)] * 4 + [
        pl.BlockSpec((1, 1, width, ta), lambda b, i: (b, i, 0, 0)),
        pl.BlockSpec((1, 1, SUBLANES, width), lambda b, i: (b, i, 0, 0))]
    return pl.pallas_call(
        _delta_prep_kernel,
        out_shape=per_dir_shapes * 2,
        grid=(bsz, nt),
        in_specs=[pl.BlockSpec((1, ta, width), main)] * 3 + [
            pl.BlockSpec((1, ta, LANES), main),
            pl.BlockSpec((1, 2 * SUBLANES, ta), lambda b, i: (b, 0, i))],
        out_specs=per_dir_specs * 2,
        compiler_params=_cparams(("arbitrary", "arbitrary"), big=True),
        name="delta_prep",
    )(q, k, v, gates, gates_t)


DELTA_SCAN_BATCH = 4


def _delta_scan_kernel(*refs):
    ins = (refs[0:6], refs[6:12])
    s0_ref = refs[12]
    outs = refs[13:15]
    send_ref, s_scr = refs[15:17]
    t = pl.program_id(1)
    cs = DELTA_CHUNK
    hd = HEAD_DIM
    nb, ta, width = ins[0][0].shape
    nc = ta // cs
    heads = width // hd
    rr = lax.broadcasted_iota(jnp.int32, (width, width), 0)
    cc = lax.broadcasted_iota(jnp.int32, (width, width), 1)
    same_head = lax.shift_right_logical(rr, 6) == lax.shift_right_logical(cc, 6)
    lane_head = lax.shift_right_logical(lax.broadcasted_iota(jnp.int32, (cs, width), 1), 6)
    zeros = jnp.zeros((cs, width), BF16)

    chains = [(bi, d) for bi in range(nb) for d in range(2)]

    @pl.when(t == 0)
    def _():
        blank = jnp.zeros((hd, hd), F32)
        for (bi, d) in chains:
            s_scr[bi, d] = jnp.concatenate(
                [jnp.concatenate([s0_ref[bi, d, h] if g == h else blank for g in range(heads)], axis=1)
                 for h in range(heads)], axis=0)

    for step in range(nc):
        chunk = (step, nc - 1 - step)
        rows = [slice(chunk[d] * cs, (chunk[d] + 1) * cs) for d in range(2)]
        s_prev = {(bi, d): s_scr[bi, d] for (bi, d) in chains}
        lhs = {(bi, d): jnp.concatenate([ins[d][1][bi, rows[d], :], ins[d][2][bi, rows[d], :]], axis=0)
               for (bi, d) in chains}
        res = {ch: _dot(lhs[ch], s_prev[ch].astype(BF16)) for ch in chains}
        v_new = {(bi, d): (ins[d][0][bi, rows[d], :] - res[bi, d][0:cs]).astype(BF16)
                 for (bi, d) in chains}
        v_bd = {ch: jnp.concatenate([jnp.where(lane_head == h, v_new[ch], zeros)
                                     for h in range(heads)], axis=0) for ch in chains}
        v_tile = {(bi, d): jnp.concatenate(
            [v_new[bi, d] if ci == chunk[d] else zeros for ci in range(nc)], axis=0)
            for (bi, d) in chains}
        o = {(bi, d): res[bi, d][cs:2 * cs] + _dot(ins[d][3][bi, rows[d], :], v_bd[bi, d])
             for (bi, d) in chains}
        upd = {(bi, d): _dot(ins[d][4][bi, 0], v_tile[bi, d]) for (bi, d) in chains}
        for (bi, d) in chains:
            gl = ins[d][5][bi, 0, chunk[d]:chunk[d] + 1, :]
            s_scr[bi, d] = s_prev[bi, d] * gl + jnp.where(same_head, upd[bi, d], 0.0)
            outs[d][bi, rows[d], :] = o[bi, d]

    @pl.when(t == pl.num_programs(1) - 1)
    def _():
        for (bi, d) in chains:
            for h in range(heads):
                send_ref[bi, d, h] = s_scr[bi, d, h * hd:(h + 1) * hd, h * hd:(h + 1) * hd]


def _delta_scan(prep, s0):
    u_f = prep[0]
    bsz, seq, width = u_f.shape
    ta = min(DELTA_TILE, seq)
    nt = seq // ta
    nb = DELTA_SCAN_BATCH if bsz % DELTA_SCAN_BATCH == 0 else 1

    def specs(index3, index4):
        return [pl.BlockSpec((nb, ta, width), index3)] * 4 + [
            pl.BlockSpec((nb, 1, width, ta), index4),
            pl.BlockSpec((nb, 1, SUBLANES, width), index4)]

    fwd3 = lambda b, t: (b, t, 0)
    fwd4 = lambda b, t: (b, t, 0, 0)
    bwd3 = lambda b, t: (b, nt - 1 - t, 0)
    bwd4 = lambda b, t: (b, nt - 1 - t, 0, 0)
    heads = width // HEAD_DIM
    state_spec = pl.BlockSpec((nb, 2, heads, HEAD_DIM, HEAD_DIM), lambda b, t: (b, 0, 0, 0, 0))
    return pl.pallas_call(
        _delta_scan_kernel,
        out_shape=[jax.ShapeDtypeStruct((bsz, seq, width), F32),
                   jax.ShapeDtypeStruct((bsz, seq, width), F32),
                   jax.ShapeDtypeStruct((bsz, 2, heads, HEAD_DIM, HEAD_DIM), F32)],
        grid=(bsz // nb, nt),
        in_specs=specs(fwd3, fwd4) + specs(bwd3, bwd4) + [state_spec],
        out_specs=[pl.BlockSpec((nb, ta, width), fwd3), pl.BlockSpec((nb, ta, width), bwd3),
                   state_spec],
        scratch_shapes=[pltpu.VMEM((nb, 2, width, width), F32)],
        compiler_params=_cparams(("arbitrary", "arbitrary")),
        name="delta_scan",
    )(*prep, s0.astype(F32))


def _deltanet(q, k, v, gates, gates_t, s0):
    prep = _delta_prep(q, k, v, gates, gates_t)
    return _delta_scan(prep, s0)


def _fourier_pos_kernel(w_ref, a1_ref, b1_ref, a0_ref, b0_ref, o_ref, c_scr, s_scr, *, scale):
    i = pl.program_id(1)
    tm = c_scr.shape[0]
    sub = a0_ref.shape[0]
    a0 = a0_ref[...]
    b0 = b0_ref[...]
    for rblk in range(tm // sub):
        j1 = i * (tm // sub) + rblk
        a1 = a1_ref[pl.ds(j1, 1), :]
        b1 = b1_ref[pl.ds(j1, 1), :]
        c_scr[rblk * sub:(rblk + 1) * sub, :] = (a1 * a0 - b1 * b0).astype(BF16)
        s_scr[rblk * sub:(rblk + 1) * sub, :] = (b1 * a0 + a1 * b0).astype(BF16)
    res = (_dot(c_scr[...], w_ref[0]) - _dot(s_scr[...], w_ref[1])) * scale
    gw = o_ref.shape[2]
    for b in range(o_ref.shape[0]):
        o_ref[b] = res[:, b * gw:(b + 1) * gw].astype(BF16)


def _fourier_positions(w, tabs, bsz, gw):
    _, seq, ncol = w.shape
    a1, b1, a0, b0 = tabs
    tm = min(512, seq)
    nsplit = 2 if (bsz % 2 == 0 and seq > 1024) else 1
    bpart = bsz // nsplit
    const2 = lambda p, i: (0, 0)
    single = pl.Buffered(1)
    scale = 1.0 / np.sqrt(float(seq) * HEAD_DIM)
    return pl.pallas_call(
        functools.partial(_fourier_pos_kernel, scale=scale),
        out_shape=jax.ShapeDtypeStruct((bsz, seq, gw), BF16),
        grid=(nsplit, seq // tm),
        in_specs=[pl.BlockSpec((2, seq, bpart * gw), lambda p, i: (0, 0, p), pipeline_mode=single),
                  pl.BlockSpec(a1.shape, const2, pipeline_mode=single),
                  pl.BlockSpec(b1.shape, const2, pipeline_mode=single),
                  pl.BlockSpec(a0.shape, const2, pipeline_mode=single),
                  pl.BlockSpec(b0.shape, const2, pipeline_mode=single)],
        out_specs=pl.BlockSpec((bpart, tm, gw), lambda p, i: (p, i, 0)),
        scratch_shapes=[pltpu.VMEM((tm, seq), BF16), pltpu.VMEM((tm, seq), BF16)],
        compiler_params=_cparams(("arbitrary", "arbitrary"), big=True),
        name="fourier_positions",
    )(w, a1, b1, a0, b0)


def _rope_tables(seq):
    rows = seq // GRID_W
    r, cl = jnp.meshgrid(jnp.arange(rows), jnp.arange(GRID_W), indexing='ij')
    pos_r = r.reshape(-1).astype(F32)
    pos_c = cl.reshape(-1).astype(F32)
    quarter = HEAD_DIM // 4
    inv_freq = ROPE_THETA ** (-jnp.arange(quarter, dtype=F32) / quarter)
    cos_parts, sin_parts = [], []
    for pos in (pos_r, pos_c):
        ang = pos[:, None] * inv_freq[None, :]
        cos_parts += [jnp.cos(ang), jnp.cos(ang)]
        sin_parts += [-jnp.sin(ang), jnp.sin(ang)]
    cos = jnp.concatenate(cos_parts, axis=-1)
    sin = jnp.concatenate(sin_parts, axis=-1)
    return jnp.tile(cos, (1, 2)), jnp.tile(sin, (1, 2))


def _trig_table(num_rows, row_mult, seq):
    j = np.arange(num_rows, dtype=np.int64)[:, None] * row_mult
    k = np.arange(seq, dtype=np.int64)[None, :]
    ang = 2.0 * np.pi * ((j * k) % seq).astype(np.float64) / seq
    return jnp.asarray(np.cos(ang), F32), jnp.asarray(np.sin(ang), F32)


def _fourier_tables(seq):
    sub = 64
    a1, b1 = _trig_table(seq // sub, sub, seq)
    a0, b0 = _trig_table(sub, 1, seq)
    idx = np.arange(GROUP_WIDTH)
    same = (idx[:, None] // HEAD_DIM) == (idx[None, :] // HEAD_DIM)
    ang = 2.0 * np.pi * ((idx[:, None] % HEAD_DIM) * (idx[None, :] % HEAD_DIM) % HEAD_DIM) / HEAD_DIM
    c_bd = jnp.asarray(np.where(same, np.cos(ang), 0.0), F32)
    s_bd = jnp.asarray(np.where(same, np.sin(ang), 0.0), F32)
    return (a1, b1, a0, b0), c_bd, s_bd


def _to_lane_dense(t):
    b, hh, l, d = t.shape
    return jnp.swapaxes(t, 1, 2).reshape(b, l, hh * d)


def _to_head_major(t, heads):
    b, l, w = t.shape
    return jnp.swapaxes(t.reshape(b, l, heads, w // heads), 1, 2)


def _swap_halves(t):
    return jnp.concatenate([t[..., HEAD_DIM:], t[..., :HEAD_DIM]], axis=-1)


def _trunk_layer(x, mod, wts, consts, ctx, final_g):
    bsz, seq, _ = x.shape
    x1 = _ffn(x, mod, wts['norm_ffn1'], wts['ffn1_w1'], wts['ffn1_w3'], wts['ffn1_w2'], which=0)
    rope_tabs = consts['rope'] if ctx is not None else None
    proj = _inproj(x1, mod, wts['norm_mix'], wts['w_in'], consts['ones_bd'], wts['gq'], wts['gk'],
                   consts['c_bd'], consts['s_bd'], rope_tabs, ctx is None)
    q, ka, kb, va, vb, pb, pc, ps, wd = proj[:9]

    k_cache = v_cache = None
    if ctx is None:
        k_cache = _to_head_major(proj[9], KV_HEADS_A)
        v_cache = _to_head_major(proj[10], KV_HEADS_A)
        s0 = jnp.zeros((bsz, 2, N_HEADS_C, HEAD_DIM, HEAD_DIM), F32)
    else:
        k_ctx, v_ctx, s0 = ctx
        k_past = _to_lane_dense(k_ctx).astype(BF16)
        v_past = _to_lane_dense(v_ctx).astype(BF16)
        ka = jnp.concatenate([k_past, ka], axis=1)
        kb = jnp.concatenate([_swap_halves(k_past), kb], axis=1)
        ones = jnp.ones_like(v_past)
        va = jnp.concatenate([jnp.concatenate([v_past, ones], axis=-1), va], axis=1)
        vb = jnp.concatenate([jnp.concatenate([_swap_halves(v_past), ones], axis=-1), vb], axis=1)
    o_a = _attention(q, ka, kb, va, vb)

    o_b, qc, kc, vc, gates, gates_t = _conv_prep(pb, pc, ps, wts['conv_b_w'], wts['conv_c_w'],
                                                 consts['ones_bd'], wts['alog_row'], wts['dtb_row'])
    d_f, d_b, s_end = _deltanet(qc, kc, vc, gates, gates_t, s0)

    o_d = _fourier_positions(wd, consts['fourier'], bsz, GROUP_WIDTH)

    x3 = _ffn(x1, mod, wts['norm_ffn2'], wts['ffn2_w1'], wts['ffn2_w3'], wts['ffn2_w2'], which=2,
              mix=(o_a, o_b, (d_f, d_b, pc, consts['ones_bd'], wts['delta_norm']), o_d),
              w_out=wts['w_out'], final_g=final_g)
    return x3, k_cache, v_cache, s_end


def _lane_row(values, start):
    row = jnp.zeros((1, LANES), F32)
    return row.at[0, start:start + values.shape[0]].set(values.astype(F32))


def kernel(x_prompt, x_sample, c, cache_k, cache_v, state_delta, c_ctx, mod_w, mod_b, norm_ffn1, norm_mix, norm_ffn2, ffn1_w1, ffn1_w3, ffn1_w2, ffn2_w1, ffn2_w3, ffn2_w2, w_in, w_out, q_norm, k_norm, conv_b_w, conv_c_w, delta_a_log, delta_dt_bias, delta_norm, final_norm):
    depth = mod_w.shape[0]
    d_model = x_prompt.shape[-1]
    dec_b = x_sample.shape[0]
    assert w_in.shape[-1] == PROJ_MAIN + PROJ_SMALL + GROUP_WIDTH

    n_cond = 1 + dec_b
    rows = -(-n_cond // SUBLANES) * SUBLANES
    cond = jnp.zeros((rows, d_model), F32).at[0].set(c_ctx).at[1:n_cond].set(c)

    idx = np.arange(LANES)
    ones_bd = jnp.asarray((idx[:, None] // HEAD_DIM) == (idx[None, :] // HEAD_DIM), BF16)
    consts_p = {'ones_bd': ones_bd}
    consts_s = {'ones_bd': ones_bd, 'rope': _rope_tables(x_sample.shape[1])}
    consts_p['fourier'], consts_p['c_bd'], consts_p['s_bd'] = _fourier_tables(x_prompt.shape[1])
    consts_s['fourier'], consts_s['c_bd'], consts_s['s_bd'] = _fourier_tables(x_sample.shape[1])

    yp, ys = x_prompt, x_sample
    k_list, v_list, s_list = [], [], []
    for l in range(depth):
        w_in_l = w_in[l]
        small = w_in_l[:, PROJ_MAIN:PROJ_MAIN + PROJ_SMALL]
        w_in_r = jnp.concatenate(
            [w_in_l[:, :PROJ_MAIN], w_in_l[:, PROJ_MAIN + PROJ_SMALL:], small,
             jnp.zeros((d_model, LANES - PROJ_SMALL), F32)], axis=1).astype(BF16)
        wts = {
            'norm_ffn1': norm_ffn1[l], 'norm_mix': norm_mix[l], 'norm_ffn2': norm_ffn2[l],
            'ffn1_w1': ffn1_w1[l].astype(BF16), 'ffn1_w3': ffn1_w3[l].astype(BF16),
            'ffn1_w2': ffn1_w2[l].astype(BF16),
            'ffn2_w1': ffn2_w1[l].astype(BF16), 'ffn2_w3': ffn2_w3[l].astype(BF16),
            'ffn2_w2': ffn2_w2[l].astype(BF16),
            'w_in': w_in_r, 'w_out': w_out[l].astype(BF16),
            'gq': jnp.tile(q_norm[l], 2).reshape(1, LANES),
            'gk': jnp.tile(k_norm[l], 2).reshape(1, LANES),
            'conv_b_w': conv_b_w[l], 'conv_c_w': conv_c_w[l],
            'alog_row': _lane_row(delta_a_log[l].reshape(-1), 8),
            'dtb_row': _lane_row(delta_dt_bias[l].reshape(-1), 8),
            'delta_norm': delta_norm[l],
        }
        mod = _modulation(cond, mod_w, mod_b, l).reshape(rows, N_MOD, d_model)
        fin = final_norm if l == depth - 1 else None
        yp, k_l, v_l, s_l = _trunk_layer(yp, mod[0:1], wts, consts_p, None, fin)
        k_list.append(k_l)
        v_list.append(v_l)
        s_list.append(s_l)
        ys, _, _, _ = _trunk_layer(ys, mod[1:n_cond], wts, consts_s,
                                   (cache_k[:, l], cache_v[:, l], state_delta[:, l]), fin)
    return (yp, ys, jnp.stack(k_list, axis=1), jnp.stack(v_list, axis=1),
            jnp.stack(s_list, axis=1))
```
